```python
import math
import jax
import jax.numpy as jnp
from jax import lax
import numpy as np

D_MODEL = 1024
BATCH = 2
SEQ = 16384
DEPTH = 4

GRID_W = 64
CTX_LEN = 256
BLK = 128
WINDOW = 128
N_BRANCH = 4
BRANCH_W = D_MODEL // 4
D_FF = 4 * D_MODEL
ROPE_BASE = 10000.0
NEG_INF = -1e30
A_HEADS = 4
A_DQK = BRANCH_W // (2 * A_HEADS)
A_DV = BRANCH_W // A_HEADS
B_HEADS = 4
B_D = BRANCH_W // B_HEADS
B_LORA_W = 64
B_LORA_A = 64
B_LORA_G = 128
B_GN_EPS = 64e-5
C_HEADS = 4
C_DK = BRANCH_W // (2 * C_HEADS)
C_DV = BRANCH_W // C_HEADS
D_QHEADS = 4
D_KVHEADS = 2
D_D = BRANCH_W // D_QHEADS

A_COLS = (A_HEADS * 2 * A_DQK, A_HEADS * 2 * A_DQK, A_HEADS * A_DV)
B_COLS = (BRANCH_W, BRANCH_W, BRANCH_W, B_LORA_W, B_LORA_W, B_LORA_A, B_LORA_A, B_LORA_G)
C_COLS = (C_HEADS * C_DK, C_HEADS * C_DK, C_HEADS * C_DV, BRANCH_W)
D_COLS = (D_QHEADS * D_D, D_KVHEADS * D_D, D_KVHEADS * D_D)
GROUP_COLS = (sum(A_COLS), sum(B_COLS), sum(C_COLS), sum(D_COLS), N_BRANCH * D_MODEL)
W_IN_COLS = sum(GROUP_COLS)

kernel_name = 'hybrid_diffusion_parallel_mixer'


def split_cols(x, sizes):
    return jnp.split(x, np.cumsum(sizes)[:-1].tolist(), axis=-1)


def rms_norm(x, g, eps=1e-6):
    xf = x.astype(jnp.float32)
    y = xf * lax.rsqrt(jnp.mean(xf * xf, axis=-1, keepdims=True) + eps)
    return (y * g.astype(jnp.float32)).astype(x.dtype)


def head_norm(x, g, b=None, eps=1e-5, center=True):
    xf = x.astype(jnp.float32)
    if center:
        xf = xf - jnp.mean(xf, axis=-1, keepdims=True)
    y = xf * lax.rsqrt(jnp.mean(xf * xf, axis=-1, keepdims=True) + eps)
    y = y.reshape(x.shape[:-2] + (-1,)) * g.astype(jnp.float32)
    if b is not None:
        y = y + b.astype(jnp.float32)
    return y.astype(x.dtype)


def rope_angles(pos, dim):
    inv_freq = ROPE_BASE ** (-jnp.arange(dim // 2, dtype=jnp.float32) / (dim // 2))
    return pos[:, None] * inv_freq[None, :]


def axial_rope_angles(rows, head_dim):
    row = jnp.repeat(jnp.arange(rows, dtype=jnp.float32), GRID_W)
    col = jnp.tile(jnp.arange(GRID_W, dtype=jnp.float32), rows)
    return rope_angles(row, head_dim // 2), rope_angles(col, head_dim // 2)


def rotate_half(x):
    x1, x2 = jnp.split(x, 2, axis=-1)
    return jnp.concatenate([-x2, x1], axis=-1)


def apply_rope(x, ang):
    full = jnp.concatenate([ang, ang], axis=-1)
    shp = (ang.shape[0],) + (1,) * (x.ndim - 3) + (full.shape[-1],)
    cos = jnp.cos(full).reshape(shp).astype(x.dtype)
    sin = jnp.sin(full).reshape(shp).astype(x.dtype)
    return x * cos + rotate_half(x) * sin


def apply_axial_rope(x, ang_r, ang_c):
    xr, xc = jnp.split(x, 2, axis=-1)
    return jnp.concatenate([apply_rope(xr, ang_r), apply_rope(xc, ang_c)], axis=-1)


def centred_shift(x):
    z = jnp.zeros_like(x[:, :1])
    prev = jnp.concatenate([z, x[:, :-1]], axis=1)
    nxt = jnp.concatenate([x[:, 1:], z], axis=1)
    return 0.5 * (prev + nxt)


def diff_attention(cols_l, cols_x, p, ang_r, ang_c, layer_idx, ctx_out):
    def heads(cols):
        q, k, v = split_cols(cols, A_COLS)
        sh = cols.shape[:2]
        return (q.reshape(sh + (A_HEADS, 2, A_DQK)), k.reshape(sh + (A_HEADS, 2, A_DQK)),
                v.reshape(sh + (A_HEADS, A_DV)))
    ql, kl, vl = heads(cols_l)
    qx, kx, vx = heads(cols_x)
    ql = apply_axial_rope(ql, ang_r, ang_c)
    kl = apply_axial_rope(kl, ang_r, ang_c)
    lam_init = 0.8 - 0.6 * math.exp(-0.3 * layer_idx)
    lq = p['diff_lam_q'].astype(jnp.float32)
    lk = p['diff_lam_k'].astype(jnp.float32)
    lam = jnp.exp(jnp.sum(lq[0] * lk[0])) - jnp.exp(jnp.sum(lq[1] * lk[1])) + lam_init
    scale = A_DQK ** -0.5

    def attend(q, k, v):
        s = jnp.einsum('bqhcd,bkhcd->bhcqk', q, k).astype(jnp.float32) * scale
        pr = jax.nn.softmax(s, axis=-1)
        wgt = (pr[:, :, 0] - lam * pr[:, :, 1]).astype(v.dtype)
        return jnp.einsum('bhqk,bkhd->bqhd', wgt, v)

    B, S = cols_l.shape[:2]
    nb = S // BLK
    k_all = jnp.concatenate([kx, kl], axis=1)
    v_all = jnp.concatenate([vx, vl], axis=1)
    q_blocks = jnp.swapaxes(ql.reshape(B, nb, BLK, A_HEADS, 2, A_DQK), 0, 1)
    o_blocks = lax.map(lambda qb: attend(qb, k_all, v_all), q_blocks)
    o_l = jnp.swapaxes(o_blocks, 0, 1).reshape(B, S, A_HEADS, A_DV)

    def finish(o):
        return head_norm(o, p['diff_subln'], center=False) * (1.0 - lam_init)
    y_l = finish(o_l)
    y_x = finish(attend(qx, kx, vx)) if ctx_out else None
    return y_l, y_x


def rwkv7_scan(r, w, k, v, kk, b, s0, reverse):
    tm = lambda t: jnp.moveaxis(t.astype(jnp.float32), 1, 0)

    def step(S, inp):
        rt, wt, kt, vt, kkt, bt = inp
        sa = jnp.einsum('bhvk,bhk->bhv', S, -kkt)
        S = S * wt[:, :, None, :] + sa[..., None] * bt[:, :, None, :] + vt[..., None] * kt[:, :, None, :]
        return S, jnp.einsum('bhvk,bhk->bhv', S, rt)

    S, y = lax.scan(step, s0, (tm(r), tm(w), tm(k), tm(v), tm(kk), tm(b)), reverse=reverse)
    return S, jnp.moveaxis(y, 0, 1)


def rwkv7_time_mix(cols_l, cols_x, p, ctx_out):
    hd = lambda t: t.reshape(t.shape[:-1] + (B_HEADS, B_D))

    def prep(cols):
        xs = cols + (centred_shift(cols) - cols) * p['rwkv_mu']
        r, k, v, wl_f, wl_b, al_f, al_b, gl = split_cols(xs, B_COLS)
        kk = hd(k * p['rwkv_kk'])
        kk = kk * lax.rsqrt(jnp.maximum(jnp.sum(kk * kk, axis=-1, keepdims=True), 1e-12))
        dirs = []
        for d, (wl, al) in enumerate(((wl_f, al_f), (wl_b, al_b))):
            w = -jax.nn.softplus(-(p['rwkv_w0'][d] + jnp.tanh(wl) @ p['rwkv_w2'][d])) - 0.5
            a = jax.nn.sigmoid(p['rwkv_a0'][d] + al @ p['rwkv_a2'][d])
            kd = k * (1.0 + (a - 1.0) * p['rwkv_ka'])
            dirs.append((hd(jnp.exp(-jnp.exp(w))), hd(kd), kk * hd(a)))
        return hd(r), hd(v), kk, gl, dirs

    rl, vl, kkl, gll, dirs_l = prep(cols_l)
    rx, vx, kkx, glx, dirs_x = prep(cols_x)
    s0 = jnp.zeros((cols_l.shape[0], B_HEADS, B_D, B_D), jnp.float32)
    outs_l, outs_x = [], []
    for d, rev in enumerate((False, True)):
        dec_x, kd_x, b_x = dirs_x[d]
        dec_l, kd_l, b_l = dirs_l[d]
        s_x, y_x = rwkv7_scan(rx, dec_x, kd_x, vx, kkx, b_x, s0, rev)
        _, y_l = rwkv7_scan(rl, dec_l, kd_l, vl, kkl, b_l, s_x, rev)
        outs_l.append((y_l, kd_l))
        outs_x.append((y_x, kd_x))

    def finish(outs, r, v, gl):
        y = head_norm(outs[0][0] + outs[1][0], p['rwkv_lnx_g'], p['rwkv_lnx_b'], eps=B_GN_EPS).astype(v.dtype)
        bonus = sum(jnp.sum(r * kd * p['rwkv_rk'], axis=-1, keepdims=True) * v for _, kd in outs)
        gate = jax.nn.sigmoid(gl) @ p['rwkv_g2']
        return (y + bonus.reshape(y.shape)) * gate

    y_l = finish(outs_l, rl, vl, gll)
    y_x = finish(outs_x, rx, vx, glx) if ctx_out else None
    return y_l, y_x


def retention_chunkwise(q, k, v, log_g, s0):
    B, T, H, _ = q.shape
    n = T // BLK
    chunks = lambda t: jnp.swapaxes(t.astype(jnp.float32).reshape(B, n, BLK, H, t.shape[-1]), 0, 1)
    idx = jnp.arange(BLK, dtype=jnp.float32)
    rel = idx[:, None] - idx[None, :]
    dmat = jnp.where(rel[None] >= 0, jnp.exp(jnp.maximum(rel, 0.0)[None] * log_g[:, None, None]), 0.0)
    q_dec = jnp.exp((idx[:, None] + 1.0) * log_g[None, :])
    k_dec = jnp.exp((BLK - 1.0 - idx[:, None]) * log_g[None, :])
    c_dec = jnp.exp(BLK * log_g)

    def step(S, blk):
        qc, kc, vc = blk
        sc = jnp.einsum('bihd,bjhd->bhij', qc, kc) * dmat[None]
        o = jnp.einsum('bhij,bjhe->bihe', sc, vc) + jnp.einsum('bihd,bhde->bihe', qc * q_dec[None, :, :, None], S)
        S = S * c_dec[None, :, None, None] + jnp.einsum('bjhd,bjhe->bhde', kc * k_dec[None, :, :, None], vc)
        return S, o

    S, o = lax.scan(step, s0, (chunks(q), chunks(k), chunks(v)))
    return S, jnp.swapaxes(o, 0, 1).reshape(B, T, H, v.shape[-1])


def retention(cols_l, cols_x, p, ang, ctx_out):
    def heads(cols):
        q, k, v, g = split_cols(cols, C_COLS)
        sh = cols.shape[:2]
        return (q.reshape(sh + (C_HEADS, C_DK)), k.reshape(sh + (C_HEADS, C_DK)) * (C_DK ** -0.5),
                v.reshape(sh + (C_HEADS, C_DV)), g)
    ql, kl, vl, gl = heads(cols_l)
    qx, kx, vx, gx = heads(cols_x)
    ql = apply_rope(ql, ang)
    kl = apply_rope(kl, ang)
    log_g = jax.nn.log_sigmoid(p['ret_decay'].astype(jnp.float32))
    s0 = jnp.zeros((cols_l.shape[0], C_HEADS, C_DK, C_DV), jnp.float32)
    flip = lambda t: t[:, ::-1]
    s_xf, o_xf = retention_chunkwise(qx, kx, vx, log_g[0], s0)
    _, o_lf = retention_chunkwise(ql, kl, vl, log_g[0], s_xf)
    s_xb, o_xb = retention_chunkwise(flip(qx), flip(kx), flip(vx), log_g[1], s0)
    _, o_lb = retention_chunkwise(flip(ql), flip(kl), flip(vl), log_g[1], s_xb)

    def finish(o, g):
        return head_norm(o, p['ret_gn']).astype(g.dtype) * jax.nn.silu(g)
    y_l = finish(o_lf + flip(o_lb), gl)
    y_x = finish(o_xf + flip(o_xb), gx) if ctx_out else None
    return y_l, y_x


def window_gqa(cols_l, cols_x, p, ang_r, ang_c, ctx_out):
    G = D_QHEADS // D_KVHEADS

    def heads(cols):
        q, k, v = split_cols(cols, D_COLS)
        sh = cols.shape[:2]
        return (q.reshape(sh + (D_QHEADS, D_D)), k.reshape(sh + (D_KVHEADS, D_D)),
                v.reshape(sh + (D_KVHEADS, D_D)))
    ql, kl, vl = heads(cols_l)
    qx, kx, vx = heads(cols_x)
    ql = apply_axial_rope(ql, ang_r, ang_c)
    kl = apply_axial_rope(kl, ang_r, ang_c)
    scale = D_D ** -0.5
    sink = p['win_sink'].astype(jnp.float32).reshape(D_KVHEADS, G)
    B, S = cols_l.shape[:2]
    C = cols_x.shape[1]
    nb = S // BLK
    qb = ql.reshape(B, nb, BLK, D_KVHEADS, G, D_D)

    def band(t):
        tp = jnp.pad(t, ((0, 0), (BLK, BLK), (0, 0), (0, 0))).reshape(B, nb + 2, BLK, D_KVHEADS, D_D)
        return jnp.concatenate([tp[:, :-2], tp[:, 1:-1], tp[:, 2:]], axis=2)
    kb, vb = band(kl), band(vl)
    start = jnp.arange(nb)[:, None] * BLK
    qpos = start + jnp.arange(BLK)[None, :]
    kpos = start - BLK + jnp.arange(3 * BLK)[None, :]
    valid = ((jnp.abs(qpos[:, :, None] - kpos[:, None, :]) <= WINDOW)
             & (kpos[:, None, :] >= 0) & (kpos[:, None, :] < S))
    s_band = jnp.einsum('bnqhgd,bnkhd->bnhgqk', qb, kb).astype(jnp.float32) * scale
    s_band = jnp.where(valid[None, :, None, None], s_band, NEG_INF)
    s_ctx = jnp.einsum('bnqhgd,bchd->bnhgqc', qb, kx).astype(jnp.float32) * scale
    s_sink = jnp.broadcast_to(sink[None, None, :, :, None, None], s_ctx.shape[:-1] + (1,))
    pr = jax.nn.softmax(jnp.concatenate([s_ctx, s_band, s_sink], axis=-1), axis=-1)
    p_ctx = pr[..., :C].astype(vl.dtype)
    p_band = pr[..., C:C + 3 * BLK].astype(vl.dtype)
    o = (jnp.einsum('bnhgqc,bchd->bnqhgd', p_ctx, vx)
         + jnp.einsum('bnhgqk,bnkhd->bnqhgd', p_band, vb))
    y_l = o.reshape(B, S, D_QHEADS * D_D)
    if ctx_out:
        s = jnp.einsum('bqhgd,bkhd->bhgqk', qx.reshape(B, C, D_KVHEADS, G, D_D), kx).astype(jnp.float32) * scale
        s_sink_x = jnp.broadcast_to(sink[None, :, :, None, None], s.shape[:-1] + (1,))
        px = jax.nn.softmax(jnp.concatenate([s, s_sink_x], axis=-1), axis=-1)[..., :C].astype(vx.dtype)
        y_x = jnp.einsum('bhgqk,bkhd->bqhgd', px, vx).reshape(B, C, D_QHEADS * D_D)
    else:
        y_x = None
    return y_l, y_x


def merge_branches(ys, gates, p):
    m = sum(jax.nn.sigmoid(gates[..., n * D_MODEL:(n + 1) * D_MODEL]) * (y @ p['w_branch'][n])
            for n, y in enumerate(ys))
    return m @ p['w_out']


def token_mixer(h_lat, h_ctx, p, rope, layer_idx, ctx_out):
    a_r, a_c, d_r, d_c, ret_ang = rope
    a_l, b_l, c_l, d_l, gate_l = split_cols(h_lat @ p['w_in'], GROUP_COLS)
    a_x, b_x, c_x, d_x, gate_x = split_cols(h_ctx @ p['w_in'], GROUP_COLS)
    ya_l, ya_x = diff_attention(a_l, a_x, p, a_r, a_c, layer_idx, ctx_out)
    yb_l, yb_x = rwkv7_time_mix(b_l, b_x, p, ctx_out)
    yc_l, yc_x = retention(c_l, c_x, p, ret_ang, ctx_out)
    yd_l, yd_x = window_gqa(d_l, d_x, p, d_r, d_c, ctx_out)
    out_l = merge_branches((ya_l, yb_l, yc_l, yd_l), gate_l, p)
    out_x = merge_branches((ya_x, yb_x, yc_x, yd_x), gate_x, p) if ctx_out else None
    return out_l, out_x


def sq_relu_mlp(h, w_up, w_down):
    return jnp.square(jax.nn.relu(h @ w_up)) @ w_down


def setup_inputs(seed: int = 0) -> dict:
    key = jax.random.key(seed)
    keys = iter(jax.random.split(key, 48))
    nrm = lambda shape, s: s * jax.random.normal(next(keys), shape, jnp.float32)
    gain = lambda shape: 1.0 + nrm(shape, 0.02)
    L, D = DEPTH, D_MODEL
    decay_speed = jnp.asarray(-6.0 + 5.0 * np.linspace(0.0, 1.0, BRANCH_W) ** 0.85, jnp.float32)
    ret_base = jnp.asarray(np.log(2.0 ** (5.0 + np.arange(C_HEADS)) - 1.0), jnp.float32)
    return {
        'x': nrm((BATCH, SEQ, D), 1.0),
        'c': nrm((BATCH, D), 1.0),
        'ctx': nrm((BATCH, CTX_LEN, D), 1.0),
        'c_ctx': nrm((D,), 1.0),
        'ada_w': nrm((L, D, 6 * D), 0.5 * D ** -0.5),
        'ada_b': nrm((L, 6 * D), 0.02),
        'norm_pre_mix': gain((L, D)),
        'norm_post_mix': gain((L, D)),
        'norm_pre_mlp': gain((L, D)),
        'norm_post_mlp': gain((L, D)),
        'w_in': nrm((L, D, W_IN_COLS), D ** -0.5),
        'diff_lam_q': nrm((L, 2, A_DQK), 0.1),
        'diff_lam_k': nrm((L, 2, A_DQK), 0.1),
        'diff_subln': gain((L, A_HEADS * A_DV)),
        'rwkv_mu': jax.random.uniform(next(keys), (L, sum(B_COLS)), jnp.float32),
        'rwkv_w0': decay_speed[None, None, :] + nrm((L, 2, BRANCH_W), 0.1),
        'rwkv_w2': nrm((L, 2, B_LORA_W, BRANCH_W), 0.1),
        'rwkv_a0': nrm((L, 2, BRANCH_W), 0.1),
        'rwkv_a2': nrm((L, 2, B_LORA_A, BRANCH_W), 0.1),
        'rwkv_g2': nrm((L, B_LORA_G, BRANCH_W), B_LORA_G ** -0.5),
        'rwkv_kk': 0.85 + nrm((L, BRANCH_W), 0.02),
        'rwkv_ka': 1.0 + nrm((L, BRANCH_W), 0.02),
        'rwkv_rk': nrm((L, B_HEADS, B_D), 0.1),
        'rwkv_lnx_g': gain((L, BRANCH_W)),
        'rwkv_lnx_b': nrm((L, BRANCH_W), 0.02),
        'ret_decay': ret_base[None, None, :] + nrm((L, 2, C_HEADS), 0.05),
        'ret_gn': gain((L, C_HEADS * C_DV)),
        'win_sink': nrm((L, D_QHEADS), 0.5),
        'w_branch': nrm((L, N_BRANCH, BRANCH_W, D), BRANCH_W ** -0.5),
        'w_out': nrm((L, D, D), D ** -0.5),
        'w_up': nrm((L, D, D_FF), D ** -0.5),
        'w_down': nrm((L, D_FF, D), D_FF ** -0.5),
    }


def reference(x, c, ctx, c_ctx, ada_w, ada_b, norm_pre_mix, norm_post_mix, norm_pre_mlp, norm_post_mlp,
              w_in, diff_lam_q, diff_lam_k, diff_subln, rwkv_mu, rwkv_w0, rwkv_w2, rwkv_a0, rwkv_a2, rwkv_g2,
              rwkv_kk, rwkv_ka, rwkv_rk, rwkv_lnx_g, rwkv_lnx_b, ret_decay, ret_gn, win_sink,
              w_branch, w_out, w_up, w_down):
    n_lat = x.shape[1]
    rows = n_lat // GRID_W
    a_r, a_c = axial_rope_angles(rows, A_DQK)
    d_r, d_c = axial_rope_angles(rows, D_D)
    ret_ang = rope_angles(jnp.arange(n_lat, dtype=jnp.float32), C_DK)
    rope = (a_r, a_c, d_r, d_c, ret_ang)
    params = dict(w_in=w_in, diff_lam_q=diff_lam_q, diff_lam_k=diff_lam_k, diff_subln=diff_subln,
                  rwkv_mu=rwkv_mu, rwkv_w0=rwkv_w0, rwkv_w2=rwkv_w2, rwkv_a0=rwkv_a0, rwkv_a2=rwkv_a2,
                  rwkv_g2=rwkv_g2, rwkv_kk=rwkv_kk, rwkv_ka=rwkv_ka, rwkv_rk=rwkv_rk,
                  rwkv_lnx_g=rwkv_lnx_g, rwkv_lnx_b=rwkv_lnx_b, ret_decay=ret_decay, ret_gn=ret_gn,
                  win_sink=win_sink, w_branch=w_branch, w_out=w_out)
    cx = ctx
    for l in range(DEPTH):
        p = {name: arr[l] for name, arr in params.items()}
        ctx_out = l < DEPTH - 1
        mod_l = (jax.nn.silu(c) @ ada_w[l] + ada_b[l])[:, None, :]
        mod_x = (jax.nn.silu(c_ctx) @ ada_w[l] + ada_b[l])[None, None, :]
        sh1, sc1, gt1, sh2, sc2, gt2 = jnp.split(mod_l, 6, axis=-1)
        xsh1, xsc1, xgt1, xsh2, xsc2, xgt2 = jnp.split(mod_x, 6, axis=-1)
        h_l = rms_norm(x, norm_pre_mix[l]) * (1.0 + sc1) + sh1
        h_x = rms_norm(cx, norm_pre_mix[l]) * (1.0 + xsc1) + xsh1
        m_l, m_x = token_mixer(h_l, h_x, p, rope, l, ctx_out)
        x = x + gt1 * rms_norm(m_l, norm_post_mix[l])
        h_l = rms_norm(x, norm_pre_mlp[l]) * (1.0 + sc2) + sh2
        x = x + gt2 * rms_norm(sq_relu_mlp(h_l, w_up[l], w_down[l]), norm_post_mlp[l])
        if ctx_out:
            cx = cx + xgt1 * rms_norm(m_x, norm_post_mix[l])
            h_x = rms_norm(cx, norm_pre_mlp[l]) * (1.0 + xsc2) + xsh2
            cx = cx + xgt2 * rms_norm(sq_relu_mlp(h_x, w_up[l], w_down[l]), norm_post_mlp[l])
    return x
```

```python
import functools
import math

import numpy as np
import jax
import jax.numpy as jnp
from jax import lax
from jax.experimental import pallas as pl
from jax.experimental.pallas import tpu as pltpu

F32 = jnp.float32
BF16 = jnp.bfloat16
HI = lax.Precision.HIGHEST

GRID_W = 64
ROPE_BASE = 10000.0
NEG_INF = -1e30
WINDOW = 128
N_HEADS = 4
HEAD_W = 64
BRANCH_W = N_HEADS * HEAD_W
RW_CHUNK = 64
RET_CHUNK = 128
RW_GN_EPS = 64e-5
VMEM_LIMIT = 56 * 1024 * 1024


def _cparams(sem, vmem=None):
    return pltpu.CompilerParams(dimension_semantics=sem, vmem_limit_bytes=vmem or VMEM_LIMIT)


def _dotf(a, b):
    return jnp.dot(a, b, precision=HI, preferred_element_type=F32)


def _dotb(a, b):
    return jnp.dot(a.astype(BF16), b.astype(BF16), preferred_element_type=F32)


def _dot_nt(a, b, precision=None):
    return lax.dot_general(a, b, (((1,), (1,)), ((), ())), precision=precision,
                           preferred_element_type=F32)


def _dot_tn(a, b, precision=None):
    return lax.dot_general(a, b, (((0,), (0,)), ((), ())), precision=precision,
                           preferred_element_type=F32)


def _sigmoid(x):
    return 1.0 / (1.0 + jnp.exp(-x))


def _softplus(x):
    return jnp.maximum(x, 0.0) + jnp.log1p(jnp.exp(-jnp.abs(x)))


def _rms(x, eps):
    return x * lax.rsqrt(jnp.mean(x * x, axis=-1, keepdims=True) + eps)


def _const_spec(shape, index):
    return pl.BlockSpec(shape, index, pipeline_mode=pl.Buffered(1))


def _ada_kernel(c_ref, w_ref, b_ref, o_ref):
    c = c_ref[...]
    o_ref[0] = _dotf(c * _sigmoid(c), w_ref[0]) + b_ref[0]


def _ada_mod(cc, ada_w, ada_b):
    L, D, N = ada_w.shape
    tn = N // 4
    return pl.pallas_call(
        _ada_kernel,
        grid=(L, N // tn),
        in_specs=[pl.BlockSpec((8, D), lambda l, n: (0, 0)),
                  pl.BlockSpec((1, D, tn), lambda l, n: (l, 0, n)),
                  pl.BlockSpec((1, 1, tn), lambda l, n: (l, 0, n))],
        out_specs=pl.BlockSpec((1, 8, tn), lambda l, n: (l, 0, n)),
        out_shape=jax.ShapeDtypeStruct((L, 8, N), F32),
        compiler_params=_cparams(("arbitrary", "arbitrary")),
        name="ada_mod",
    )(cc, ada_w, ada_b.reshape(L, 1, N))


def _win_kernel(x_ref, g_ref, mod_ref, *refs, n_out):
    w_refs, o_refs = refs[:n_out], refs[n_out:]
    h = _rms(x_ref[0], 1e-6) * g_ref[...]
    h = h * (1.0 + mod_ref[1:2, :]) + mod_ref[0:1, :]
    hb = h.astype(BF16)
    for w_ref, o_ref in zip(w_refs, o_refs):
        o_ref[0] = jnp.dot(hb, w_ref[...], preferred_element_type=F32).astype(o_ref.dtype)


def _input_proj(xs, gain, mods, weights, l, tm, nct):
    B, T, D = xs.shape
    n_out = len(weights)
    in_specs = [pl.BlockSpec((1, tm, D), lambda b, i: (b, i, 0)),
                pl.BlockSpec((None, 1, D), lambda b, i: (l, 0, 0)),
                pl.BlockSpec((None, None, None, 8, D),
                             lambda b, i: (l, b, jnp.minimum(i // nct, 1), 0, 0))]
    out_specs, out_shape = [], []
    for w in weights:
        n = w.shape[-1]
        in_specs.append(_const_spec((None, D, n), lambda b, i: (l, 0, 0)))
        out_specs.append(pl.BlockSpec((1, tm, n), lambda b, i: (b, i, 0)))
        out_shape.append(jax.ShapeDtypeStruct((B, T, n), F32))
    return pl.pallas_call(
        functools.partial(_win_kernel, n_out=n_out),
        grid=(B, T // tm),
        in_specs=in_specs, out_specs=out_specs, out_shape=out_shape,
        compiler_params=_cparams(("parallel", "parallel")),
        name="input_proj",
    )(xs, gain, mods, *weights)


def _rope_tables(pos_sets, lanes, group, n_ctx):
    n_sets = len(pos_sets)
    sub = group // n_sets
    half = sub // 2
    angs = []
    for pos in pos_sets:
        inv_freq = ROPE_BASE ** (-jnp.arange(half, dtype=F32) / half)
        ang = pos[:, None] * inv_freq[None, :]
        angs.append(jnp.concatenate([ang, ang], axis=-1))
    ang = jnp.concatenate(angs, axis=-1)
    ang = jnp.tile(ang, (1, lanes // group))
    ang = jnp.concatenate([jnp.zeros((n_ctx, lanes), F32), ang], axis=0)
    first = (np.arange(lanes) % sub) < half
    cos, sin = jnp.cos(ang), jnp.sin(ang)
    sin_minus = jnp.where(first[None, :], -sin, 0.0)
    sin_plus = jnp.where(first[None, :], 0.0, sin)
    return cos, sin_minus, sin_plus, half


def _rope(x, cos, sin_minus, sin_plus, half):
    n = x.shape[-1]
    return x * cos + pltpu.roll(x, n - half, 1) * sin_minus + pltpu.roll(x, half, 1) * sin_plus


def _aprep_kernel(c_ref, cos_ref, sm_ref, sp_ref, q_ref, k_ref, v_ref, *, half, scale):
    c = c_ref[0]
    cos, sm, sp = cos_ref[...], sm_ref[...], sp_ref[...]
    q = _rope(c[:, 0:BRANCH_W], cos, sm, sp, half) * scale
    k = _rope(c[:, BRANCH_W:2 * BRANCH_W], cos, sm, sp, half)
    v = c[:, 2 * BRANCH_W:3 * BRANCH_W]
    lane = lax.broadcasted_iota(jnp.int32, q.shape, 1)
    comp = (lane >> 5) & 1
    q0 = jnp.where(comp == 0, q, 0.0).astype(BF16)
    q1 = jnp.where(comp == 1, q, 0.0).astype(BF16)
    kb, vb = k.astype(BF16), v.astype(BF16)
    for h in range(N_HEADS):
        sl = slice(h * HEAD_W, (h + 1) * HEAD_W)
        q_ref[0, h, 0] = q0[:, sl]
        q_ref[0, h, 1] = q1[:, sl]
        k_ref[0, h] = kb[:, sl]
        v_ref[0, h] = vb[:, sl]


def _diff_prep(cols, tabs, tm):
    B, T, _ = cols.shape
    cos, sm, sp, half = tabs
    tab_spec = pl.BlockSpec((tm, BRANCH_W), lambda b, i: (i, 0))
    return pl.pallas_call(
        functools.partial(_aprep_kernel, half=half, scale=(HEAD_W // 2) ** -0.5),
        grid=(B, T // tm),
        in_specs=[pl.BlockSpec((1, tm, 3 * BRANCH_W), lambda b, i: (b, i, 0)),
                  tab_spec, tab_spec, tab_spec],
        out_specs=[pl.BlockSpec((1, N_HEADS, 2, tm, HEAD_W), lambda b, i: (b, 0, 0, i, 0)),
                   pl.BlockSpec((1, N_HEADS, tm, HEAD_W), lambda b, i: (b, 0, i, 0)),
                   pl.BlockSpec((1, N_HEADS, tm, HEAD_W), lambda b, i: (b, 0, i, 0))],
        out_shape=[jax.ShapeDtypeStruct((B, N_HEADS, 2, T, HEAD_W), BF16),
                   jax.ShapeDtypeStruct((B, N_HEADS, T, HEAD_W), BF16),
                   jax.ShapeDtypeStruct((B, N_HEADS, T, HEAD_W), BF16)],
        compiler_params=_cparams(("parallel", "parallel")),
        name="diff_prep",
    )(cols, cos, sm, sp)


def _flash_kernel(lq_ref, lk_ref, g_ref, q_ref, k_ref, v_ref, o_ref, m_sc, l_sc, acc_sc, *,
                  tq, tk, n_ctx_q, n_ctx_kv, n_kv, lam_init):
    qi = pl.program_id(2)
    q = q_ref[0, 0].reshape(2 * tq, HEAD_W)
    m_sc[...] = jnp.full(m_sc.shape, NEG_INF, F32)
    l_sc[...] = jnp.zeros(l_sc.shape, F32)
    acc_sc[...] = jnp.zeros(acc_sc.shape, F32)
    n_steps = jnp.where(qi < n_ctx_q, n_ctx_kv, n_kv)

    def body(j, carry):
        off = pl.multiple_of(j * tk, tk)
        kc = k_ref[0, 0, pl.ds(off, tk), :]
        vc = v_ref[0, 0, pl.ds(off, tk), :]
        s = _dot_nt(q, kc)
        m_old = m_sc[...]
        m_new = jnp.maximum(m_old, jnp.max(s, axis=1, keepdims=True))
        alpha = jnp.exp(m_old - m_new)
        p = jnp.exp(s - m_new)
        l_sc[...] = l_sc[...] * alpha + jnp.sum(p, axis=1, keepdims=True)
        acc_sc[...] = acc_sc[...] * alpha + jnp.dot(p.astype(BF16), vc, preferred_element_type=F32)
        m_sc[...] = m_new
        return carry

    lax.fori_loop(0, n_steps, body, 0)
    o = acc_sc[...] / l_sc[...]
    e0 = jnp.exp(jnp.sum(lq_ref[0:1, :] * lk_ref[0:1, :], axis=1, keepdims=True))
    e1 = jnp.exp(jnp.sum(lq_ref[1:2, :] * lk_ref[1:2, :], axis=1, keepdims=True))
    lam = e0 - e1 + lam_init
    y = o[:tq] - lam * o[tq:]
    o_ref[0, 0] = _rms(y, 1e-5) * g_ref[...] * (1.0 - lam_init)


def _diff_attention(q, k, v, lam_q, lam_k, subln, l, tq, tk, n_ctx):
    B, H, _, T, _ = q.shape
    dqk = lam_q.shape[-1]
    lam_init = 0.8 - 0.6 * math.exp(-0.3 * l)
    kern = functools.partial(_flash_kernel, tq=tq, tk=tk, n_ctx_q=n_ctx // tq,
                             n_ctx_kv=n_ctx // tk, n_kv=T // tk, lam_init=lam_init)
    return pl.pallas_call(
        kern,
        grid=(B, H, T // tq),
        in_specs=[pl.BlockSpec((None, 2, dqk), lambda b, h, i: (l, 0, 0)),
                  pl.BlockSpec((None, 2, dqk), lambda b, h, i: (l, 0, 0)),
                  pl.BlockSpec((None, None, 1, HEAD_W), lambda b, h, i: (l, h, 0, 0)),
                  pl.BlockSpec((1, 1, 2, tq, HEAD_W), lambda b, h, i: (b, h, 0, i, 0)),
                  pl.BlockSpec((1, 1, T, HEAD_W), lambda b, h, i: (b, h, 0, 0)),
                  pl.BlockSpec((1, 1, T, HEAD_W), lambda b, h, i: (b, h, 0, 0))],
        out_specs=pl.BlockSpec((1, 1, tq, HEAD_W), lambda b, h, i: (b, h, i, 0)),
        out_shape=jax.ShapeDtypeStruct((B, H, T, HEAD_W), F32),
        scratch_shapes=[pltpu.VMEM((2 * tq, 1), F32), pltpu.VMEM((2 * tq, 1), F32),
                        pltpu.VMEM((2 * tq, HEAD_W), F32)],
        compiler_params=_cparams(("parallel", "parallel", "arbitrary")),
        name="diff_attention",
    )(lam_q, lam_k, subln, q, k, v)


def _rwprep_kernel(c_ref, p_ref, n_ref, mu_ref, kk_ref, ka_ref, rk_ref, w0_ref, a0_ref,
                   w2_ref, a2_ref, g2_ref, ones_ref, rvk_ref, dir_ref, aux_ref, *, tm, n_ctx, T):
    i = pl.program_id(1)
    c = c_ref[0]
    t0 = i * tm
    seg_start = jnp.logical_or(t0 == 0, t0 == n_ctx)
    seg_end = jnp.logical_or(t0 + tm == n_ctx, t0 + tm == T)
    pv = jnp.where(seg_start, 0.0, p_ref[0][7:8, :])
    nx = jnp.where(seg_end, 0.0, n_ref[0][0:1, :])
    row = lax.broadcasted_iota(jnp.int32, c.shape, 0)
    prev = jnp.where(row == 0, pv, pltpu.roll(c, 1, 0))
    nxt = jnp.where(row == tm - 1, nx, pltpu.roll(c, tm - 1, 0))
    xs = c + (0.5 * (prev + nxt) - c) * mu_ref[...]
    W = BRANCH_W
    r, k, v = xs[:, 0:W], xs[:, W:2 * W], xs[:, 2 * W:3 * W]
    lo, gl = xs[:, 3 * W:4 * W], xs[:, 4 * W:]
    ones_bd = ones_ref[...]
    kk = k * kk_ref[...]
    kk = kk * lax.rsqrt(jnp.maximum(_dotf(kk * kk, ones_bd), 1e-12))
    wpre = _dotf(jnp.tanh(lo), w2_ref[...])
    apre = _dotf(lo, a2_ref[...])
    gate = _dotf(_sigmoid(gl), g2_ref[...])
    rb = jnp.zeros_like(r)
    for d in range(2):
        z = w0_ref[d:d + 1, :] + wpre[:, d * W:(d + 1) * W]
        w = -_softplus(-z) - 0.5
        a = _sigmoid(a0_ref[d:d + 1, :] + apre[:, d * W:(d + 1) * W])
        kd = k * (1.0 + (a - 1.0) * ka_ref[...])
        dir_ref[0, d, :, 0:W] = -jnp.exp(w)
        dir_ref[0, d, :, W:2 * W] = kd
        dir_ref[0, d, :, 2 * W:3 * W] = kk * a
        rb = rb + r * kd * rk_ref[...]
    rvk_ref[0, :, 0:W] = r
    rvk_ref[0, :, W:2 * W] = v
    rvk_ref[0, :, 2 * W:3 * W] = kk
    aux_ref[0, :, 0:W] = _dotf(rb, ones_bd) * v
    aux_ref[0, :, W:2 * W] = gate


def _rwkv_prep(cols, prm, l, tm, n_ctx):
    B, T, NB = cols.shape
    W = BRANCH_W
    nblk8 = T // 8
    r8 = tm // 8
    vec = lambda n: pl.BlockSpec((None, 1, n), lambda b, i: (l, 0, 0))
    pair = lambda n: pl.BlockSpec((None, 2, n), lambda b, i: (l, 0, 0))
    mat = lambda m, n: _const_spec((None, m, n), lambda b, i: (l, 0, 0))
    return pl.pallas_call(
        functools.partial(_rwprep_kernel, tm=tm, n_ctx=n_ctx, T=T),
        grid=(B, T // tm),
        in_specs=[pl.BlockSpec((1, tm, NB), lambda b, i: (b, i, 0)),
                  pl.BlockSpec((1, 8, NB), lambda b, i: (b, jnp.maximum(i * r8 - 1, 0), 0)),
                  pl.BlockSpec((1, 8, NB), lambda b, i: (b, jnp.minimum((i + 1) * r8, nblk8 - 1), 0)),
                  vec(NB), vec(W), vec(W), vec(W), pair(W), pair(W),
                  mat(W, 2 * W), mat(W, 2 * W), mat(NB - 4 * W, W),
                  _const_spec((W, W), lambda b, i: (0, 0))],
        out_specs=[pl.BlockSpec((1, tm, 3 * W), lambda b, i: (b, i, 0)),
                   pl.BlockSpec((1, 2, tm, 3 * W), lambda b, i: (b, 0, i, 0)),
                   pl.BlockSpec((1, tm, 2 * W), lambda b, i: (b, i, 0))],
        out_shape=[jax.ShapeDtypeStruct((B, T, 3 * W), F32),
                   jax.ShapeDtypeStruct((B, 2, T, 3 * W), F32),
                   jax.ShapeDtypeStruct((B, T, 2 * W), F32)],
        compiler_params=_cparams(("parallel", "parallel")),
        name="rwkv_prep",
    )(cols, cols, cols, prm["mu"], prm["kk"], prm["ka"], prm["rk"], prm["w0"], prm["a0"],
      prm["w2"], prm["a2"], prm["g2"], prm["ones_bd"])


def _rwscan_kernel(rvk_ref, dir_ref, y_ref, s_ref):
    C, W = RW_CHUNK, BRANCH_W
    d = pl.program_id(1)
    s = pl.program_id(2)

    @pl.when(s == 0)
    def _():
        s_ref[...] = jnp.zeros(s_ref.shape, F32)

    r, v, kk = rvk_ref[0, :, 0:W], rvk_ref[0, :, W:2 * W], rvk_ref[0, :, 2 * W:3 * W]
    lw, kd, b = dir_ref[0, 0, :, 0:W], dir_ref[0, 0, :, W:2 * W], dir_ref[0, 0, :, 2 * W:3 * W]
    sgn = 1 - 2 * d
    ti = lax.broadcasted_iota(jnp.int32, (C, C), 0)
    tj = lax.broadcasted_iota(jnp.int32, (C, C), 1)
    tri = jnp.where((tj - ti) * sgn <= 0, 1.0, 0.0)
    log_g = _dotf(tri, lw)
    log_gc = jnp.sum(lw, axis=0, keepdims=True)
    g_inv = jnp.exp(-log_g)
    kk_t = kk * jnp.exp(log_g - lw)
    r_t = r * jnp.exp(log_g)
    b_t, kd_t = b * g_inv, kd * g_inv
    g_tail = jnp.exp(log_gc - log_g)
    b_h, kd_h = b * g_tail, kd * g_tail

    lane = lax.broadcasted_iota(jnp.int32, (1, W), 1)
    head_of_lane = lane >> 6
    stack = lambda x: jnp.concatenate(
        [jnp.where(head_of_lane == h, x, 0.0) for h in range(N_HEADS)], axis=0)
    tile = lambda x: jnp.concatenate([x] * N_HEADS, axis=0)

    ri = lax.broadcasted_iota(jnp.int32, (W, W), 0)
    ci = lax.broadcasted_iota(jnp.int32, (W, W), 1)
    same_head = (ri >> 6) == (ci >> 6)
    dt = ((ci & (C - 1)) - (ri & (C - 1))) * sgn
    strict = jnp.logical_and(same_head, dt < 0)
    incl = jnp.logical_and(same_head, dt <= 0)

    left = jnp.concatenate([stack(kk_t), stack(r_t)], axis=0)
    right = jnp.concatenate([stack(b_t), stack(kd_t)], axis=0)
    g = _dot_nt(left, right, HI)
    m_sl = jnp.where(strict, g[:W, :W], 0.0)
    n_sl = jnp.where(strict, g[:W, W:], 0.0)
    p_l = jnp.where(incl, g[W:, :W], 0.0)
    q_l = jnp.where(incl, g[W:, W:], 0.0)

    s0 = s_ref[...]
    g2 = _dotf(jnp.concatenate([kk_t, r_t], axis=0), s0)
    vs = tile(v)
    rhs = -(tile(g2[:C]) + _dotf(n_sl, vs))

    eye = jnp.where(ri == ci, 1.0, 0.0)
    d0 = jnp.where((ri >> 4) == (ci >> 4), m_sl, 0.0)
    a2 = _dotf(d0, d0)
    a4 = _dotf(a2, a2)
    a8 = _dotf(a4, a4)
    t = eye - d0
    t = t + _dotf(t, a2)
    t = t + _dotf(t, a4)
    t = t + _dotf(t, a8)
    c1 = jnp.where(jnp.logical_and((ri >> 5) == (ci >> 5), (ri >> 4) != (ci >> 4)), m_sl, 0.0)
    t = t - _dotf(t, _dotf(c1, t))
    c2 = jnp.where((ri >> 5) != (ci >> 5), m_sl, 0.0)
    t = t - _dotf(t, _dotf(c2, t))

    sa = _dotf(t, rhs)
    sav = jnp.concatenate([sa, vs], axis=0)
    ys = tile(g2[C:]) + _dotf(jnp.concatenate([p_l, q_l], axis=1), sav)
    y = jnp.zeros((C, W), F32)
    for h in range(N_HEADS):
        y = y + jnp.where(head_of_lane == h, ys[h * C:(h + 1) * C], 0.0)
    y_ref[0, 0] = y

    ds = _dot_tn(jnp.concatenate([stack(b_h), stack(kd_h)], axis=0), sav, HI)
    gc_col = jnp.sum(jnp.where(ri == ci, jnp.exp(log_gc), 0.0), axis=1, keepdims=True)
    s_ref[...] = jnp.where(same_head, s0 * gc_col + ds, 0.0)


def _chunk_order(s, d, n_ctx_chunks, n_chunks):
    bwd = jnp.where(s < n_ctx_chunks, n_ctx_chunks - 1 - s, n_chunks - 1 - (s - n_ctx_chunks))
    return jnp.where(d == 0, s, bwd)


def _rwkv_scan(rvk, dirs, n_ctx):
    B, T, _ = rvk.shape
    C, W = RW_CHUNK, BRANCH_W
    nc, ncx = T // C, n_ctx // C
    ch = lambda s, d: _chunk_order(s, d, ncx, nc)
    return pl.pallas_call(
        _rwscan_kernel,
        grid=(B, 2, nc),
        in_specs=[pl.BlockSpec((1, C, 3 * W), lambda b, d, s: (b, ch(s, d), 0)),
                  pl.BlockSpec((1, 1, C, 3 * W), lambda b, d, s: (b, d, ch(s, d), 0))],
        out_specs=pl.BlockSpec((1, 1, C, W), lambda b, d, s: (b, d, ch(s, d), 0)),
        out_shape=jax.ShapeDtypeStruct((B, 2, T, W), F32),
        scratch_shapes=[pltpu.VMEM((W, W), F32)],
        compiler_params=_cparams(("parallel", "parallel", "arbitrary")),
        name="rwkv_scan",
    )(rvk, dirs)


def _ret_kernel(c_ref, cos_ref, sm_ref, sp_ref, dec_ref, o_ref, s_ref, *, half, scale):
    C, W = RET_CHUNK, BRANCH_W
    QW = W // 2
    d = pl.program_id(1)
    s = pl.program_id(2)

    @pl.when(s == 0)
    def _():
        s_ref[...] = jnp.zeros(s_ref.shape, F32)

    c = c_ref[0]
    cos, sm, sp = cos_ref[...], sm_ref[...], sp_ref[...]
    q = _rope(c[:, 0:QW], cos, sm, sp, half)
    k = _rope(c[:, QW:2 * QW], cos, sm, sp, half) * scale
    v = c[:, 2 * QW:2 * QW + W]
    log_g = -_softplus(-dec_ref[...])
    lane_q = lax.broadcasted_iota(jnp.int32, (1, QW), 1) >> 5
    lane_v = lax.broadcasted_iota(jnp.int32, (1, W), 1) >> 6
    lg_q = jnp.zeros((1, QW), F32)
    lg_v = jnp.zeros((1, W), F32)
    for h in range(N_HEADS):
        lg_q = jnp.where(lane_q == h, log_g[:, h:h + 1], lg_q)
        lg_v = jnp.where(lane_v == h, log_g[:, h:h + 1], lg_v)
    fwd = d == 0
    idx = lax.broadcasted_iota(jnp.int32, (C, 1), 0).astype(F32)
    q_dec = jnp.exp(jnp.where(fwd, idx + 1.0, C - idx) * lg_q)
    k_dec = jnp.exp(jnp.where(fwd, C - 1.0 - idx, idx) * lg_q)
    ti = lax.broadcasted_iota(jnp.int32, (C, C), 0)
    tj = lax.broadcasted_iota(jnp.int32, (C, C), 1)
    rel = ((ti - tj) * (1 - 2 * d)).astype(F32)

    s0 = s_ref[...]
    o = _dotb(q * q_dec, s0)
    kb = k.astype(BF16)
    vb = v.astype(BF16)
    for h in range(N_HEADS):
        dm = jnp.where(rel >= 0, jnp.exp(jnp.maximum(rel, 0.0) * log_g[:, h:h + 1]), 0.0)
        qh = jnp.where(lane_q == h, q, 0.0).astype(BF16)
        sc = _dot_nt(qh, kb) * dm
        o = o + jnp.where(lane_v == h, jnp.dot(sc.astype(BF16), vb, preferred_element_type=F32), 0.0)
    o_ref[0, 0] = o

    ri = lax.broadcasted_iota(jnp.int32, (QW, W), 0) >> 5
    ci = lax.broadcasted_iota(jnp.int32, (QW, W), 1) >> 6
    upd = _dot_tn((k * k_dec).astype(BF16), vb)
    s_ref[...] = jnp.where(ri == ci, s0 * jnp.exp(C * lg_v) + upd, 0.0)


def _retention(cols, tabs, decay, l, n_ctx):
    B, T, NCOL = cols.shape
    C, W = RET_CHUNK, BRANCH_W
    cos, sm, sp, half = tabs
    nc, ncx = T // C, n_ctx // C
    ch = lambda s, d: _chunk_order(s, d, ncx, nc)
    tab_spec = pl.BlockSpec((C, W // 2), lambda b, d, s: (ch(s, d), 0))
    return pl.pallas_call(
        functools.partial(_ret_kernel, half=half, scale=(HEAD_W // 2) ** -0.5),
        grid=(B, 2, nc),
        in_specs=[pl.BlockSpec((1, C, NCOL), lambda b, d, s: (b, ch(s, d), 0)),
                  tab_spec, tab_spec, tab_spec,
                  pl.BlockSpec((None, None, 1, N_HEADS), lambda b, d, s: (l, d, 0, 0))],
        out_specs=pl.BlockSpec((1, 1, C, W), lambda b, d, s: (b, d, ch(s, d), 0)),
        out_shape=jax.ShapeDtypeStruct((B, 2, T, W), F32),
        scratch_shapes=[pltpu.VMEM((W // 2, W), F32)],
        compiler_params=_cparams(("parallel", "parallel", "arbitrary")),
        name="retention",
    )(cols, cos, sm, sp, decay)


def _dprep_kernel(c_ref, cos_ref, sm_ref, sp_ref, q_ref, k_ref, v_ref, *, half, scale):
    W = BRANCH_W
    c = c_ref[0]
    cos, sm, sp = cos_ref[...], sm_ref[...], sp_ref[...]
    q = (_rope(c[:, 0:W], cos, sm, sp, half) * scale).astype(BF16)
    kw = W // 2
    k = _rope(c[:, W:W + kw], cos[:, :kw], sm[:, :kw], sp[:, :kw], half).astype(BF16)
    v = c[:, W + kw:W + 2 * kw].astype(BF16)
    for kvh in range(2):
        for g in range(2):
            h = kvh * 2 + g
            q_ref[0, kvh, g] = q[:, h * HEAD_W:(h + 1) * HEAD_W]
        k_ref[0, kvh] = k[:, kvh * HEAD_W:(kvh + 1) * HEAD_W]
        v_ref[0, kvh] = v[:, kvh * HEAD_W:(kvh + 1) * HEAD_W]


def _win_prep(cols, tabs, tm):
    B, T, NCOL = cols.shape
    cos, sm, sp, half = tabs
    tab_spec = pl.BlockSpec((tm, BRANCH_W), lambda b, i: (i, 0))
    return pl.pallas_call(
        functools.partial(_dprep_kernel, half=half, scale=HEAD_W ** -0.5),
        grid=(B, T // tm),
        in_specs=[pl.BlockSpec((1, tm, NCOL), lambda b, i: (b, i, 0)), tab_spec, tab_spec, tab_spec],
        out_specs=[pl.BlockSpec((1, 2, 2, tm, HEAD_W), lambda b, i: (b, 0, 0, i, 0)),
                   pl.BlockSpec((1, 2, tm, HEAD_W), lambda b, i: (b, 0, i, 0)),
                   pl.BlockSpec((1, 2, tm, HEAD_W), lambda b, i: (b, 0, i, 0))],
        out_shape=[jax.ShapeDtypeStruct((B, 2, 2, T, HEAD_W), BF16),
                   jax.ShapeDtypeStruct((B, 2, T, HEAD_W), BF16),
                   jax.ShapeDtypeStruct((B, 2, T, HEAD_W), BF16)],
        compiler_params=_cparams(("parallel", "parallel")),
        name="win_prep",
    )(cols, cos, sm, sp)


def _winattn_kernel(sink_ref, q_ref, k_ref, v_ref, o_ref, *, tq, n_ctx, T):
    i = pl.program_id(2)
    band = tq + 2 * WINDOW
    q = q_ref[0, 0].reshape(2 * tq, HEAD_W)
    kx, vx = k_ref[0, 0, 0:n_ctx, :], v_ref[0, 0, 0:n_ctx, :]
    ws = pl.multiple_of(jnp.clip(i * tq - WINDOW, n_ctx, T - band), WINDOW)
    kb, vb = k_ref[0, 0, pl.ds(ws, band), :], v_ref[0, 0, pl.ds(ws, band), :]
    s_ctx = _dot_nt(q, kx)
    s_band = _dot_nt(q, kb)
    qpos = i * tq + lax.broadcasted_iota(jnp.int32, (tq, band), 0)
    kpos = ws + lax.broadcasted_iota(jnp.int32, (tq, band), 1)
    valid = jnp.logical_and(jnp.abs(qpos - kpos) <= WINDOW, qpos >= n_ctx)
    valid = jnp.concatenate([valid, valid], axis=0)
    s_band = jnp.where(valid, s_band, NEG_INF)
    sink = sink_ref[...]
    m = jnp.maximum(jnp.maximum(jnp.max(s_ctx, axis=1, keepdims=True),
                                jnp.max(s_band, axis=1, keepdims=True)), sink)
    p_ctx = jnp.exp(s_ctx - m)
    p_band = jnp.exp(s_band - m)
    den = (jnp.sum(p_ctx, axis=1, keepdims=True) + jnp.sum(p_band, axis=1, keepdims=True)
           + jnp.exp(sink - m))
    o = (jnp.dot(p_ctx.astype(BF16), vx, preferred_element_type=F32)
         + jnp.dot(p_band.astype(BF16), vb, preferred_element_type=F32)) / den
    o_ref[0, 0] = o.reshape(2, tq, HEAD_W)


def _window_attention(q, k, v, sink_col, l, tq, n_ctx):
    B, KVH, G, T, _ = q.shape
    return pl.pallas_call(
        functools.partial(_winattn_kernel, tq=tq, n_ctx=n_ctx, T=T),
        grid=(B, KVH, T // tq),
        in_specs=[pl.BlockSpec((None, None, G * tq, 1), lambda b, h, i: (l, h, 0, 0)),
                  pl.BlockSpec((1, 1, G, tq, HEAD_W), lambda b, h, i: (b, h, 0, i, 0)),
                  pl.BlockSpec((1, 1, T, HEAD_W), lambda b, h, i: (b, h, 0, 0)),
                  pl.BlockSpec((1, 1, T, HEAD_W), lambda b, h, i: (b, h, 0, 0))],
        out_specs=pl.BlockSpec((1, 1, G, tq, HEAD_W), lambda b, h, i: (b, h, 0, i, 0)),
        out_shape=jax.ShapeDtypeStruct((B, KVH, G, T, HEAD_W), F32),
        compiler_params=_cparams(("parallel", "parallel", "arbitrary")),
        name="window_attention",
    )(sink_col, q, k, v)


def _group_norm(x, ones_bd, eps, center):
    inv = 1.0 / HEAD_W
    if center:
        x = x - _dotf(x, ones_bd) * inv
    return x * lax.rsqrt(_dotf(x * x, ones_bd) * inv + eps)


def _merge_kernel(x_ref, gates_ref, ya_ref, yb_ref, aux_ref, oc_ref, gc_ref, yd_ref,
                  lnx_g_ref, lnx_b_ref, gn_ref, wb_ref, wo_ref, pn_ref, mod_ref, ones_ref, o_ref):
    W = BRANCH_W
    D = x_ref.shape[-1]
    ones_bd = ones_ref[...]
    yb = _group_norm(yb_ref[0, 0] + yb_ref[0, 1], ones_bd, RW_GN_EPS, True)
    yb = yb * lnx_g_ref[...] + lnx_b_ref[...]
    yb = (yb + aux_ref[0, :, 0:W]) * aux_ref[0, :, W:2 * W]
    yc = _group_norm(oc_ref[0, 0] + oc_ref[0, 1], ones_bd, 1e-5, True) * gn_ref[...]
    gc = gc_ref[0]
    yc = yc * (gc * _sigmoid(gc))

    def lift_heads(y_ref4, n):
        acc = None
        for h in range(N_HEADS):
            t = jnp.dot(y_ref4(h).astype(BF16), wb_ref[n, h * HEAD_W:(h + 1) * HEAD_W, :],
                        preferred_element_type=F32)
            acc = t if acc is None else acc + t
        return acc

    lifted = [lift_heads(lambda h: ya_ref[0, h], 0),
              jnp.dot(yb.astype(BF16), wb_ref[1], preferred_element_type=F32),
              jnp.dot(yc.astype(BF16), wb_ref[2], preferred_element_type=F32),
              lift_heads(lambda h: yd_ref[0, h // 2, h % 2], 3)]
    m = None
    for n in range(4):
        t = _sigmoid(gates_ref[0, :, n * D:(n + 1) * D]) * lifted[n]
        m = t if m is None else m + t
    out = jnp.dot(m.astype(BF16), wo_ref[...], preferred_element_type=F32)
    o_ref[0] = x_ref[0] + mod_ref[2:3, :] * (_rms(out, 1e-6) * pn_ref[...])


def _merge(xs, gates, ya, yb, aux, oc, cols_c, yd, prm, mods, l, tm, nct):
    B, T, D = xs.shape
    W = BRANCH_W
    vec = lambda n: pl.BlockSpec((None, 1, n), lambda b, i: (l, 0, 0))
    return pl.pallas_call(
        _merge_kernel,
        grid=(B, T // tm),
        in_specs=[pl.BlockSpec((1, tm, D), lambda b, i: (b, i, 0)),
                  pl.BlockSpec((1, tm, 4 * D), lambda b, i: (b, i, 0)),
                  pl.BlockSpec((1, N_HEADS, tm, HEAD_W), lambda b, i: (b, 0, i, 0)),
                  pl.BlockSpec((1, 2, tm, W), lambda b, i: (b, 0, i, 0)),
                  pl.BlockSpec((1, tm, 2 * W), lambda b, i: (b, i, 0)),
                  pl.BlockSpec((1, 2, tm, W), lambda b, i: (b, 0, i, 0)),
                  pl.BlockSpec((1, tm, W), lambda b, i: (b, i, 2)),
                  pl.BlockSpec((1, 2, 2, tm, HEAD_W), lambda b, i: (b, 0, 0, i, 0)),
                  vec(W), vec(W), vec(W),
                  _const_spec((None, 4, W, D), lambda b, i: (l, 0, 0, 0)),
                  _const_spec((None, D, D), lambda b, i: (l, 0, 0)),
                  vec(D),
                  pl.BlockSpec((None, None, None, 8, D),
                               lambda b, i: (l, b, jnp.minimum(i // nct, 1), 0, 0)),
                  _const_spec((W, W), lambda b, i: (0, 0))],
        out_specs=pl.BlockSpec((1, tm, D), lambda b, i: (b, i, 0)),
        out_shape=jax.ShapeDtypeStruct((B, T, D), F32),
        compiler_params=_cparams(("parallel", "parallel")),
        name="merge",
    )(xs, gates, ya, yb, aux, oc, cols_c, yd, prm["lnx_g"], prm["lnx_b"], prm["ret_gn"],
      prm["w_branch"], prm["w_out"], prm["norm_post_mix"], mods, prm["ones_bd"])


def _mlp_kernel(x_ref, g_ref, pn_ref, mod_ref, wu_ref, wd_ref, o_ref):
    x = x_ref[0]
    h = _rms(x, 1e-6) * g_ref[...]
    h = h * (1.0 + mod_ref[4:5, :]) + mod_ref[3:4, :]
    u = jnp.dot(h.astype(BF16), wu_ref[...], preferred_element_type=F32)
    u = jnp.square(jnp.maximum(u, 0.0))
    y = jnp.dot(u.astype(BF16), wd_ref[...], preferred_element_type=F32)
    o_ref[0] = x + mod_ref[5:6, :] * (_rms(y, 1e-6) * pn_ref[...])


def _mlp(xs, prm, mods, l, tm, nct):
    B, T, D = xs.shape
    F = prm["w_up"].shape[-1]
    vec = lambda n: pl.BlockSpec((None, 1, n), lambda b, i: (l, 0, 0))
    return pl.pallas_call(
        _mlp_kernel,
        grid=(B, T // tm),
        in_specs=[pl.BlockSpec((1, tm, D), lambda b, i: (b, i, 0)),
                  vec(D), vec(D),
                  pl.BlockSpec((None, None, None, 8, D),
                               lambda b, i: (l, b, jnp.minimum(i // nct, 1), 0, 0)),
                  _const_spec((None, D, F), lambda b, i: (l, 0, 0)),
                  _const_spec((None, F, D), lambda b, i: (l, 0, 0))],
        out_specs=pl.BlockSpec((1, tm, D), lambda b, i: (b, i, 0)),
        out_shape=jax.ShapeDtypeStruct((B, T, D), F32),
        compiler_params=_cparams(("parallel", "parallel")),
        name="mlp",
    )(xs, prm["norm_pre_mlp"], prm["norm_post_mlp"], mods, prm["w_up"], prm["w_down"])


def kernel(x, c, ctx, c_ctx, ada_w, ada_b, norm_pre_mix, norm_post_mix, norm_pre_mlp, norm_post_mlp,
           w_in, diff_lam_q, diff_lam_k, diff_subln, rwkv_mu, rwkv_w0, rwkv_w2, rwkv_a0, rwkv_a2,
           rwkv_g2, rwkv_kk, rwkv_ka, rwkv_rk, rwkv_lnx_g, rwkv_lnx_b, ret_decay, ret_gn, win_sink,
           w_branch, w_out, w_up, w_down):
    B, S, D = x.shape
    n_ctx = ctx.shape[1]
    T = n_ctx + S
    L = ada_w.shape[0]
    W = BRANCH_W
    TM = 256
    nct = n_ctx // TM
    assert n_ctx % TM == 0 and S % TM == 0 and S % GRID_W == 0 and B + 1 <= 8

    cc = jnp.zeros((8, D), F32).at[:B].set(c).at[B].set(c_ctx)
    mod = _ada_mod(cc, ada_w, ada_b).reshape(L, 8, 6, D)
    mod_ctx = jnp.broadcast_to(mod[:, B][:, None], (L, B, 6, D))
    mods = jnp.stack([mod_ctx, mod[:, :B]], axis=2)
    mods = jnp.pad(mods, ((0, 0), (0, 0), (0, 0), (0, 2), (0, 0)))

    lat = jnp.arange(S, dtype=jnp.int32)
    row = (lat // GRID_W).astype(F32)
    col = (lat % GRID_W).astype(F32)
    tabs_a = _rope_tables([row, col], W, HEAD_W // 2, n_ctx)
    tabs_d = _rope_tables([row, col], W, HEAD_W, n_ctx)
    tabs_c = _rope_tables([lat.astype(F32)], W // 2, HEAD_W // 2, n_ctx)

    sizes = (3 * W, rwkv_mu.shape[-1], 3 * W, 2 * W, 4 * D)
    offs = np.concatenate([[0], np.cumsum(sizes)])
    w_groups = [w_in[:, :, int(offs[n]):int(offs[n + 1])].astype(BF16) for n in range(5)]

    lw_w, la_w = rwkv_w2.shape[2], rwkv_a2.shape[2]
    zw = jnp.zeros((L, W, 2 * W), F32)
    w2cat = (zw.at[:, 0:lw_w, 0:W].set(rwkv_w2[:, 0])
             .at[:, lw_w:2 * lw_w, W:2 * W].set(rwkv_w2[:, 1]))
    a2cat = (zw.at[:, 2 * lw_w:2 * lw_w + la_w, 0:W].set(rwkv_a2[:, 0])
             .at[:, 2 * lw_w + la_w:2 * lw_w + 2 * la_w, W:2 * W].set(rwkv_a2[:, 1]))
    blk = np.arange(W) // HEAD_W
    ones_bd = jnp.asarray((blk[:, None] == blk[None, :]).astype(np.float32))
    v3 = lambda a: a.reshape(L, 1, -1)
    prm = dict(mu=v3(rwkv_mu), kk=v3(rwkv_kk), ka=v3(rwkv_ka), rk=v3(rwkv_rk), w0=rwkv_w0, a0=rwkv_a0,
               w2=w2cat, a2=a2cat, g2=rwkv_g2, ones_bd=ones_bd,
               lnx_g=v3(rwkv_lnx_g), lnx_b=v3(rwkv_lnx_b), ret_gn=v3(ret_gn),
               w_branch=w_branch.astype(BF16), w_out=w_out.astype(BF16),
               norm_post_mix=v3(norm_post_mix), norm_pre_mlp=v3(norm_pre_mlp),
               norm_post_mlp=v3(norm_post_mlp), w_up=w_up.astype(BF16), w_down=w_down.astype(BF16))
    subln = diff_subln.reshape(L, N_HEADS, 1, HEAD_W)
    decay = ret_decay.reshape(L, 2, 1, N_HEADS)
    TQ_D = 256
    sink_col = jnp.broadcast_to(win_sink.reshape(L, 2, 2, 1, 1), (L, 2, 2, TQ_D, 1)).reshape(L, 2, 2 * TQ_D, 1)
    gain_pre = v3(norm_pre_mix)

    xs = jnp.concatenate([ctx, x], axis=1)
    for l in range(L):
        cols_a, cols_b, cols_c, cols_d, gates = _input_proj(xs, gain_pre, mods, w_groups, l, TM, nct)
        qa, ka, va = _diff_prep(cols_a, tabs_a, TM)
        ya = _diff_attention(qa, ka, va, diff_lam_q, diff_lam_k, subln, l, 256, 256, n_ctx)
        rvk, dirs, aux = _rwkv_prep(cols_b, prm, l, TM, n_ctx)
        yb = _rwkv_scan(rvk, dirs, n_ctx)
        oc = _retention(cols_c, tabs_c, decay, l, n_ctx)
        qd, kd, vd = _win_prep(cols_d, tabs_d, TM)
        yd = _window_attention(qd, kd, vd, sink_col, l, TQ_D, n_ctx)
        xs = _merge(xs, gates, ya, yb, aux, oc, cols_c, yd, prm, mods, l, TM, nct)
        xs = _mlp(xs, prm, mods, l, TM, nct)
    return xs[:, n_ctx:]
```

```python
import functools
import math

import numpy as np
import jax
import jax.numpy as jnp
from jax import lax
from jax.experimental import pallas as pl
from jax.experimental.pallas import tpu as pltpu

F32 = jnp.float32
BF16 = jnp.bfloat16
HI = lax.Precision.HIGHEST

GRID_W = 64
ROPE_BASE = 10000.0
NEG_INF = -1e30
WINDOW = 128
N_HEADS = 4
HEAD_W = 64
BRANCH_W = N_HEADS * HEAD_W
RW_CHUNK = 64
RET_CHUNK = 128
RW_GN_EPS = 64e-5
VMEM_LIMIT = 56 * 1024 * 1024


def _cparams(sem, vmem=None):
    return pltpu.CompilerParams(dimension_semantics=sem, vmem_limit_bytes=vmem or VMEM_LIMIT)


def _dotf(a, b):
    return jnp.dot(a, b, precision=HI, preferred_element_type=F32)


def _dotb(a, b):
    return jnp.dot(a.astype(BF16), b.astype(BF16), preferred_element_type=F32)


def _dot_nt(a, b, precision=None):
    return lax.dot_general(a, b, (((1,), (1,)), ((), ())), precision=precision,
                           preferred_element_type=F32)


def _dot_tn(a, b, precision=None):
    return lax.dot_general(a, b, (((0,), (0,)), ((), ())), precision=precision,
                           preferred_element_type=F32)


def _sdot(a, b):
    return jnp.dot(a.astype(BF16), b.astype(BF16), preferred_element_type=F32)


def _sdot_nt(a, b):
    return _dot_nt(a.astype(BF16), b.astype(BF16))


def _sdot_tn(a, b):
    return _dot_tn(a.astype(BF16), b.astype(BF16))


def _sigmoid(x):
    return 1.0 / (1.0 + jnp.exp(-x))


def _softplus(x):
    return jnp.maximum(x, 0.0) + jnp.log1p(jnp.exp(-jnp.abs(x)))


def _rms(x, eps):
    return x * lax.rsqrt(jnp.mean(x * x, axis=-1, keepdims=True) + eps)


def _const_spec(shape, index):
    return pl.BlockSpec(shape, index, pipeline_mode=pl.Buffered(1))


def _ada_kernel(c_ref, w_ref, b_ref, o_ref):
    c = c_ref[...]
    o_ref[0] = _dotf(c * _sigmoid(c), w_ref[0]) + b_ref[0]


def _ada_mod(cc, ada_w, ada_b):
    L, D, N = ada_w.shape
    tn = N // 4
    return pl.pallas_call(
        _ada_kernel,
        grid=(L, N // tn),
        in_specs=[pl.BlockSpec((8, D), lambda l, n: (0, 0)),
                  pl.BlockSpec((1, D, tn), lambda l, n: (l, 0, n)),
                  pl.BlockSpec((1, 1, tn), lambda l, n: (l, 0, n))],
        out_specs=pl.BlockSpec((1, 8, tn), lambda l, n: (l, 0, n)),
        out_shape=jax.ShapeDtypeStruct((L, 8, N), F32),
        compiler_params=_cparams(("arbitrary", "arbitrary")),
        name="ada_mod",
    )(cc, ada_w, ada_b.reshape(L, 1, N))


def _win_kernel(x_ref, g_ref, mod_ref, *refs, n_out):
    w_refs, o_refs = refs[:n_out], refs[n_out:]
    h = _rms(x_ref[0], 1e-6) * g_ref[...]
    h = h * (1.0 + mod_ref[1:2, :]) + mod_ref[0:1, :]
    hb = h.astype(BF16)
    for w_ref, o_ref in zip(w_refs, o_refs):
        o_ref[0] = jnp.dot(hb, w_ref[...], preferred_element_type=F32).astype(o_ref.dtype)


def _input_proj(xs, gain, mods, weights, l, tm, nct):
    B, T, D = xs.shape
    n_out = len(weights)
    in_specs = [pl.BlockSpec((1, tm, D), lambda b, i: (b, i, 0)),
                pl.BlockSpec((None, 1, D), lambda b, i: (l, 0, 0)),
                pl.BlockSpec((None, None, None, 8, D),
                             lambda b, i: (l, b, jnp.minimum(i // nct, 1), 0, 0))]
    out_specs, out_shape = [], []
    for w in weights:
        n = w.shape[-1]
        in_specs.append(_const_spec((None, D, n), lambda b, i: (l, 0, 0)))
        out_specs.append(pl.BlockSpec((1, tm, n), lambda b, i: (b, i, 0)))
        out_shape.append(jax.ShapeDtypeStruct((B, T, n), F32))
    return pl.pallas_call(
        functools.partial(_win_kernel, n_out=n_out),
        grid=(B, T // tm),
        in_specs=in_specs, out_specs=out_specs, out_shape=out_shape,
        compiler_params=_cparams(("parallel", "parallel")),
        name="input_proj",
    )(xs, gain, mods, *weights)


def _rope_tables(pos_sets, lanes, group, n_ctx):
    n_sets = len(pos_sets)
    sub = group // n_sets
    half = sub // 2
    angs = []
    for pos in pos_sets:
        inv_freq = ROPE_BASE ** (-jnp.arange(half, dtype=F32) / half)
        ang = pos[:, None] * inv_freq[None, :]
        angs.append(jnp.concatenate([ang, ang], axis=-1))
    ang = jnp.concatenate(angs, axis=-1)
    ang = jnp.tile(ang, (1, lanes // group))
    ang = jnp.concatenate([jnp.zeros((n_ctx, lanes), F32), ang], axis=0)
    first = (np.arange(lanes) % sub) < half
    cos, sin = jnp.cos(ang), jnp.sin(ang)
    sin_minus = jnp.where(first[None, :], -sin, 0.0)
    sin_plus = jnp.where(first[None, :], 0.0, sin)
    return cos, sin_minus, sin_plus, half


def _rope(x, cos, sin_minus, sin_plus, half):
    n = x.shape[-1]
    return x * cos + pltpu.roll(x, n - half, 1) * sin_minus + pltpu.roll(x, half, 1) * sin_plus


def _aprep_kernel(c_ref, cos_ref, sm_ref, sp_ref, q_ref, k_ref, v_ref, *, half, scale):
    c = c_ref[0]
    cos, sm, sp = cos_ref[...], sm_ref[...], sp_ref[...]
    q = _rope(c[:, 0:BRANCH_W], cos, sm, sp, half) * scale
    k = _rope(c[:, BRANCH_W:2 * BRANCH_W], cos, sm, sp, half)
    v = c[:, 2 * BRANCH_W:3 * BRANCH_W]
    lane = lax.broadcasted_iota(jnp.int32, q.shape, 1)
    comp = (lane >> 5) & 1
    q0 = jnp.where(comp == 0, q, 0.0).astype(BF16)
    q1 = jnp.where(comp == 1, q, 0.0).astype(BF16)
    kb, vt = k.astype(BF16), v.T.astype(BF16)
    for h in range(N_HEADS):
        sl = slice(h * HEAD_W, (h + 1) * HEAD_W)
        q_ref[0, h, 0] = q0[:, sl]
        q_ref[0, h, 1] = q1[:, sl]
        k_ref[0, h] = kb[:, sl]
        v_ref[0, h] = vt[sl, :]


def _diff_prep(cols, tabs, tm):
    B, T, _ = cols.shape
    cos, sm, sp, half = tabs
    tab_spec = pl.BlockSpec((tm, BRANCH_W), lambda b, i: (i, 0))
    return pl.pallas_call(
        functools.partial(_aprep_kernel, half=half, scale=(HEAD_W // 2) ** -0.5),
        grid=(B, T // tm),
        in_specs=[pl.BlockSpec((1, tm, 3 * BRANCH_W), lambda b, i: (b, i, 0)),
                  tab_spec, tab_spec, tab_spec],
        out_specs=[pl.BlockSpec((1, N_HEADS, 2, tm, HEAD_W), lambda b, i: (b, 0, 0, i, 0)),
                   pl.BlockSpec((1, N_HEADS, tm, HEAD_W), lambda b, i: (b, 0, i, 0)),
                   pl.BlockSpec((1, N_HEADS, HEAD_W, tm), lambda b, i: (b, 0, 0, i))],
        out_shape=[jax.ShapeDtypeStruct((B, N_HEADS, 2, T, HEAD_W), BF16),
                   jax.ShapeDtypeStruct((B, N_HEADS, T, HEAD_W), BF16),
                   jax.ShapeDtypeStruct((B, N_HEADS, HEAD_W, T), BF16)],
        compiler_params=_cparams(("parallel", "parallel")),
        name="diff_prep",
    )(cols, cos, sm, sp)


def _flash_kernel(lq_ref, lk_ref, g_ref, q_ref, k_ref, vt_ref, o_ref, s_sc, acc_sc, *,
                  tq, tk, n_ctx, n_pairs, T, lam_init):
    qi = pl.program_id(2)
    q = q_ref[0, 0].reshape(2 * tq, HEAD_W)
    ones = jnp.ones((8, tk), BF16)

    def scores(slot, off):
        s_sc[slot] = _dot_nt(k_ref[0, 0, pl.ds(off, tk), :], q)

    def absorb(slot, off, m):
        s = s_sc[slot]
        m_new = jnp.maximum(m, jnp.max(s, axis=0, keepdims=True))
        p = jnp.exp(s - m_new).astype(BF16)
        vt = jnp.concatenate([vt_ref[0, 0, :, pl.ds(off, tk)], ones], axis=0)
        acc_sc[...] = acc_sc[...] * jnp.exp(m - m_new) + jnp.dot(vt, p, preferred_element_type=F32)
        return m_new

    acc_sc[...] = jnp.zeros(acc_sc.shape, F32)
    scores(0, 0)
    scores(1, n_ctx)
    m = absorb(0, 0, jnp.full((1, 2 * tq), NEG_INF, F32))

    def body(jj, m):
        off1 = pl.multiple_of(n_ctx + 2 * jj * tk, tk)
        off0 = pl.multiple_of(off1 + tk, tk)
        off2 = pl.multiple_of(jnp.minimum(off0 + tk, T - tk), tk)
        scores(0, off0)
        m = absorb(1, off1, m)
        scores(1, off2)
        return absorb(0, off0, m)

    lax.fori_loop(0, jnp.where(qi * tq < n_ctx, 0, n_pairs), body, m)
    acc = acc_sc[...]
    o = acc[0:HEAD_W] / acc[HEAD_W:HEAD_W + 1]
    e0 = jnp.exp(jnp.sum(lq_ref[0:1, :] * lk_ref[0:1, :], axis=1, keepdims=True))
    e1 = jnp.exp(jnp.sum(lq_ref[1:2, :] * lk_ref[1:2, :], axis=1, keepdims=True))
    lam = e0 - e1 + lam_init
    y = o[:, :tq] - lam * o[:, tq:]
    y = y * lax.rsqrt(jnp.mean(y * y, axis=0, keepdims=True) + 1e-5) * g_ref[...] * (1.0 - lam_init)
    o_ref[0, 0] = y.T


def _diff_attention(q, k, v, lam_q, lam_k, subln, l, tq, tk, n_ctx):
    B, H, _, T, _ = q.shape
    dqk = lam_q.shape[-1]
    lam_init = 0.8 - 0.6 * math.exp(-0.3 * l)
    assert tk == n_ctx and (T - n_ctx) % (2 * tk) == 0 and n_ctx % tq == 0
    kern = functools.partial(_flash_kernel, tq=tq, tk=tk, n_ctx=n_ctx,
                             n_pairs=(T - n_ctx) // (2 * tk), T=T, lam_init=lam_init)
    return pl.pallas_call(
        kern,
        grid=(B, H, T // tq),
        in_specs=[pl.BlockSpec((None, 2, dqk), lambda b, h, i: (l, 0, 0)),
                  pl.BlockSpec((None, 2, dqk), lambda b, h, i: (l, 0, 0)),
                  pl.BlockSpec((None, None, HEAD_W, 1), lambda b, h, i: (l, h, 0, 0)),
                  pl.BlockSpec((1, 1, 2, tq, HEAD_W), lambda b, h, i: (b, h, 0, i, 0)),
                  pl.BlockSpec((1, 1, T, HEAD_W), lambda b, h, i: (b, h, 0, 0)),
                  pl.BlockSpec((1, 1, HEAD_W, T), lambda b, h, i: (b, h, 0, 0))],
        out_specs=pl.BlockSpec((1, 1, tq, HEAD_W), lambda b, h, i: (b, h, i, 0)),
        out_shape=jax.ShapeDtypeStruct((B, H, T, HEAD_W), F32),
        scratch_shapes=[pltpu.VMEM((2, tk, 2 * tq), F32), pltpu.VMEM((HEAD_W + 8, 2 * tq), F32)],
        compiler_params=_cparams(("parallel", "parallel", "arbitrary")),
        name="diff_attention",
    )(lam_q, lam_k, subln, q, k, v)


def _rwprep_kernel(c_ref, p_ref, n_ref, mu_ref, kk_ref, ka_ref, rk_ref, w0_ref, a0_ref,
                   w2_ref, a2_ref, g2_ref, ones_ref, rvk_ref, dir_ref, aux_ref, *, tm, n_ctx, T):
    i = pl.program_id(1)
    c = c_ref[0]
    t0 = i * tm
    seg_start = jnp.logical_or(t0 == 0, t0 == n_ctx)
    seg_end = jnp.logical_or(t0 + tm == n_ctx, t0 + tm == T)
    pv = jnp.where(seg_start, 0.0, p_ref[0][7:8, :])
    nx = jnp.where(seg_end, 0.0, n_ref[0][0:1, :])
    row = lax.broadcasted_iota(jnp.int32, c.shape, 0)
    prev = jnp.where(row == 0, pv, pltpu.roll(c, 1, 0))
    nxt = jnp.where(row == tm - 1, nx, pltpu.roll(c, tm - 1, 0))
    xs = c + (0.5 * (prev + nxt) - c) * mu_ref[...]
    W = BRANCH_W
    r, k, v = xs[:, 0:W], xs[:, W:2 * W], xs[:, 2 * W:3 * W]
    lo, gl = xs[:, 3 * W:4 * W], xs[:, 4 * W:]
    ones_bd = ones_ref[...]
    kk = k * kk_ref[...]
    kk = kk * lax.rsqrt(jnp.maximum(_dotf(kk * kk, ones_bd), 1e-12))
    wpre = _dotf(jnp.tanh(lo), w2_ref[...])
    apre = _dotf(lo, a2_ref[...])
    gate = _dotf(_sigmoid(gl), g2_ref[...])
    rb = jnp.zeros_like(r)
    for d in range(2):
        z = w0_ref[d:d + 1, :] + wpre[:, d * W:(d + 1) * W]
        w = -_softplus(-z) - 0.5
        a = _sigmoid(a0_ref[d:d + 1, :] + apre[:, d * W:(d + 1) * W])
        kd = k * (1.0 + (a - 1.0) * ka_ref[...])
        dir_ref[0, d, :, 0:W] = -jnp.exp(w)
        dir_ref[0, d, :, W:2 * W] = kd
        dir_ref[0, d, :, 2 * W:3 * W] = kk * a
        rb = rb + r * kd * rk_ref[...]
    rvk_ref[0, :, 0:W] = r
    rvk_ref[0, :, W:2 * W] = v
    rvk_ref[0, :, 2 * W:3 * W] = kk
    aux_ref[0, :, 0:W] = _dotf(rb, ones_bd) * v
    aux_ref[0, :, W:2 * W] = gate


def _rwkv_prep(cols, prm, l, tm, n_ctx):
    B, T, NB = cols.shape
    W = BRANCH_W
    nblk8 = T // 8
    r8 = tm // 8
    vec = lambda n: pl.BlockSpec((None, 1, n), lambda b, i: (l, 0, 0))
    pair = lambda n: pl.BlockSpec((None, 2, n), lambda b, i: (l, 0, 0))
    mat = lambda m, n: _const_spec((None, m, n), lambda b, i: (l, 0, 0))
    return pl.pallas_call(
        functools.partial(_rwprep_kernel, tm=tm, n_ctx=n_ctx, T=T),
        grid=(B, T // tm),
        in_specs=[pl.BlockSpec((1, tm, NB), lambda b, i: (b, i, 0)),
                  pl.BlockSpec((1, 8, NB), lambda b, i: (b, jnp.maximum(i * r8 - 1, 0), 0)),
                  pl.BlockSpec((1, 8, NB), lambda b, i: (b, jnp.minimum((i + 1) * r8, nblk8 - 1), 0)),
                  vec(NB), vec(W), vec(W), vec(W), pair(W), pair(W),
                  mat(W, 2 * W), mat(W, 2 * W), mat(NB - 4 * W, W),
                  _const_spec((W, W), lambda b, i: (0, 0))],
        out_specs=[pl.BlockSpec((1, tm, 3 * W), lambda b, i: (b, i, 0)),
                   pl.BlockSpec((1, 2, tm, 3 * W), lambda b, i: (b, 0, i, 0)),
                   pl.BlockSpec((1, tm, 2 * W), lambda b, i: (b, i, 0))],
        out_shape=[jax.ShapeDtypeStruct((B, T, 3 * W), F32),
                   jax.ShapeDtypeStruct((B, 2, T, 3 * W), F32),
                   jax.ShapeDtypeStruct((B, T, 2 * W), F32)],
        compiler_params=_cparams(("parallel", "parallel")),
        name="rwkv_prep",
    )(cols, cols, cols, prm["mu"], prm["kk"], prm["ka"], prm["rk"], prm["w0"], prm["a0"],
      prm["w2"], prm["a2"], prm["g2"], prm["ones_bd"])


def _rw_chunk(r, v, kk, lw, kd, b, s0, sgn):
    C, W = RW_CHUNK, BRANCH_W
    ti = lax.broadcasted_iota(jnp.int32, (C, C), 0)
    tj = lax.broadcasted_iota(jnp.int32, (C, C), 1)
    tri = jnp.where((tj - ti) * sgn <= 0, 1.0, 0.0)
    log_g = _dotf(tri, lw)
    log_gc = jnp.sum(lw, axis=0, keepdims=True)
    g_inv = jnp.exp(-log_g)
    kk_t = kk * jnp.exp(log_g - lw)
    r_t = r * jnp.exp(log_g)
    b_t, kd_t = b * g_inv, kd * g_inv
    g_tail = jnp.exp(log_gc - log_g)
    b_h, kd_h = b * g_tail, kd * g_tail

    lane = lax.broadcasted_iota(jnp.int32, (1, W), 1)
    head_of_lane = lane >> 6
    stack = lambda x: jnp.concatenate(
        [jnp.where(head_of_lane == h, x, 0.0) for h in range(N_HEADS)], axis=0)
    tile = lambda x: jnp.concatenate([x] * N_HEADS, axis=0)

    ri = lax.broadcasted_iota(jnp.int32, (W, W), 0)
    ci = lax.broadcasted_iota(jnp.int32, (W, W), 1)
    same_head = (ri >> 6) == (ci >> 6)
    dt = ((ci & (C - 1)) - (ri & (C - 1))) * sgn
    strict = jnp.logical_and(same_head, dt < 0)
    incl = jnp.logical_and(same_head, dt <= 0)

    left = jnp.concatenate([stack(kk_t), stack(r_t)], axis=0)
    right = jnp.concatenate([stack(b_t), stack(kd_t)], axis=0)
    g = _sdot_nt(left, right)
    m_sl = jnp.where(strict, g[:W, :W], 0.0)
    n_sl = jnp.where(strict, g[:W, W:], 0.0)
    p_l = jnp.where(incl, g[W:, :W], 0.0)
    q_l = jnp.where(incl, g[W:, W:], 0.0)

    g2 = _sdot(jnp.concatenate([kk_t, r_t], axis=0), s0)
    vs = tile(v)
    rhs = -(tile(g2[:C]) + _sdot(n_sl, vs))

    eye = jnp.where(ri == ci, 1.0, 0.0)
    d0 = jnp.where((ri >> 4) == (ci >> 4), m_sl, 0.0)
    a2 = _sdot(d0, d0)
    a4 = _sdot(a2, a2)
    a8 = _sdot(a4, a4)
    t = eye - d0
    t = t + _sdot(t, a2)
    t = t + _sdot(t, a4)
    t = t + _sdot(t, a8)
    c1 = jnp.where(jnp.logical_and((ri >> 5) == (ci >> 5), (ri >> 4) != (ci >> 4)), m_sl, 0.0)
    t = t - _sdot(t, _sdot(c1, t))
    c2 = jnp.where((ri >> 5) != (ci >> 5), m_sl, 0.0)
    t = t - _sdot(t, _sdot(c2, t))

    sa = _sdot(t, rhs)
    sav = jnp.concatenate([sa, vs], axis=0)
    ys = tile(g2[C:]) + _sdot(jnp.concatenate([p_l, q_l], axis=1), sav)
    y = jnp.zeros((C, W), F32)
    for h in range(N_HEADS):
        y = y + jnp.where(head_of_lane == h, ys[h * C:(h + 1) * C], 0.0)

    ds = _sdot_tn(jnp.concatenate([stack(b_h), stack(kd_h)], axis=0), sav)
    gc_col = jnp.sum(jnp.where(ri == ci, jnp.exp(log_gc), 0.0), axis=1, keepdims=True)
    return y, jnp.where(same_head, s0 * gc_col + ds, 0.0)


def _rwscan_kernel(rvf_ref, rvb_ref, df_ref, db_ref, yf_ref, yb_ref, s_ref):
    W = BRANCH_W

    @pl.when(pl.program_id(0) == 0)
    def _():
        s_ref[...] = jnp.zeros(s_ref.shape, F32)

    for bi in range(rvf_ref.shape[0]):
        for d, (rv, dr, yo) in enumerate(((rvf_ref, df_ref, yf_ref), (rvb_ref, db_ref, yb_ref))):
            y, s_new = _rw_chunk(rv[bi, :, 0:W], rv[bi, :, W:2 * W], rv[bi, :, 2 * W:3 * W],
                                 dr[bi, 0, :, 0:W], dr[bi, 0, :, W:2 * W], dr[bi, 0, :, 2 * W:3 * W],
                                 s_ref[2 * bi + d], 1 - 2 * d)
            yo[bi] = y
            s_ref[2 * bi + d] = s_new


def _chunk_order(s, d, n_ctx_chunks, n_chunks):
    bwd = jnp.where(s < n_ctx_chunks, n_ctx_chunks - 1 - s, n_chunks - 1 - (s - n_ctx_chunks))
    return jnp.where(d == 0, s, bwd)


def _rwkv_scan(rvk, dirs, n_ctx):
    B, T, _ = rvk.shape
    C, W = RW_CHUNK, BRANCH_W
    nc, ncx = T // C, n_ctx // C
    ch = lambda s, d: _chunk_order(s, d, ncx, nc)
    rv_spec = lambda d: pl.BlockSpec((B, C, 3 * W), lambda s: (0, ch(s, d), 0))
    dir_spec = lambda d: pl.BlockSpec((B, 1, C, 3 * W), lambda s: (0, d, ch(s, d), 0))
    y_spec = lambda d: pl.BlockSpec((B, C, W), lambda s: (0, ch(s, d), 0))
    return pl.pallas_call(
        _rwscan_kernel,
        grid=(nc,),
        in_specs=[rv_spec(0), rv_spec(1), dir_spec(0), dir_spec(1)],
        out_specs=[y_spec(0), y_spec(1)],
        out_shape=[jax.ShapeDtypeStruct((B, T, W), F32)] * 2,
        scratch_shapes=[pltpu.VMEM((2 * B, W, W), F32)],
        compiler_params=_cparams(("arbitrary",)),
        name="rwkv_scan",
    )(rvk, rvk, dirs, dirs)


def _ret_kernel(c_ref, cos_ref, sm_ref, sp_ref, dec_ref, o_ref, s_ref, *, half, scale):
    C, W = RET_CHUNK, BRANCH_W
    QW = W // 2
    d = pl.program_id(1)
    s = pl.program_id(2)

    @pl.when(s == 0)
    def _():
        s_ref[...] = jnp.zeros(s_ref.shape, F32)

    c = c_ref[0]
    cos, sm, sp = cos_ref[...], sm_ref[...], sp_ref[...]
    q = _rope(c[:, 0:QW], cos, sm, sp, half)
    k = _rope(c[:, QW:2 * QW], cos, sm, sp, half) * scale
    v = c[:, 2 * QW:2 * QW + W]
    log_g = -_softplus(-dec_ref[...])
    lane_q = lax.broadcasted_iota(jnp.int32, (1, QW), 1) >> 5
    lane_v = lax.broadcasted_iota(jnp.int32, (1, W), 1) >> 6
    lg_q = jnp.zeros((1, QW), F32)
    lg_v = jnp.zeros((1, W), F32)
    for h in range(N_HEADS):
        lg_q = jnp.where(lane_q == h, log_g[:, h:h + 1], lg_q)
        lg_v = jnp.where(lane_v == h, log_g[:, h:h + 1], lg_v)
    fwd = d == 0
    idx = lax.broadcasted_iota(jnp.int32, (C, 1), 0).astype(F32)
    q_dec = jnp.exp(jnp.where(fwd, idx + 1.0, C - idx) * lg_q)
    k_dec = jnp.exp(jnp.where(fwd, C - 1.0 - idx, idx) * lg_q)
    ti = lax.broadcasted_iota(jnp.int32, (C, C), 0)
    tj = lax.broadcasted_iota(jnp.int32, (C, C), 1)
    rel = ((ti - tj) * (1 - 2 * d)).astype(F32)

    s0 = s_ref[...]
    o = _dotb(q * q_dec, s0)
    kb = k.astype(BF16)
    vb = v.astype(BF16)
    for h in range(N_HEADS):
        dm = jnp.where(rel >= 0, jnp.exp(jnp.maximum(rel, 0.0) * log_g[:, h:h + 1]), 0.0)
        qh = jnp.where(lane_q == h, q, 0.0).astype(BF16)
        sc = _dot_nt(qh, kb) * dm
        o = o + jnp.where(lane_v == h, jnp.dot(sc.astype(BF16), vb, preferred_element_type=F32), 0.0)
    o_ref[0, 0] = o

    ri = lax.broadcasted_iota(jnp.int32, (QW, W), 0) >> 5
    ci = lax.broadcasted_iota(jnp.int32, (QW, W), 1) >> 6
    upd = _dot_tn((k * k_dec).astype(BF16), vb)
    s_ref[...] = jnp.where(ri == ci, s0 * jnp.exp(C * lg_v) + upd, 0.0)


def _retention(cols, tabs, decay, l, n_ctx):
    B, T, NCOL = cols.shape
    C, W = RET_CHUNK, BRANCH_W
    cos, sm, sp, half = tabs
    nc, ncx = T // C, n_ctx // C
    ch = lambda s, d: _chunk_order(s, d, ncx, nc)
    tab_spec = pl.BlockSpec((C, W // 2), lambda b, d, s: (ch(s, d), 0))
    return pl.pallas_call(
        functools.partial(_ret_kernel, half=half, scale=(HEAD_W // 2) ** -0.5),
        grid=(B, 2, nc),
        in_specs=[pl.BlockSpec((1, C, NCOL), lambda b, d, s: (b, ch(s, d), 0)),
                  tab_spec, tab_spec, tab_spec,
                  pl.BlockSpec((None, None, 1, N_HEADS), lambda b, d, s: (l, d, 0, 0))],
        out_specs=pl.BlockSpec((1, 1, C, W), lambda b, d, s: (b, d, ch(s, d), 0)),
        out_shape=jax.ShapeDtypeStruct((B, 2, T, W), F32),
        scratch_shapes=[pltpu.VMEM((W // 2, W), F32)],
        compiler_params=_cparams(("parallel", "parallel", "arbitrary")),
        name="retention",
    )(cols, cos, sm, sp, decay)


def _dprep_kernel(c_ref, cos_ref, sm_ref, sp_ref, q_ref, k_ref, v_ref, *, half, scale):
    W = BRANCH_W
    c = c_ref[0]
    cos, sm, sp = cos_ref[...], sm_ref[...], sp_ref[...]
    q = (_rope(c[:, 0:W], cos, sm, sp, half) * scale).astype(BF16)
    kw = W // 2
    k = _rope(c[:, W:W + kw], cos[:, :kw], sm[:, :kw], sp[:, :kw], half).astype(BF16)
    v = c[:, W + kw:W + 2 * kw].astype(BF16)
    for kvh in range(2):
        for g in range(2):
            h = kvh * 2 + g
            q_ref[0, kvh, g] = q[:, h * HEAD_W:(h + 1) * HEAD_W]
        k_ref[0, kvh] = k[:, kvh * HEAD_W:(kvh + 1) * HEAD_W]
        v_ref[0, kvh] = v[:, kvh * HEAD_W:(kvh + 1) * HEAD_W]


def _win_prep(cols, tabs, tm):
    B, T, NCOL = cols.shape
    cos, sm, sp, half = tabs
    tab_spec = pl.BlockSpec((tm, BRANCH_W), lambda b, i: (i, 0))
    return pl.pallas_call(
        functools.partial(_dprep_kernel, half=half, scale=HEAD_W ** -0.5),
        grid=(B, T // tm),
        in_specs=[pl.BlockSpec((1, tm, NCOL), lambda b, i: (b, i, 0)), tab_spec, tab_spec, tab_spec],
        out_specs=[pl.BlockSpec((1, 2, 2, tm, HEAD_W), lambda b, i: (b, 0, 0, i, 0)),
                   pl.BlockSpec((1, 2, tm, HEAD_W), lambda b, i: (b, 0, i, 0)),
                   pl.BlockSpec((1, 2, tm, HEAD_W), lambda b, i: (b, 0, i, 0))],
        out_shape=[jax.ShapeDtypeStruct((B, 2, 2, T, HEAD_W), BF16),
                   jax.ShapeDtypeStruct((B, 2, T, HEAD_W), BF16),
                   jax.ShapeDtypeStruct((B, 2, T, HEAD_W), BF16)],
        compiler_params=_cparams(("parallel", "parallel")),
        name="win_prep",
    )(cols, cos, sm, sp)


def _winattn_kernel(sink_ref, q_ref, k_ref, v_ref, o_ref, *, tq, n_ctx, T):
    i = pl.program_id(2)
    band = tq + 2 * WINDOW
    q = q_ref[0, 0].reshape(2 * tq, HEAD_W)
    kx, vx = k_ref[0, 0, 0:n_ctx, :], v_ref[0, 0, 0:n_ctx, :]
    ws = pl.multiple_of(jnp.clip(i * tq - WINDOW, n_ctx, T - band), WINDOW)
    kb, vb = k_ref[0, 0, pl.ds(ws, band), :], v_ref[0, 0, pl.ds(ws, band), :]
    s_ctx = _dot_nt(q, kx)
    s_band = _dot_nt(q, kb)
    qpos = i * tq + lax.broadcasted_iota(jnp.int32, (tq, band), 0)
    kpos = ws + lax.broadcasted_iota(jnp.int32, (tq, band), 1)
    valid = jnp.logical_and(jnp.abs(qpos - kpos) <= WINDOW, qpos >= n_ctx)
    valid = jnp.concatenate([valid, valid], axis=0)
    s_band = jnp.where(valid, s_band, NEG_INF)
    sink = sink_ref[...]
    m = jnp.maximum(jnp.maximum(jnp.max(s_ctx, axis=1, keepdims=True),
                                jnp.max(s_band, axis=1, keepdims=True)), sink)
    p_ctx = jnp.exp(s_ctx - m)
    p_band = jnp.exp(s_band - m)
    den = (jnp.sum(p_ctx, axis=1, keepdims=True) + jnp.sum(p_band, axis=1, keepdims=True)
           + jnp.exp(sink - m))
    o = (jnp.dot(p_ctx.astype(BF16), vx, preferred_element_type=F32)
         + jnp.dot(p_band.astype(BF16), vb, preferred_element_type=F32)) / den
    o_ref[0, 0] = o.reshape(2, tq, HEAD_W)


def _window_attention(q, k, v, sink_col, l, tq, n_ctx):
    B, KVH, G, T, _ = q.shape
    return pl.pallas_call(
        functools.partial(_winattn_kernel, tq=tq, n_ctx=n_ctx, T=T),
        grid=(B, KVH, T // tq),
        in_specs=[pl.BlockSpec((None, None, G * tq, 1), lambda b, h, i: (l, h, 0, 0)),
                  pl.BlockSpec((1, 1, G, tq, HEAD_W), lambda b, h, i: (b, h, 0, i, 0)),
                  pl.BlockSpec((1, 1, T, HEAD_W), lambda b, h, i: (b, h, 0, 0)),
                  pl.BlockSpec((1, 1, T, HEAD_W), lambda b, h, i: (b, h, 0, 0))],
        out_specs=pl.BlockSpec((1, 1, G, tq, HEAD_W), lambda b, h, i: (b, h, 0, i, 0)),
        out_shape=jax.ShapeDtypeStruct((B, KVH, G, T, HEAD_W), F32),
        compiler_params=_cparams(("parallel", "parallel", "arbitrary")),
        name="window_attention",
    )(sink_col, q, k, v)


def _group_norm(x, ones_bd, eps, center):
    inv = 1.0 / HEAD_W
    if center:
        x = x - _dotf(x, ones_bd) * inv
    return x * lax.rsqrt(_dotf(x * x, ones_bd) * inv + eps)


def _merge_kernel(x_ref, gates_ref, ya_ref, ybf_ref, ybb_ref, aux_ref, oc_ref, gc_ref, yd_ref,
                  lnx_g_ref, lnx_b_ref, gn_ref, wb_ref, wo_ref, pn_ref, mod_ref, ones_ref, o_ref):
    W = BRANCH_W
    D = x_ref.shape[-1]
    ones_bd = ones_ref[...]
    yb = _group_norm(ybf_ref[0] + ybb_ref[0], ones_bd, RW_GN_EPS, True)
    yb = yb * lnx_g_ref[...] + lnx_b_ref[...]
    yb = (yb + aux_ref[0, :, 0:W]) * aux_ref[0, :, W:2 * W]
    yc = _group_norm(oc_ref[0, 0] + oc_ref[0, 1], ones_bd, 1e-5, True) * gn_ref[...]
    gc = gc_ref[0]
    yc = yc * (gc * _sigmoid(gc))

    def lift_heads(y_ref4, n):
        acc = None
        for h in range(N_HEADS):
            t = jnp.dot(y_ref4(h).astype(BF16), wb_ref[n, h * HEAD_W:(h + 1) * HEAD_W, :],
                        preferred_element_type=F32)
            acc = t if acc is None else acc + t
        return acc

    lifted = [lift_heads(lambda h: ya_ref[0, h], 0),
              jnp.dot(yb.astype(BF16), wb_ref[1], preferred_element_type=F32),
              jnp.dot(yc.astype(BF16), wb_ref[2], preferred_element_type=F32),
              lift_heads(lambda h: yd_ref[0, h // 2, h % 2], 3)]
    m = None
    for n in range(4):
        t = _sigmoid(gates_ref[0, :, n * D:(n + 1) * D]) * lifted[n]
        m = t if m is None else m + t
    out = jnp.dot(m.astype(BF16), wo_ref[...], preferred_element_type=F32)
    o_ref[0] = x_ref[0] + mod_ref[2:3, :] * (_rms(out, 1e-6) * pn_ref[...])


def _merge(xs, gates, ya, yb, aux, oc, cols_c, yd, prm, mods, l, tm, nct):
    B, T, D = xs.shape
    W = BRANCH_W
    vec = lambda n: pl.BlockSpec((None, 1, n), lambda b, i: (l, 0, 0))
    return pl.pallas_call(
        _merge_kernel,
        grid=(B, T // tm),
        in_specs=[pl.BlockSpec((1, tm, D), lambda b, i: (b, i, 0)),
                  pl.BlockSpec((1, tm, 4 * D), lambda b, i: (b, i, 0)),
                  pl.BlockSpec((1, N_HEADS, tm, HEAD_W), lambda b, i: (b, 0, i, 0)),
                  pl.BlockSpec((1, tm, W), lambda b, i: (b, i, 0)),
                  pl.BlockSpec((1, tm, W), lambda b, i: (b, i, 0)),
                  pl.BlockSpec((1, tm, 2 * W), lambda b, i: (b, i, 0)),
                  pl.BlockSpec((1, 2, tm, W), lambda b, i: (b, 0, i, 0)),
                  pl.BlockSpec((1, tm, W), lambda b, i: (b, i, 2)),
                  pl.BlockSpec((1, 2, 2, tm, HEAD_W), lambda b, i: (b, 0, 0, i, 0)),
                  vec(W), vec(W), vec(W),
                  _const_spec((None, 4, W, D), lambda b, i: (l, 0, 0, 0)),
                  _const_spec((None, D, D), lambda b, i: (l, 0, 0)),
                  vec(D),
                  pl.BlockSpec((None, None, None, 8, D),
                               lambda b, i: (l, b, jnp.minimum(i // nct, 1), 0, 0)),
                  _const_spec((W, W), lambda b, i: (0, 0))],
        out_specs=pl.BlockSpec((1, tm, D), lambda b, i: (b, i, 0)),
        out_shape=jax.ShapeDtypeStruct((B, T, D), F32),
        compiler_params=_cparams(("parallel", "parallel")),
        name="merge",
    )(xs, gates, ya, yb[0], yb[1], aux, oc, cols_c, yd, prm["lnx_g"], prm["lnx_b"], prm["ret_gn"],
      prm["w_branch"], prm["w_out"], prm["norm_post_mix"], mods, prm["ones_bd"])


def _mlp_kernel(x_ref, g_ref, pn_ref, mod_ref, wu_ref, wd_ref, o_ref):
    x = x_ref[0]
    h = _rms(x, 1e-6) * g_ref[...]
    h = h * (1.0 + mod_ref[4:5, :]) + mod_ref[3:4, :]
    u = jnp.dot(h.astype(BF16), wu_ref[...], preferred_element_type=F32)
    u = jnp.square(jnp.maximum(u, 0.0))
    y = jnp.dot(u.astype(BF16), wd_ref[...], preferred_element_type=F32)
    o_ref[0] = x + mod_ref[5:6, :] * (_rms(y, 1e-6) * pn_ref[...])


def _mlp(xs, prm, mods, l, tm, nct):
    B, T, D = xs.shape
    F = prm["w_up"].shape[-1]
    vec = lambda n: pl.BlockSpec((None, 1, n), lambda b, i: (l, 0, 0))
    return pl.pallas_call(
        _mlp_kernel,
        grid=(B, T // tm),
        in_specs=[pl.BlockSpec((1, tm, D), lambda b, i: (b, i, 0)),
                  vec(D), vec(D),
                  pl.BlockSpec((None, None, None, 8, D),
                               lambda b, i: (l, b, jnp.minimum(i // nct, 1), 0, 0)),
                  _const_spec((None, D, F), lambda b, i: (l, 0, 0)),
                  _const_spec((None, F, D), lambda b, i: (l, 0, 0))],
        out_specs=pl.BlockSpec((1, tm, D), lambda b, i: (b, i, 0)),
        out_shape=jax.ShapeDtypeStruct((B, T, D), F32),
        compiler_params=_cparams(("parallel", "parallel")),
        name="mlp",
    )(xs, prm["norm_pre_mlp"], prm["norm_post_mlp"], mods, prm["w_up"], prm["w_down"])


def kernel(x, c, ctx, c_ctx, ada_w, ada_b, norm_pre_mix, norm_post_mix, norm_pre_mlp, norm_post_mlp,
           w_in, diff_lam_q, diff_lam_k, diff_subln, rwkv_mu, rwkv_w0, rwkv_w2, rwkv_a0, rwkv_a2,
           rwkv_g2, rwkv_kk, rwkv_ka, rwkv_rk, rwkv_lnx_g, rwkv_lnx_b, ret_decay, ret_gn, win_sink,
           w_branch, w_out, w_up, w_down):
    B, S, D = x.shape
    n_ctx = ctx.shape[1]
    T = n_ctx + S
    L = ada_w.shape[0]
    W = BRANCH_W
    TM = 256
    nct = n_ctx // TM
    assert n_ctx % TM == 0 and S % TM == 0 and S % GRID_W == 0 and B + 1 <= 8

    cc = jnp.zeros((8, D), F32).at[:B].set(c).at[B].set(c_ctx)
    mod = _ada_mod(cc, ada_w, ada_b).reshape(L, 8, 6, D)
    mod_ctx = jnp.broadcast_to(mod[:, B][:, None], (L, B, 6, D))
    mods = jnp.stack([mod_ctx, mod[:, :B]], axis=2)
    mods = jnp.pad(mods, ((0, 0), (0, 0), (0, 0), (0, 2), (0, 0)))

    lat = jnp.arange(S, dtype=jnp.int32)
    row = (lat // GRID_W).astype(F32)
    col = (lat % GRID_W).astype(F32)
    tabs_a = _rope_tables([row, col], W, HEAD_W // 2, n_ctx)
    tabs_d = _rope_tables([row, col], W, HEAD_W, n_ctx)
    tabs_c = _rope_tables([lat.astype(F32)], W // 2, HEAD_W // 2, n_ctx)

    sizes = (3 * W, rwkv_mu.shape[-1], 3 * W, 2 * W, 4 * D)
    offs = np.concatenate([[0], np.cumsum(sizes)])
    w_groups = [w_in[:, :, int(offs[n]):int(offs[n + 1])].astype(BF16) for n in range(5)]

    lw_w, la_w = rwkv_w2.shape[2], rwkv_a2.shape[2]
    zw = jnp.zeros((L, W, 2 * W), F32)
    w2cat = (zw.at[:, 0:lw_w, 0:W].set(rwkv_w2[:, 0])
             .at[:, lw_w:2 * lw_w, W:2 * W].set(rwkv_w2[:, 1]))
    a2cat = (zw.at[:, 2 * lw_w:2 * lw_w + la_w, 0:W].set(rwkv_a2[:, 0])
             .at[:, 2 * lw_w + la_w:2 * lw_w + 2 * la_w, W:2 * W].set(rwkv_a2[:, 1]))
    blk = np.arange(W) // HEAD_W
    ones_bd = jnp.asarray((blk[:, None] == blk[None, :]).astype(np.float32))
    v3 = lambda a: a.reshape(L, 1, -1)
    prm = dict(mu=v3(rwkv_mu), kk=v3(rwkv_kk), ka=v3(rwkv_ka), rk=v3(rwkv_rk), w0=rwkv_w0, a0=rwkv_a0,
               w2=w2cat, a2=a2cat, g2=rwkv_g2, ones_bd=ones_bd,
               lnx_g=v3(rwkv_lnx_g), lnx_b=v3(rwkv_lnx_b), ret_gn=v3(ret_gn),
               w_branch=w_branch.astype(BF16), w_out=w_out.astype(BF16),
               norm_post_mix=v3(norm_post_mix), norm_pre_mlp=v3(norm_pre_mlp),
               norm_post_mlp=v3(norm_post_mlp), w_up=w_up.astype(BF16), w_down=w_down.astype(BF16))
    subln = diff_subln.reshape(L, N_HEADS, HEAD_W, 1)
    decay = ret_decay.reshape(L, 2, 1, N_HEADS)
    TQ_D = 256
    sink_col = jnp.broadcast_to(win_sink.reshape(L, 2, 2, 1, 1), (L, 2, 2, TQ_D, 1)).reshape(L, 2, 2 * TQ_D, 1)
    gain_pre = v3(norm_pre_mix)

    xs = jnp.concatenate([ctx, x], axis=1)
    for l in range(L):
        cols_a, cols_b, cols_c, cols_d, gates = _input_proj(xs, gain_pre, mods, w_groups, l, TM, nct)
        qa, ka, va = _diff_prep(cols_a, tabs_a, TM)
        ya = _diff_attention(qa, ka, va, diff_lam_q, diff_lam_k, subln, l, 256, 256, n_ctx)
        rvk, dirs, aux = _rwkv_prep(cols_b, prm, l, TM, n_ctx)
        yb = _rwkv_scan(rvk, dirs, n_ctx)
        oc = _retention(cols_c, tabs_c, decay, l, n_ctx)
        qd, kd, vd = _win_prep(cols_d, tabs_d, TM)
        yd = _window_attention(qd, kd, vd, sink_col, l, TQ_D, n_ctx)
        xs = _merge(xs, gates, ya, yb, aux, oc, cols_c, yd, prm, mods, l, TM, nct)
        xs = _mlp(xs, prm, mods, l, TM, nct)
    return xs[:, n_ctx:]
```

```python
import functools
import math

import numpy as np
import jax
import jax.numpy as jnp
from jax import lax
from jax.experimental import pallas as pl
from jax.experimental.pallas import tpu as pltpu

F32 = jnp.float32
BF16 = jnp.bfloat16
HI = lax.Precision.HIGHEST

GRID_W = 64
ROPE_BASE = 10000.0
NEG_INF = -1e30
WINDOW = 128
N_HEADS = 4
HEAD_W = 64
BRANCH_W = N_HEADS * HEAD_W
RW_CHUNK = 64
RET_CHUNK = 128
RW_GN_EPS = 64e-5
ATT_TQ, ATT_TK, ATT_UNROLL = 256, 256, 8
ATT_LOOKAHEAD, ATT_SLOTS = 2, 4
VMEM_LIMIT = 56 * 1024 * 1024


def _cparams(sem, vmem=None):
    return pltpu.CompilerParams(dimension_semantics=sem, vmem_limit_bytes=vmem or VMEM_LIMIT)


def _dotf(a, b):
    return jnp.dot(a, b, precision=HI, preferred_element_type=F32)


def _dotb(a, b):
    return jnp.dot(a.astype(BF16), b.astype(BF16), preferred_element_type=F32)


def _dot_nt(a, b, precision=None):
    return lax.dot_general(a, b, (((1,), (1,)), ((), ())), precision=precision,
                           preferred_element_type=F32)


def _dot_tn(a, b, precision=None):
    return lax.dot_general(a, b, (((0,), (0,)), ((), ())), precision=precision,
                           preferred_element_type=F32)


def _sdot(a, b):
    return jnp.dot(a.astype(BF16), b.astype(BF16), preferred_element_type=F32)


def _sdot_nt(a, b):
    return _dot_nt(a.astype(BF16), b.astype(BF16))


def _sdot_tn(a, b):
    return _dot_tn(a.astype(BF16), b.astype(BF16))


def _sigmoid(x):
    return 1.0 / (1.0 + jnp.exp(-x))


def _softplus(x):
    return jnp.maximum(x, 0.0) + jnp.log1p(jnp.exp(-jnp.abs(x)))


def _rms(x, eps):
    return x * lax.rsqrt(jnp.mean(x * x, axis=-1, keepdims=True) + eps)


def _const_spec(shape, index):
    return pl.BlockSpec(shape, index, pipeline_mode=pl.Buffered(1))


def _ada_kernel(c_ref, w_ref, b_ref, o_ref):
    c = c_ref[...]
    o_ref[0] = _dotf(c * _sigmoid(c), w_ref[0]) + b_ref[0]


def _ada_mod(cc, ada_w, ada_b):
    L, D, N = ada_w.shape
    tn = N // 4
    return pl.pallas_call(
        _ada_kernel,
        grid=(L, N // tn),
        in_specs=[pl.BlockSpec((8, D), lambda l, n: (0, 0)),
                  pl.BlockSpec((1, D, tn), lambda l, n: (l, 0, n)),
                  pl.BlockSpec((1, 1, tn), lambda l, n: (l, 0, n))],
        out_specs=pl.BlockSpec((1, 8, tn), lambda l, n: (l, 0, n)),
        out_shape=jax.ShapeDtypeStruct((L, 8, N), F32),
        compiler_params=_cparams(("arbitrary", "arbitrary")),
        name="ada_mod",
    )(cc, ada_w, ada_b.reshape(L, 1, N))


def _win_kernel(x_ref, g_ref, mod_ref, *refs, n_out):
    w_refs, o_refs = refs[:n_out], refs[n_out:]
    h = _rms(x_ref[0], 1e-6) * g_ref[...]
    h = h * (1.0 + mod_ref[1:2, :]) + mod_ref[0:1, :]
    hb = h.astype(BF16)
    for w_ref, o_ref in zip(w_refs, o_refs):
        o_ref[0] = jnp.dot(hb, w_ref[...], preferred_element_type=F32).astype(o_ref.dtype)


def _input_proj(xs, gain, mods, weights, l, tm, nct):
    B, T, D = xs.shape
    n_out = len(weights)
    in_specs = [pl.BlockSpec((1, tm, D), lambda b, i: (b, i, 0)),
                pl.BlockSpec((None, 1, D), lambda b, i: (l, 0, 0)),
                pl.BlockSpec((None, None, None, 8, D),
                             lambda b, i: (l, b, jnp.minimum(i // nct, 1), 0, 0))]
    out_specs, out_shape = [], []
    for w in weights:
        n = w.shape[-1]
        in_specs.append(_const_spec((None, D, n), lambda b, i: (l, 0, 0)))
        out_specs.append(pl.BlockSpec((1, tm, n), lambda b, i: (b, i, 0)))
        out_shape.append(jax.ShapeDtypeStruct((B, T, n), F32))
    return pl.pallas_call(
        functools.partial(_win_kernel, n_out=n_out),
        grid=(B, T // tm),
        in_specs=in_specs, out_specs=out_specs, out_shape=out_shape,
        compiler_params=_cparams(("parallel", "parallel")),
        name="input_proj",
    )(xs, gain, mods, *weights)


def _rope_tables(pos_sets, lanes, group, n_ctx):
    n_sets = len(pos_sets)
    sub = group // n_sets
    half = sub // 2
    angs = []
    for pos in pos_sets:
        inv_freq = ROPE_BASE ** (-jnp.arange(half, dtype=F32) / half)
        ang = pos[:, None] * inv_freq[None, :]
        angs.append(jnp.concatenate([ang, ang], axis=-1))
    ang = jnp.concatenate(angs, axis=-1)
    ang = jnp.tile(ang, (1, lanes // group))
    ang = jnp.concatenate([jnp.zeros((n_ctx, lanes), F32), ang], axis=0)
    first = (np.arange(lanes) % sub) < half
    cos, sin = jnp.cos(ang), jnp.sin(ang)
    sin_minus = jnp.where(first[None, :], -sin, 0.0)
    sin_plus = jnp.where(first[None, :], 0.0, sin)
    return cos, sin_minus, sin_plus, half


def _rope(x, cos, sin_minus, sin_plus, half):
    n = x.shape[-1]
    return x * cos + pltpu.roll(x, n - half, 1) * sin_minus + pltpu.roll(x, half, 1) * sin_plus


def _aprep_kernel(c_ref, cos_ref, sm_ref, sp_ref, q_ref, k_ref, v_ref, *, half, scale):
    c = c_ref[0]
    cos, sm, sp = cos_ref[...], sm_ref[...], sp_ref[...]
    q = _rope(c[:, 0:BRANCH_W], cos, sm, sp, half) * scale
    k = _rope(c[:, BRANCH_W:2 * BRANCH_W], cos, sm, sp, half)
    v = c[:, 2 * BRANCH_W:3 * BRANCH_W]
    lane = lax.broadcasted_iota(jnp.int32, q.shape, 1)
    comp = (lane >> 5) & 1
    q0 = jnp.where(comp == 0, q, 0.0).astype(BF16)
    q1 = jnp.where(comp == 1, q, 0.0).astype(BF16)
    kb, vt = k.astype(BF16), v.T.astype(BF16)
    for h in range(N_HEADS):
        sl = slice(h * HEAD_W, (h + 1) * HEAD_W)
        q_ref[0, h, 0] = q0[:, sl]
        q_ref[0, h, 1] = q1[:, sl]
        k_ref[0, h] = kb[:, sl]
        v_ref[0, h] = vt[sl, :]


def _diff_prep(cols, tabs, tm):
    B, T, _ = cols.shape
    cos, sm, sp, half = tabs
    tab_spec = pl.BlockSpec((tm, BRANCH_W), lambda b, i: (i, 0))
    return pl.pallas_call(
        functools.partial(_aprep_kernel, half=half, scale=(HEAD_W // 2) ** -0.5 * math.log2(math.e)),
        grid=(B, T // tm),
        in_specs=[pl.BlockSpec((1, tm, 3 * BRANCH_W), lambda b, i: (b, i, 0)),
                  tab_spec, tab_spec, tab_spec],
        out_specs=[pl.BlockSpec((1, N_HEADS, 2, tm, HEAD_W), lambda b, i: (b, 0, 0, i, 0)),
                   pl.BlockSpec((1, N_HEADS, tm, HEAD_W), lambda b, i: (b, 0, i, 0)),
                   pl.BlockSpec((1, N_HEADS, HEAD_W, tm), lambda b, i: (b, 0, 0, i))],
        out_shape=[jax.ShapeDtypeStruct((B, N_HEADS, 2, T, HEAD_W), BF16),
                   jax.ShapeDtypeStruct((B, N_HEADS, T, HEAD_W), BF16),
                   jax.ShapeDtypeStruct((B, N_HEADS, HEAD_W, T), BF16)],
        compiler_params=_cparams(("parallel", "parallel")),
        name="diff_prep",
    )(cols, cos, sm, sp)


def _flash_kernel(lq_ref, lk_ref, g_ref, q_ref, k_ref, vt_ref, o_ref, s_sc, acc_sc, *,
                  tq, tk, unroll, n_ctx, T, lam_init):
    qi = pl.program_id(2)
    q = q_ref[0, 0].reshape(2 * tq, HEAD_W)
    ones = jnp.ones((8, tk), BF16)

    def scores(slot, off):
        s_sc[slot] = _dot_nt(k_ref[0, 0, pl.ds(off, tk), :], q)

    def absorb(slot, off, m):
        s = s_sc[slot]
        m_new = jnp.maximum(m, jnp.max(s, axis=0, keepdims=True))
        p = jnp.exp2(s - m_new).astype(BF16)
        vt = jnp.concatenate([vt_ref[0, 0, :, pl.ds(off, tk)], ones], axis=0)
        acc_sc[...] = acc_sc[...] * jnp.exp2(m - m_new) + jnp.dot(vt, p, preferred_element_type=F32)
        return m_new

    n_slots, look = s_sc.shape[0], ATT_LOOKAHEAD
    n_cc = n_ctx // tk
    acc_sc[...] = jnp.zeros(acc_sc.shape, F32)
    m = jnp.full((1, 2 * tq), NEG_INF, F32)
    for c in range(look):
        scores(c % n_slots, min(c * tk, T - tk))
    for c in range(n_cc):
        scores((c + look) % n_slots, min((c + look) * tk, T - tk))
        m = absorb(c % n_slots, c * tk, m)

    def body(jj, m):
        base = n_ctx + jj * (unroll * tk)
        for u in range(unroll):
            off = pl.multiple_of(base + u * tk, tk)
            nxt = pl.multiple_of(jnp.minimum(base + (u + look) * tk, T - tk), tk)
            scores((n_cc + u + look) % n_slots, nxt)
            m = absorb((n_cc + u) % n_slots, off, m)
        return m

    n_trips = (T - n_ctx) // (unroll * tk)
    lax.fori_loop(0, jnp.where(qi * tq < n_ctx, 0, n_trips), body, m)
    acc = acc_sc[...]
    o = acc[0:HEAD_W] / acc[HEAD_W:HEAD_W + 1]
    e0 = jnp.exp(jnp.sum(lq_ref[0:1, :] * lk_ref[0:1, :], axis=1, keepdims=True))
    e1 = jnp.exp(jnp.sum(lq_ref[1:2, :] * lk_ref[1:2, :], axis=1, keepdims=True))
    lam = e0 - e1 + lam_init
    y = o[:, :tq] - lam * o[:, tq:]
    y = y * lax.rsqrt(jnp.mean(y * y, axis=0, keepdims=True) + 1e-5) * g_ref[...] * (1.0 - lam_init)
    o_ref[0, 0] = y.T


def _diff_attention(q, k, v, lam_q, lam_k, subln, l, tq, tk, unroll, n_ctx):
    B, H, _, T, _ = q.shape
    dqk = lam_q.shape[-1]
    lam_init = 0.8 - 0.6 * math.exp(-0.3 * l)
    unroll = math.gcd(unroll, (T - n_ctx) // tk)
    assert n_ctx % tk == 0 and unroll % ATT_SLOTS == 0 and ATT_LOOKAHEAD < ATT_SLOTS
    assert (T - n_ctx) % (unroll * tk) == 0 and n_ctx % tq == 0
    kern = functools.partial(_flash_kernel, tq=tq, tk=tk, unroll=unroll, n_ctx=n_ctx, T=T,
                             lam_init=lam_init)
    return pl.pallas_call(
        kern,
        grid=(B, H, T // tq),
        in_specs=[pl.BlockSpec((None, 2, dqk), lambda b, h, i: (l, 0, 0)),
                  pl.BlockSpec((None, 2, dqk), lambda b, h, i: (l, 0, 0)),
                  pl.BlockSpec((None, None, HEAD_W, 1), lambda b, h, i: (l, h, 0, 0)),
                  pl.BlockSpec((1, 1, 2, tq, HEAD_W), lambda b, h, i: (b, h, 0, i, 0)),
                  pl.BlockSpec((1, 1, T, HEAD_W), lambda b, h, i: (b, h, 0, 0)),
                  pl.BlockSpec((1, 1, HEAD_W, T), lambda b, h, i: (b, h, 0, 0))],
        out_specs=pl.BlockSpec((1, 1, tq, HEAD_W), lambda b, h, i: (b, h, i, 0)),
        out_shape=jax.ShapeDtypeStruct((B, H, T, HEAD_W), F32),
        scratch_shapes=[pltpu.VMEM((ATT_SLOTS, tk, 2 * tq), F32), pltpu.VMEM((HEAD_W + 8, 2 * tq), F32)],
        compiler_params=_cparams(("parallel", "parallel", "arbitrary")),
        name="diff_attention",
    )(lam_q, lam_k, subln, q, k, v)


def _rwprep_kernel(c_ref, p_ref, n_ref, mu_ref, kk_ref, ka_ref, rk_ref, w0_ref, a0_ref,
                   w2_ref, a2_ref, g2_ref, ones_ref, rvk_ref, dir_ref, aux_ref, *, tm, n_ctx, T):
    i = pl.program_id(1)
    c = c_ref[0]
    t0 = i * tm
    seg_start = jnp.logical_or(t0 == 0, t0 == n_ctx)
    seg_end = jnp.logical_or(t0 + tm == n_ctx, t0 + tm == T)
    pv = jnp.where(seg_start, 0.0, p_ref[0][7:8, :])
    nx = jnp.where(seg_end, 0.0, n_ref[0][0:1, :])
    row = lax.broadcasted_iota(jnp.int32, c.shape, 0)
    prev = jnp.where(row == 0, pv, pltpu.roll(c, 1, 0))
    nxt = jnp.where(row == tm - 1, nx, pltpu.roll(c, tm - 1, 0))
    xs = c + (0.5 * (prev + nxt) - c) * mu_ref[...]
    W = BRANCH_W
    r, k, v = xs[:, 0:W], xs[:, W:2 * W], xs[:, 2 * W:3 * W]
    lo, gl = xs[:, 3 * W:4 * W], xs[:, 4 * W:]
    ones_bd = ones_ref[...]
    kk = k * kk_ref[...]
    kk = kk * lax.rsqrt(jnp.maximum(_dotf(kk * kk, ones_bd), 1e-12))
    wpre = _dotf(jnp.tanh(lo), w2_ref[...])
    apre = _dotf(lo, a2_ref[...])
    gate = _dotf(_sigmoid(gl), g2_ref[...])
    rb = jnp.zeros_like(r)
    for d in range(2):
        z = w0_ref[d:d + 1, :] + wpre[:, d * W:(d + 1) * W]
        w = -_softplus(-z) - 0.5
        a = _sigmoid(a0_ref[d:d + 1, :] + apre[:, d * W:(d + 1) * W])
        kd = k * (1.0 + (a - 1.0) * ka_ref[...])
        dir_ref[0, d, :, 0:W] = -jnp.exp(w)
        dir_ref[0, d, :, W:2 * W] = kd
        dir_ref[0, d, :, 2 * W:3 * W] = kk * a
        rb = rb + r * kd * rk_ref[...]
    rvk_ref[0, :, 0:W] = r
    rvk_ref[0, :, W:2 * W] = v
    rvk_ref[0, :, 2 * W:3 * W] = kk
    aux_ref[0, :, 0:W] = _dotf(rb, ones_bd) * v
    aux_ref[0, :, W:2 * W] = gate


def _rwkv_prep(cols, prm, l, tm, n_ctx):
    B, T, NB = cols.shape
    W = BRANCH_W
    nblk8 = T // 8
    r8 = tm // 8
    vec = lambda n: pl.BlockSpec((None, 1, n), lambda b, i: (l, 0, 0))
    pair = lambda n: pl.BlockSpec((None, 2, n), lambda b, i: (l, 0, 0))
    mat = lambda m, n: _const_spec((None, m, n), lambda b, i: (l, 0, 0))
    return pl.pallas_call(
        functools.partial(_rwprep_kernel, tm=tm, n_ctx=n_ctx, T=T),
        grid=(B, T // tm),
        in_specs=[pl.BlockSpec((1, tm, NB), lambda b, i: (b, i, 0)),
                  pl.BlockSpec((1, 8, NB), lambda b, i: (b, jnp.maximum(i * r8 - 1, 0), 0)),
                  pl.BlockSpec((1, 8, NB), lambda b, i: (b, jnp.minimum((i + 1) * r8, nblk8 - 1), 0)),
                  vec(NB), vec(W), vec(W), vec(W), pair(W), pair(W),
                  mat(W, 2 * W), mat(W, 2 * W), mat(NB - 4 * W, W),
                  _const_spec((W, W), lambda b, i: (0, 0))],
        out_specs=[pl.BlockSpec((1, tm, 3 * W), lambda b, i: (b, i, 0)),
                   pl.BlockSpec((1, 2, tm, 3 * W), lambda b, i: (b, 0, i, 0)),
                   pl.BlockSpec((1, tm, 2 * W), lambda b, i: (b, i, 0))],
        out_shape=[jax.ShapeDtypeStruct((B, T, 3 * W), F32),
                   jax.ShapeDtypeStruct((B, 2, T, 3 * W), F32),
                   jax.ShapeDtypeStruct((B, T, 2 * W), F32)],
        compiler_params=_cparams(("parallel", "parallel")),
        name="rwkv_prep",
    )(cols, cols, cols, prm["mu"], prm["kk"], prm["ka"], prm["rk"], prm["w0"], prm["a0"],
      prm["w2"], prm["a2"], prm["g2"], prm["ones_bd"])


def _rw_chunks(chains):
    C, W = RW_CHUNK, BRANCH_W
    n = len(chains)
    each = lambda f, *cols: [f(*a) for a in zip(*cols)]
    r, v, kk, lw, kd, b, s0, sgn = (list(col) for col in zip(*chains))

    ti = lax.broadcasted_iota(jnp.int32, (C, C), 0)
    tj = lax.broadcasted_iota(jnp.int32, (C, C), 1)
    lane = lax.broadcasted_iota(jnp.int32, (1, W), 1)
    head_of_lane = lane >> 6
    stack = lambda x: jnp.concatenate(
        [jnp.where(head_of_lane == h, x, 0.0) for h in range(N_HEADS)], axis=0)
    tile = lambda x: jnp.concatenate([x] * N_HEADS, axis=0)
    ri = lax.broadcasted_iota(jnp.int32, (W, W), 0)
    ci = lax.broadcasted_iota(jnp.int32, (W, W), 1)
    same_head = (ri >> 6) == (ci >> 6)
    dt = (ci & (C - 1)) - (ri & (C - 1))
    masks = {sg: (jnp.where((tj - ti) * sg <= 0, 1.0, 0.0),
                  jnp.logical_and(same_head, dt * sg < 0),
                  jnp.logical_and(same_head, dt * sg <= 0)) for sg in set(sgn)}
    tri = [masks[sg][0] for sg in sgn]
    strict = [masks[sg][1] for sg in sgn]
    incl = [masks[sg][2] for sg in sgn]
    blk16 = (ri >> 4) == (ci >> 4)
    blk32 = (ri >> 5) == (ci >> 5)
    eye = jnp.where(ri == ci, 1.0, 0.0)

    log_g = each(_dotf, tri, lw)
    log_gc = [jnp.sum(x, axis=0, keepdims=True) for x in lw]
    kk_t = each(lambda kk_, lg, lw_: kk_ * jnp.exp(lg - lw_), kk, log_g, lw)
    r_t = each(lambda r_, lg: r_ * jnp.exp(lg), r, log_g)
    g_inv = [jnp.exp(-lg) for lg in log_g]
    g_tail = each(lambda lgc, lg: jnp.exp(lgc - lg), log_gc, log_g)

    left = each(lambda a, c: jnp.concatenate([stack(a), stack(c)], axis=0).astype(BF16), kk_t, r_t)
    right = each(lambda b_, kd_, gi: jnp.concatenate([stack(b_ * gi), stack(kd_ * gi)], axis=0).astype(BF16),
                 b, kd, g_inv)
    g = each(_dot_nt, left, right)
    m_sl = each(lambda st, g_: jnp.where(st, g_[:W, :W], 0.0), strict, g)
    n_sl = each(lambda st, g_: jnp.where(st, g_[:W, W:], 0.0), strict, g)
    pq_l = each(lambda ic, g_: jnp.concatenate([jnp.where(ic, g_[W:, :W], 0.0),
                                                jnp.where(ic, g_[W:, W:], 0.0)], axis=1).astype(BF16), incl, g)

    g2 = each(lambda a, c, s_: _sdot(jnp.concatenate([a, c], axis=0), s_), kk_t, r_t, s0)
    vs = [tile(x).astype(BF16) for x in v]
    rhs = each(lambda g2_, n_, vs_: -(tile(g2_[:C]) + _sdot(n_, vs_)), g2, n_sl, vs)

    d0 = [jnp.where(blk16, m_, 0.0).astype(BF16) for m_ in m_sl]
    a2 = each(_sdot, d0, d0)
    a4 = each(_sdot, a2, a2)
    a8 = each(_sdot, a4, a4)
    t = [eye - d_ for d_ in d0]
    for a_ in (a2, a4, a8):
        t = each(lambda t_, x: t_ + _sdot(t_, x), t, a_)
    for cm in (jnp.logical_and(blk32, jnp.logical_not(blk16)), jnp.logical_not(blk32)):
        u = each(lambda m_, t_: _sdot(jnp.where(cm, m_, 0.0), t_), m_sl, t)
        t = each(lambda t_, u_: t_ - _sdot(t_, u_), t, u)

    sa = each(_sdot, t, rhs)
    sav = each(lambda sa_, vs_: jnp.concatenate([sa_.astype(BF16), vs_], axis=0), sa, vs)
    ys = each(lambda g2_, pq, sav_: tile(g2_[C:]) + _sdot(pq, sav_), g2, pq_l, sav)
    ds = each(lambda b_, kd_, gt, sav_: _sdot_tn(jnp.concatenate([stack(b_ * gt), stack(kd_ * gt)], axis=0),
                                                 sav_), b, kd, g_tail, sav)
    out = []
    for i in range(n):
        y = jnp.zeros((C, W), F32)
        for h in range(N_HEADS):
            y = y + jnp.where(head_of_lane == h, ys[i][h * C:(h + 1) * C], 0.0)
        gc_col = jnp.sum(jnp.where(ri == ci, jnp.exp(log_gc[i]), 0.0), axis=1, keepdims=True)
        out.append((y, jnp.where(same_head, s0[i] * gc_col + ds[i], 0.0)))
    return out


def _rwscan_kernel(rvf_ref, rvb_ref, df_ref, db_ref, yf_ref, yb_ref, s_ref):
    W = BRANCH_W

    @pl.when(pl.program_id(0) == 0)
    def _():
        s_ref[...] = jnp.zeros(s_ref.shape, F32)

    ids = [(bi, d) for bi in range(rvf_ref.shape[0]) for d in range(2)]
    refs = ((rvf_ref, df_ref, yf_ref), (rvb_ref, db_ref, yb_ref))
    chains = []
    for bi, d in ids:
        rv, dr, _ = refs[d]
        chains.append((rv[bi, :, 0:W], rv[bi, :, W:2 * W], rv[bi, :, 2 * W:3 * W],
                       dr[bi, 0, :, 0:W], dr[bi, 0, :, W:2 * W], dr[bi, 0, :, 2 * W:3 * W],
                       s_ref[2 * bi + d], 1 - 2 * d))
    for (bi, d), (y, s_new) in zip(ids, _rw_chunks(chains)):
        refs[d][2][bi] = y
        s_ref[2 * bi + d] = s_new


def _chunk_order(s, d, n_ctx_chunks, n_chunks):
    bwd = jnp.where(s < n_ctx_chunks, n_ctx_chunks - 1 - s, n_chunks - 1 - (s - n_ctx_chunks))
    return jnp.where(d == 0, s, bwd)


def _rwkv_scan(rvk, dirs, n_ctx):
    B, T, _ = rvk.shape
    C, W = RW_CHUNK, BRANCH_W
    nc, ncx = T // C, n_ctx // C
    ch = lambda s, d: _chunk_order(s, d, ncx, nc)
    rv_spec = lambda d: pl.BlockSpec((B, C, 3 * W), lambda s: (0, ch(s, d), 0))
    dir_spec = lambda d: pl.BlockSpec((B, 1, C, 3 * W), lambda s: (0, d, ch(s, d), 0))
    y_spec = lambda d: pl.BlockSpec((B, C, W), lambda s: (0, ch(s, d), 0))
    return pl.pallas_call(
        _rwscan_kernel,
        grid=(nc,),
        in_specs=[rv_spec(0), rv_spec(1), dir_spec(0), dir_spec(1)],
        out_specs=[y_spec(0), y_spec(1)],
        out_shape=[jax.ShapeDtypeStruct((B, T, W), F32)] * 2,
        scratch_shapes=[pltpu.VMEM((2 * B, W, W), F32)],
        compiler_params=_cparams(("arbitrary",)),
        name="rwkv_scan",
    )(rvk, rvk, dirs, dirs)


def _ret_kernel(c_ref, cos_ref, sm_ref, sp_ref, dec_ref, o_ref, s_ref, *, half, scale):
    C, W = RET_CHUNK, BRANCH_W
    QW = W // 2
    d = pl.program_id(1)
    s = pl.program_id(2)

    @pl.when(s == 0)
    def _():
        s_ref[...] = jnp.zeros(s_ref.shape, F32)

    c = c_ref[0]
    cos, sm, sp = cos_ref[...], sm_ref[...], sp_ref[...]
    q = _rope(c[:, 0:QW], cos, sm, sp, half)
    k = _rope(c[:, QW:2 * QW], cos, sm, sp, half) * scale
    v = c[:, 2 * QW:2 * QW + W]
    log_g = -_softplus(-dec_ref[...])
    lane_q = lax.broadcasted_iota(jnp.int32, (1, QW), 1) >> 5
    lane_v = lax.broadcasted_iota(jnp.int32, (1, W), 1) >> 6
    lg_q = jnp.zeros((1, QW), F32)
    lg_v = jnp.zeros((1, W), F32)
    for h in range(N_HEADS):
        lg_q = jnp.where(lane_q == h, log_g[:, h:h + 1], lg_q)
        lg_v = jnp.where(lane_v == h, log_g[:, h:h + 1], lg_v)
    fwd = d == 0
    idx = lax.broadcasted_iota(jnp.int32, (C, 1), 0).astype(F32)
    q_dec = jnp.exp(jnp.where(fwd, idx + 1.0, C - idx) * lg_q)
    k_dec = jnp.exp(jnp.where(fwd, C - 1.0 - idx, idx) * lg_q)
    ti = lax.broadcasted_iota(jnp.int32, (C, C), 0)
    tj = lax.broadcasted_iota(jnp.int32, (C, C), 1)
    rel = ((ti - tj) * (1 - 2 * d)).astype(F32)

    s0 = s_ref[...]
    o = _dotb(q * q_dec, s0)
    kb = k.astype(BF16)
    vb = v.astype(BF16)
    for h in range(N_HEADS):
        dm = jnp.where(rel >= 0, jnp.exp(jnp.maximum(rel, 0.0) * log_g[:, h:h + 1]), 0.0)
        qh = jnp.where(lane_q == h, q, 0.0).astype(BF16)
        sc = _dot_nt(qh, kb) * dm
        o = o + jnp.where(lane_v == h, jnp.dot(sc.astype(BF16), vb, preferred_element_type=F32), 0.0)
    o_ref[0, 0] = o

    ri = lax.broadcasted_iota(jnp.int32, (QW, W), 0) >> 5
    ci = lax.broadcasted_iota(jnp.int32, (QW, W), 1) >> 6
    upd = _dot_tn((k * k_dec).astype(BF16), vb)
    s_ref[...] = jnp.where(ri == ci, s0 * jnp.exp(C * lg_v) + upd, 0.0)


def _retention(cols, tabs, decay, l, n_ctx):
    B, T, NCOL = cols.shape
    C, W = RET_CHUNK, BRANCH_W
    cos, sm, sp, half = tabs
    nc, ncx = T // C, n_ctx // C
    ch = lambda s, d: _chunk_order(s, d, ncx, nc)
    tab_spec = pl.BlockSpec((C, W // 2), lambda b, d, s: (ch(s, d), 0))
    return pl.pallas_call(
        functools.partial(_ret_kernel, half=half, scale=(HEAD_W // 2) ** -0.5),
        grid=(B, 2, nc),
        in_specs=[pl.BlockSpec((1, C, NCOL), lambda b, d, s: (b, ch(s, d), 0)),
                  tab_spec, tab_spec, tab_spec,
                  pl.BlockSpec((None, None, 1, N_HEADS), lambda b, d, s: (l, d, 0, 0))],
        out_specs=pl.BlockSpec((1, 1, C, W), lambda b, d, s: (b, d, ch(s, d), 0)),
        out_shape=jax.ShapeDtypeStruct((B, 2, T, W), F32),
        scratch_shapes=[pltpu.VMEM((W // 2, W), F32)],
        compiler_params=_cparams(("parallel", "parallel", "arbitrary")),
        name="retention",
    )(cols, cos, sm, sp, decay)


def _dprep_kernel(c_ref, cos_ref, sm_ref, sp_ref, q_ref, k_ref, v_ref, *, half, scale):
    W = BRANCH_W
    c = c_ref[0]
    cos, sm, sp = cos_ref[...], sm_ref[...], sp_ref[...]
    q = (_rope(c[:, 0:W], cos, sm, sp, half) * scale).astype(BF16)
    kw = W // 2
    k = _rope(c[:, W:W + kw], cos[:, :kw], sm[:, :kw], sp[:, :kw], half).astype(BF16)
    v = c[:, W + kw:W + 2 * kw].astype(BF16)
    for kvh in range(2):
        for g in range(2):
            h = kvh * 2 + g
            q_ref[0, kvh, g] = q[:, h * HEAD_W:(h + 1) * HEAD_W]
        k_ref[0, kvh] = k[:, kvh * HEAD_W:(kvh + 1) * HEAD_W]
        v_ref[0, kvh] = v[:, kvh * HEAD_W:(kvh + 1) * HEAD_W]


def _win_prep(cols, tabs, tm):
    B, T, NCOL = cols.shape
    cos, sm, sp, half = tabs
    tab_spec = pl.BlockSpec((tm, BRANCH_W), lambda b, i: (i, 0))
    return pl.pallas_call(
        functools.partial(_dprep_kernel, half=half, scale=HEAD_W ** -0.5),
        grid=(B, T // tm),
        in_specs=[pl.BlockSpec((1, tm, NCOL), lambda b, i: (b, i, 0)), tab_spec, tab_spec, tab_spec],
        out_specs=[pl.BlockSpec((1, 2, 2, tm, HEAD_W), lambda b, i: (b, 0, 0, i, 0)),
                   pl.BlockSpec((1, 2, tm, HEAD_W), lambda b, i: (b, 0, i, 0)),
                   pl.BlockSpec((1, 2, tm, HEAD_W), lambda b, i: (b, 0, i, 0))],
        out_shape=[jax.ShapeDtypeStruct((B, 2, 2, T, HEAD_W), BF16),
                   jax.ShapeDtypeStruct((B, 2, T, HEAD_W), BF16),
                   jax.ShapeDtypeStruct((B, 2, T, HEAD_W), BF16)],
        compiler_params=_cparams(("parallel", "parallel")),
        name="win_prep",
    )(cols, cos, sm, sp)


def _winattn_kernel(sink_ref, q_ref, k_ref, v_ref, o_ref, *, tq, n_ctx, T):
    i = pl.program_id(2)
    band = tq + 2 * WINDOW
    q = q_ref[0, 0].reshape(2 * tq, HEAD_W)
    kx, vx = k_ref[0, 0, 0:n_ctx, :], v_ref[0, 0, 0:n_ctx, :]
    ws = pl.multiple_of(jnp.clip(i * tq - WINDOW, n_ctx, T - band), WINDOW)
    kb, vb = k_ref[0, 0, pl.ds(ws, band), :], v_ref[0, 0, pl.ds(ws, band), :]
    s_ctx = _dot_nt(q, kx)
    s_band = _dot_nt(q, kb)
    qpos = i * tq + lax.broadcasted_iota(jnp.int32, (tq, band), 0)
    kpos = ws + lax.broadcasted_iota(jnp.int32, (tq, band), 1)
    valid = jnp.logical_and(jnp.abs(qpos - kpos) <= WINDOW, qpos >= n_ctx)
    valid = jnp.concatenate([valid, valid], axis=0)
    s_band = jnp.where(valid, s_band, NEG_INF)
    sink = sink_ref[...]
    m = jnp.maximum(jnp.maximum(jnp.max(s_ctx, axis=1, keepdims=True),
                                jnp.max(s_band, axis=1, keepdims=True)), sink)
    p_ctx = jnp.exp(s_ctx - m)
    p_band = jnp.exp(s_band - m)
    den = (jnp.sum(p_ctx, axis=1, keepdims=True) + jnp.sum(p_band, axis=1, keepdims=True)
           + jnp.exp(sink - m))
    o = (jnp.dot(p_ctx.astype(BF16), vx, preferred_element_type=F32)
         + jnp.dot(p_band.astype(BF16), vb, preferred_element_type=F32)) / den
    o_ref[0, 0] = o.reshape(2, tq, HEAD_W)


def _window_attention(q, k, v, sink_col, l, tq, n_ctx):
    B, KVH, G, T, _ = q.shape
    return pl.pallas_call(
        functools.partial(_winattn_kernel, tq=tq, n_ctx=n_ctx, T=T),
        grid=(B, KVH, T // tq),
        in_specs=[pl.BlockSpec((None, None, G * tq, 1), lambda b, h, i: (l, h, 0, 0)),
                  pl.BlockSpec((1, 1, G, tq, HEAD_W), lambda b, h, i: (b, h, 0, i, 0)),
                  pl.BlockSpec((1, 1, T, HEAD_W), lambda b, h, i: (b, h, 0, 0)),
                  pl.BlockSpec((1, 1, T, HEAD_W), lambda b, h, i: (b, h, 0, 0))],
        out_specs=pl.BlockSpec((1, 1, G, tq, HEAD_W), lambda b, h, i: (b, h, 0, i, 0)),
        out_shape=jax.ShapeDtypeStruct((B, KVH, G, T, HEAD_W), F32),
        compiler_params=_cparams(("parallel", "parallel", "arbitrary")),
        name="window_attention",
    )(sink_col, q, k, v)


def _group_norm(x, ones_bd, eps, center):
    inv = 1.0 / HEAD_W
    if center:
        x = x - _dotf(x, ones_bd) * inv
    return x * lax.rsqrt(_dotf(x * x, ones_bd) * inv + eps)


def _merge_kernel(x_ref, gates_ref, ya_ref, ybf_ref, ybb_ref, aux_ref, oc_ref, gc_ref, yd_ref,
                  lnx_g_ref, lnx_b_ref, gn_ref, wb_ref, wo_ref, pn_ref, mod_ref, ones_ref, o_ref):
    W = BRANCH_W
    D = x_ref.shape[-1]
    ones_bd = ones_ref[...]
    yb = _group_norm(ybf_ref[0] + ybb_ref[0], ones_bd, RW_GN_EPS, True)
    yb = yb * lnx_g_ref[...] + lnx_b_ref[...]
    yb = (yb + aux_ref[0, :, 0:W]) * aux_ref[0, :, W:2 * W]
    yc = _group_norm(oc_ref[0, 0] + oc_ref[0, 1], ones_bd, 1e-5, True) * gn_ref[...]
    gc = gc_ref[0]
    yc = yc * (gc * _sigmoid(gc))

    def lift_heads(y_ref4, n):
        acc = None
        for h in range(N_HEADS):
            t = jnp.dot(y_ref4(h).astype(BF16), wb_ref[n, h * HEAD_W:(h + 1) * HEAD_W, :],
                        preferred_element_type=F32)
            acc = t if acc is None else acc + t
        return acc

    lifted = [lift_heads(lambda h: ya_ref[0, h], 0),
              jnp.dot(yb.astype(BF16), wb_ref[1], preferred_element_type=F32),
              jnp.dot(yc.astype(BF16), wb_ref[2], preferred_element_type=F32),
              lift_heads(lambda h: yd_ref[0, h // 2, h % 2], 3)]
    m = None
    for n in range(4):
        t = _sigmoid(gates_ref[0, :, n * D:(n + 1) * D]) * lifted[n]
        m = t if m is None else m + t
    out = jnp.dot(m.astype(BF16), wo_ref[...], preferred_element_type=F32)
    o_ref[0] = x_ref[0] + mod_ref[2:3, :] * (_rms(out, 1e-6) * pn_ref[...])


def _merge(xs, gates, ya, yb, aux, oc, cols_c, yd, prm, mods, l, tm, nct):
    B, T, D = xs.shape
    W = BRANCH_W
    vec = lambda n: pl.BlockSpec((None, 1, n), lambda b, i: (l, 0, 0))
    return pl.pallas_call(
        _merge_kernel,
        grid=(B, T // tm),
        in_specs=[pl.BlockSpec((1, tm, D), lambda b, i: (b, i, 0)),
                  pl.BlockSpec((1, tm, 4 * D), lambda b, i: (b, i, 0)),
                  pl.BlockSpec((1, N_HEADS, tm, HEAD_W), lambda b, i: (b, 0, i, 0)),
                  pl.BlockSpec((1, tm, W), lambda b, i: (b, i, 0)),
                  pl.BlockSpec((1, tm, W), lambda b, i: (b, i, 0)),
                  pl.BlockSpec((1, tm, 2 * W), lambda b, i: (b, i, 0)),
                  pl.BlockSpec((1, 2, tm, W), lambda b, i: (b, 0, i, 0)),
                  pl.BlockSpec((1, tm, W), lambda b, i: (b, i, 2)),
                  pl.BlockSpec((1, 2, 2, tm, HEAD_W), lambda b, i: (b, 0, 0, i, 0)),
                  vec(W), vec(W), vec(W),
                  _const_spec((None, 4, W, D), lambda b, i: (l, 0, 0, 0)),
                  _const_spec((None, D, D), lambda b, i: (l, 0, 0)),
                  vec(D),
                  pl.BlockSpec((None, None, None, 8, D),
                               lambda b, i: (l, b, jnp.minimum(i // nct, 1), 0, 0)),
                  _const_spec((W, W), lambda b, i: (0, 0))],
        out_specs=pl.BlockSpec((1, tm, D), lambda b, i: (b, i, 0)),
        out_shape=jax.ShapeDtypeStruct((B, T, D), F32),
        compiler_params=_cparams(("parallel", "parallel")),
        name="merge",
    )(xs, gates, ya, yb[0], yb[1], aux, oc, cols_c, yd, prm["lnx_g"], prm["lnx_b"], prm["ret_gn"],
      prm["w_branch"], prm["w_out"], prm["norm_post_mix"], mods, prm["ones_bd"])


def _mlp_kernel(x_ref, g_ref, pn_ref, mod_ref, wu_ref, wd_ref, o_ref):
    x = x_ref[0]
    h = _rms(x, 1e-6) * g_ref[...]
    h = h * (1.0 + mod_ref[4:5, :]) + mod_ref[3:4, :]
    u = jnp.dot(h.astype(BF16), wu_ref[...], preferred_element_type=F32)
    u = jnp.square(jnp.maximum(u, 0.0))
    y = jnp.dot(u.astype(BF16), wd_ref[...], preferred_element_type=F32)
    o_ref[0] = x + mod_ref[5:6, :] * (_rms(y, 1e-6) * pn_ref[...])


def _mlp(xs, prm, mods, l, tm, nct):
    B, T, D = xs.shape
    F = prm["w_up"].shape[-1]
    vec = lambda n: pl.BlockSpec((None, 1, n), lambda b, i: (l, 0, 0))
    return pl.pallas_call(
        _mlp_kernel,
        grid=(B, T // tm),
        in_specs=[pl.BlockSpec((1, tm, D), lambda b, i: (b, i, 0)),
                  vec(D), vec(D),
                  pl.BlockSpec((None, None, None, 8, D),
                               lambda b, i: (l, b, jnp.minimum(i // nct, 1), 0, 0)),
                  _const_spec((None, D, F), lambda b, i: (l, 0, 0)),
                  _const_spec((None, F, D), lambda b, i: (l, 0, 0))],
        out_specs=pl.BlockSpec((1, tm, D), lambda b, i: (b, i, 0)),
        out_shape=jax.ShapeDtypeStruct((B, T, D), F32),
        compiler_params=_cparams(("parallel", "parallel")),
        name="mlp",
    )(xs, prm["norm_pre_mlp"], prm["norm_post_mlp"], mods, prm["w_up"], prm["w_down"])


def kernel(x, c, ctx, c_ctx, ada_w, ada_b, norm_pre_mix, norm_post_mix, norm_pre_mlp, norm_post_mlp,
           w_in, diff_lam_q, diff_lam_k, diff_subln, rwkv_mu, rwkv_w0, rwkv_w2, rwkv_a0, rwkv_a2,
           rwkv_g2, rwkv_kk, rwkv_ka, rwkv_rk, rwkv_lnx_g, rwkv_lnx_b, ret_decay, ret_gn, win_sink,
           w_branch, w_out, w_up, w_down):
    B, S, D = x.shape
    n_ctx = ctx.shape[1]
    T = n_ctx + S
    L = ada_w.shape[0]
    W = BRANCH_W
    TM = 256
    nct = n_ctx // TM
    assert n_ctx % TM == 0 and S % TM == 0 and S % GRID_W == 0 and B + 1 <= 8

    cc = jnp.zeros((8, D), F32).at[:B].set(c).at[B].set(c_ctx)
    mod = _ada_mod(cc, ada_w, ada_b).reshape(L, 8, 6, D)
    mod_ctx = jnp.broadcast_to(mod[:, B][:, None], (L, B, 6, D))
    mods = jnp.stack([mod_ctx, mod[:, :B]], axis=2)
    mods = jnp.pad(mods, ((0, 0), (0, 0), (0, 0), (0, 2), (0, 0)))

    lat = jnp.arange(S, dtype=jnp.int32)
    row = (lat // GRID_W).astype(F32)
    col = (lat % GRID_W).astype(F32)
    tabs_a = _rope_tables([row, col], W, HEAD_W // 2, n_ctx)
    tabs_d = _rope_tables([row, col], W, HEAD_W, n_ctx)
    tabs_c = _rope_tables([lat.astype(F32)], W // 2, HEAD_W // 2, n_ctx)

    sizes = (3 * W, rwkv_mu.shape[-1], 3 * W, 2 * W, 4 * D)
    offs = np.concatenate([[0], np.cumsum(sizes)])
    w_groups = [w_in[:, :, int(offs[n]):int(offs[n + 1])].astype(BF16) for n in range(5)]

    lw_w, la_w = rwkv_w2.shape[2], rwkv_a2.shape[2]
    zw = jnp.zeros((L, W, 2 * W), F32)
    w2cat = (zw.at[:, 0:lw_w, 0:W].set(rwkv_w2[:, 0])
             .at[:, lw_w:2 * lw_w, W:2 * W].set(rwkv_w2[:, 1]))
    a2cat = (zw.at[:, 2 * lw_w:2 * lw_w + la_w, 0:W].set(rwkv_a2[:, 0])
             .at[:, 2 * lw_w + la_w:2 * lw_w + 2 * la_w, W:2 * W].set(rwkv_a2[:, 1]))
    blk = np.arange(W) // HEAD_W
    ones_bd = jnp.asarray((blk[:, None] == blk[None, :]).astype(np.float32))
    v3 = lambda a: a.reshape(L, 1, -1)
    prm = dict(mu=v3(rwkv_mu), kk=v3(rwkv_kk), ka=v3(rwkv_ka), rk=v3(rwkv_rk), w0=rwkv_w0, a0=rwkv_a0,
               w2=w2cat, a2=a2cat, g2=rwkv_g2, ones_bd=ones_bd,
               lnx_g=v3(rwkv_lnx_g), lnx_b=v3(rwkv_lnx_b), ret_gn=v3(ret_gn),
               w_branch=w_branch.astype(BF16), w_out=w_out.astype(BF16),
               norm_post_mix=v3(norm_post_mix), norm_pre_mlp=v3(norm_pre_mlp),
               norm_post_mlp=v3(norm_post_mlp), w_up=w_up.astype(BF16), w_down=w_down.astype(BF16))
    subln = diff_subln.reshape(L, N_HEADS, HEAD_W, 1)
    decay = ret_decay.reshape(L, 2, 1, N_HEADS)
    TQ_D = 256
    sink_col = jnp.broadcast_to(win_sink.reshape(L, 2, 2, 1, 1), (L, 2, 2, TQ_D, 1)).reshape(L, 2, 2 * TQ_D, 1)
    gain_pre = v3(norm_pre_mix)

    xs = jnp.concatenate([ctx, x], axis=1)
    for l in range(L):
        cols_a, cols_b, cols_c, cols_d, gates = _input_proj(xs, gain_pre, mods, w_groups, l, TM, nct)
        qa, ka, va = _diff_prep(cols_a, tabs_a, TM)
        ya = _diff_attention(qa, ka, va, diff_lam_q, diff_lam_k, subln, l, ATT_TQ, ATT_TK, ATT_UNROLL, n_ctx)
        rvk, dirs, aux = _rwkv_prep(cols_b, prm, l, TM, n_ctx)
        yb = _rwkv_scan(rvk, dirs, n_ctx)
        oc = _retention(cols_c, tabs_c, decay, l, n_ctx)
        qd, kd, vd = _win_prep(cols_d, tabs_d, TM)
        yd = _window_attention(qd, kd, vd, sink_col, l, TQ_D, n_ctx)
        xs = _merge(xs, gates, ya, yb, aux, oc, cols_c, yd, prm, mods, l, TM, nct)
        xs = _mlp(xs, prm, mods, l, TM, nct)
    return xs[:, n_ctx:]
```

```python
import functools
import math

import numpy as np
import jax
import jax.numpy as jnp
from jax import lax
from jax.experimental import pallas as pl
from jax.experimental.pallas import tpu as pltpu

F32 = jnp.float32
BF16 = jnp.bfloat16
HI = lax.Precision.HIGHEST

GRID_W = 64
ROPE_BASE = 10000.0
NEG_INF = -1e30
WINDOW = 128
N_HEADS = 4
HEAD_W = 64
BRANCH_W = N_HEADS * HEAD_W
RW_CHUNK = 64
RET_CHUNK = 128
RW_GN_EPS = 64e-5
ATT_TQ, ATT_TK, ATT_UNROLL = 256, 256, 64
ATT_LOOKAHEAD, ATT_SLOTS = 2, 4
VMEM_LIMIT = 56 * 1024 * 1024


def _cparams(sem, vmem=None):
    return pltpu.CompilerParams(dimension_semantics=sem, vmem_limit_bytes=vmem or VMEM_LIMIT)


def _dotf(a, b):
    return jnp.dot(a, b, precision=HI, preferred_element_type=F32)


def _dotb(a, b):
    return jnp.dot(a.astype(BF16), b.astype(BF16), preferred_element_type=F32)


def _dot_nt(a, b, precision=None):
    return lax.dot_general(a, b, (((1,), (1,)), ((), ())), precision=precision,
                           preferred_element_type=F32)


def _dot_tn(a, b, precision=None):
    return lax.dot_general(a, b, (((0,), (0,)), ((), ())), precision=precision,
                           preferred_element_type=F32)


def _sdot(a, b):
    return jnp.dot(a.astype(BF16), b.astype(BF16), preferred_element_type=F32)


def _sdot_nt(a, b):
    return _dot_nt(a.astype(BF16), b.astype(BF16))


def _sdot_tn(a, b):
    return _dot_tn(a.astype(BF16), b.astype(BF16))


def _sigmoid(x):
    return 1.0 / (1.0 + jnp.exp(-x))


def _softplus(x):
    return jnp.maximum(x, 0.0) + jnp.log1p(jnp.exp(-jnp.abs(x)))


def _rms(x, eps):
    return x * lax.rsqrt(jnp.mean(x * x, axis=-1, keepdims=True) + eps)


def _const_spec(shape, index):
    return pl.BlockSpec(shape, index, pipeline_mode=pl.Buffered(1))


def _ada_kernel(c_ref, w_ref, b_ref, o_ref):
    c = c_ref[...]
    o_ref[0] = _dotf(c * _sigmoid(c), w_ref[0]) + b_ref[0]


def _ada_mod(cc, ada_w, ada_b):
    L, D, N = ada_w.shape
    tn = N // 4
    return pl.pallas_call(
        _ada_kernel,
        grid=(L, N // tn),
        in_specs=[pl.BlockSpec((8, D), lambda l, n: (0, 0)),
                  pl.BlockSpec((1, D, tn), lambda l, n: (l, 0, n)),
                  pl.BlockSpec((1, 1, tn), lambda l, n: (l, 0, n))],
        out_specs=pl.BlockSpec((1, 8, tn), lambda l, n: (l, 0, n)),
        out_shape=jax.ShapeDtypeStruct((L, 8, N), F32),
        compiler_params=_cparams(("arbitrary", "arbitrary")),
        name="ada_mod",
    )(cc, ada_w, ada_b.reshape(L, 1, N))


def _win_kernel(x_ref, g_ref, mod_ref, *refs, n_out):
    w_refs, o_refs = refs[:n_out], refs[n_out:]
    h = _rms(x_ref[0], 1e-6) * g_ref[...]
    h = h * (1.0 + mod_ref[1:2, :]) + mod_ref[0:1, :]
    hb = h.astype(BF16)
    for w_ref, o_ref in zip(w_refs, o_refs):
        o_ref[0] = jnp.dot(hb, w_ref[...], preferred_element_type=F32).astype(o_ref.dtype)


def _input_proj(xs, gain, mods, weights, out_dtypes, l, tm, nct):
    B, T, D = xs.shape
    n_out = len(weights)
    in_specs = [pl.BlockSpec((1, tm, D), lambda b, i: (b, i, 0)),
                pl.BlockSpec((None, 1, D), lambda b, i: (l, 0, 0)),
                pl.BlockSpec((None, None, None, 8, D),
                             lambda b, i: (l, b, jnp.minimum(i // nct, 1), 0, 0))]
    out_specs, out_shape = [], []
    for w, dt in zip(weights, out_dtypes):
        n = w.shape[-1]
        in_specs.append(_const_spec((None, D, n), lambda b, i: (l, 0, 0)))
        out_specs.append(pl.BlockSpec((1, tm, n), lambda b, i: (b, i, 0)))
        out_shape.append(jax.ShapeDtypeStruct((B, T, n), dt))
    return pl.pallas_call(
        functools.partial(_win_kernel, n_out=n_out),
        grid=(B, T // tm),
        in_specs=in_specs, out_specs=out_specs, out_shape=out_shape,
        compiler_params=_cparams(("parallel", "parallel")),
        name="input_proj",
    )(xs, gain, mods, *weights)


def _rope_tables(pos_sets, lanes, group, n_ctx):
    n_sets = len(pos_sets)
    sub = group // n_sets
    half = sub // 2
    angs = []
    for pos in pos_sets:
        inv_freq = ROPE_BASE ** (-jnp.arange(half, dtype=F32) / half)
        ang = pos[:, None] * inv_freq[None, :]
        angs.append(jnp.concatenate([ang, ang], axis=-1))
    ang = jnp.concatenate(angs, axis=-1)
    ang = jnp.tile(ang, (1, lanes // group))
    ang = jnp.concatenate([jnp.zeros((n_ctx, lanes), F32), ang], axis=0)
    first = (np.arange(lanes) % sub) < half
    cos, sin = jnp.cos(ang), jnp.sin(ang)
    sin_minus = jnp.where(first[None, :], -sin, 0.0)
    sin_plus = jnp.where(first[None, :], 0.0, sin)
    return cos, sin_minus, sin_plus, half


def _rope(x, cos, sin_minus, sin_plus, half):
    n = x.shape[-1]
    return x * cos + pltpu.roll(x, n - half, 1) * sin_minus + pltpu.roll(x, half, 1) * sin_plus


def _aprep_kernel(c_ref, cos_ref, sm_ref, sp_ref, q_ref, k_ref, v_ref, *, half, scale):
    c = c_ref[0]
    cos, sm, sp = cos_ref[...], sm_ref[...], sp_ref[...]
    q = _rope(c[:, 0:BRANCH_W], cos, sm, sp, half) * scale
    k = _rope(c[:, BRANCH_W:2 * BRANCH_W], cos, sm, sp, half)
    v = c[:, 2 * BRANCH_W:3 * BRANCH_W]
    lane = lax.broadcasted_iota(jnp.int32, q.shape, 1)
    comp = (lane >> 5) & 1
    q0 = jnp.where(comp == 0, q, 0.0).astype(BF16)
    q1 = jnp.where(comp == 1, q, 0.0).astype(BF16)
    kb, vt = k.astype(BF16), v.T.astype(BF16)
    for h in range(N_HEADS):
        sl = slice(h * HEAD_W, (h + 1) * HEAD_W)
        q_ref[0, h, 0] = q0[:, sl]
        q_ref[0, h, 1] = q1[:, sl]
        k_ref[0, h] = kb[:, sl]
        v_ref[0, h] = vt[sl, :]


def _diff_prep(cols, tabs, tm):
    B, T, _ = cols.shape
    cos, sm, sp, half = tabs
    tab_spec = pl.BlockSpec((tm, BRANCH_W), lambda b, i: (i, 0))
    return pl.pallas_call(
        functools.partial(_aprep_kernel, half=half, scale=(HEAD_W // 2) ** -0.5 * math.log2(math.e)),
        grid=(B, T // tm),
        in_specs=[pl.BlockSpec((1, tm, 3 * BRANCH_W), lambda b, i: (b, i, 0)),
                  tab_spec, tab_spec, tab_spec],
        out_specs=[pl.BlockSpec((1, N_HEADS, 2, tm, HEAD_W), lambda b, i: (b, 0, 0, i, 0)),
                   pl.BlockSpec((1, N_HEADS, tm, HEAD_W), lambda b, i: (b, 0, i, 0)),
                   pl.BlockSpec((1, N_HEADS, HEAD_W, tm), lambda b, i: (b, 0, 0, i))],
        out_shape=[jax.ShapeDtypeStruct((B, N_HEADS, 2, T, HEAD_W), BF16),
                   jax.ShapeDtypeStruct((B, N_HEADS, T, HEAD_W), BF16),
                   jax.ShapeDtypeStruct((B, N_HEADS, HEAD_W, T), BF16)],
        compiler_params=_cparams(("parallel", "parallel")),
        name="diff_prep",
    )(cols, cos, sm, sp)


def _flash_kernel(lq_ref, lk_ref, g_ref, q_ref, k_ref, vt_ref, o_ref, s_sc, acc_sc, *,
                  tq, tk, unroll, n_ctx, T, lam_init):
    qi = pl.program_id(2)
    q = q_ref[0, 0].reshape(2 * tq, HEAD_W)
    def scores(slot, off, size):
        s_sc[slot, 0:size] = _dot_nt(k_ref[0, 0, pl.ds(off, size), :], q)

    def absorb(slot, off, size, m):
        s = s_sc[slot, 0:size]
        m_new = jnp.maximum(m, jnp.max(s, axis=0, keepdims=True))
        p = jnp.exp2(s - m_new).astype(BF16)
        vt = jnp.concatenate([vt_ref[0, 0, :, pl.ds(off, size)], jnp.ones((8, size), BF16)], axis=0)
        acc_sc[...] = acc_sc[...] * jnp.exp2(m - m_new) + jnp.dot(vt, p, preferred_element_type=F32)
        return m_new

    n_slots, look = s_sc.shape[0], ATT_LOOKAHEAD
    lat_off = lambda c: n_ctx + (c - 1) * tk
    acc_sc[...] = jnp.zeros(acc_sc.shape, F32)
    m = jnp.full((1, 2 * tq), NEG_INF, F32)
    scores(0, 0, n_ctx)
    for c in range(1, look + 1):
        scores(c % n_slots, min(lat_off(c), T - tk), tk)
    m = absorb(0, 0, n_ctx, m)

    def body(jj, m):
        c0 = 1 + jj * unroll
        for u in range(unroll):
            off = pl.multiple_of(lat_off(c0 + u), math.gcd(n_ctx, tk))
            nxt = pl.multiple_of(jnp.minimum(lat_off(c0 + u + look), T - tk), math.gcd(n_ctx, tk))
            scores((1 + u + look) % n_slots, nxt, tk)
            m = absorb((1 + u) % n_slots, off, tk, m)
        return m

    n_trips = (T - n_ctx) // (unroll * tk)
    lax.fori_loop(0, jnp.where(qi * tq < n_ctx, 0, n_trips), body, m)
    acc = acc_sc[...]
    o = acc[0:HEAD_W] / acc[HEAD_W:HEAD_W + 1]
    e0 = jnp.exp(jnp.sum(lq_ref[0:1, :] * lk_ref[0:1, :], axis=1, keepdims=True))
    e1 = jnp.exp(jnp.sum(lq_ref[1:2, :] * lk_ref[1:2, :], axis=1, keepdims=True))
    lam = e0 - e1 + lam_init
    y = o[:, :tq] - lam * o[:, tq:]
    y = y * lax.rsqrt(jnp.mean(y * y, axis=0, keepdims=True) + 1e-5) * g_ref[...] * (1.0 - lam_init)
    o_ref[0, 0] = y.T


def _diff_attention(q, k, v, lam_q, lam_k, subln, l, tq, tk, unroll, n_ctx):
    B, H, _, T, _ = q.shape
    dqk = lam_q.shape[-1]
    lam_init = 0.8 - 0.6 * math.exp(-0.3 * l)
    unroll = math.gcd(unroll, (T - n_ctx) // tk)
    assert unroll % ATT_SLOTS == 0 and ATT_LOOKAHEAD < ATT_SLOTS
    assert (T - n_ctx) % (unroll * tk) == 0 and n_ctx % tq == 0 and n_ctx % 128 == 0 and tk % 128 == 0
    kern = functools.partial(_flash_kernel, tq=tq, tk=tk, unroll=unroll, n_ctx=n_ctx, T=T,
                             lam_init=lam_init)
    return pl.pallas_call(
        kern,
        grid=(B, H, T // tq),
        in_specs=[pl.BlockSpec((None, 2, dqk), lambda b, h, i: (l, 0, 0)),
                  pl.BlockSpec((None, 2, dqk), lambda b, h, i: (l, 0, 0)),
                  pl.BlockSpec((None, None, HEAD_W, 1), lambda b, h, i: (l, h, 0, 0)),
                  pl.BlockSpec((1, 1, 2, tq, HEAD_W), lambda b, h, i: (b, h, 0, i, 0)),
                  pl.BlockSpec((1, 1, T, HEAD_W), lambda b, h, i: (b, h, 0, 0)),
                  pl.BlockSpec((1, 1, HEAD_W, T), lambda b, h, i: (b, h, 0, 0))],
        out_specs=pl.BlockSpec((1, 1, tq, HEAD_W), lambda b, h, i: (b, h, i, 0)),
        out_shape=jax.ShapeDtypeStruct((B, H, T, HEAD_W), F32),
        scratch_shapes=[pltpu.VMEM((ATT_SLOTS, max(tk, n_ctx), 2 * tq), F32),
                        pltpu.VMEM((HEAD_W + 8, 2 * tq), F32)],
        compiler_params=_cparams(("parallel", "parallel", "arbitrary")),
        name="diff_attention",
    )(lam_q, lam_k, subln, q, k, v)


def _rwprep_kernel(c_ref, p_ref, n_ref, mu_ref, kk_ref, ka_ref, rk_ref, w0_ref, a0_ref,
                   w2_ref, a2_ref, g2_ref, ones_ref, rvk_ref, dir_ref, aux_ref, *, tm, n_ctx, T):
    i = pl.program_id(1)
    c = c_ref[0]
    t0 = i * tm
    seg_start = jnp.logical_or(t0 == 0, t0 == n_ctx)
    seg_end = jnp.logical_or(t0 + tm == n_ctx, t0 + tm == T)
    pv = jnp.where(seg_start, 0.0, p_ref[0][7:8, :])
    nx = jnp.where(seg_end, 0.0, n_ref[0][0:1, :])
    row = lax.broadcasted_iota(jnp.int32, c.shape, 0)
    prev = jnp.where(row == 0, pv, pltpu.roll(c, 1, 0))
    nxt = jnp.where(row == tm - 1, nx, pltpu.roll(c, tm - 1, 0))
    xs = c + (0.5 * (prev + nxt) - c) * mu_ref[...]
    W = BRANCH_W
    r, k, v = xs[:, 0:W], xs[:, W:2 * W], xs[:, 2 * W:3 * W]
    lo, gl = xs[:, 3 * W:4 * W], xs[:, 4 * W:]
    ones_bd = ones_ref[...]
    kk = k * kk_ref[...]
    kk = kk * lax.rsqrt(jnp.maximum(_dotf(kk * kk, ones_bd), 1e-12))
    wpre = _dotf(jnp.tanh(lo), w2_ref[...])
    apre = _dotf(lo, a2_ref[...])
    gate = _dotf(_sigmoid(gl), g2_ref[...])
    rb = jnp.zeros_like(r)
    for d in range(2):
        z = w0_ref[d:d + 1, :] + wpre[:, d * W:(d + 1) * W]
        w = -_softplus(-z) - 0.5
        a = _sigmoid(a0_ref[d:d + 1, :] + apre[:, d * W:(d + 1) * W])
        kd = k * (1.0 + (a - 1.0) * ka_ref[...])
        dir_ref[0, d, :, 0:W] = -jnp.exp(w)
        dir_ref[0, d, :, W:2 * W] = kd
        dir_ref[0, d, :, 2 * W:3 * W] = kk * a
        rb = rb + r * kd * rk_ref[...]
    rvk_ref[0, :, 0:W] = r
    rvk_ref[0, :, W:2 * W] = v
    rvk_ref[0, :, 2 * W:3 * W] = kk
    aux_ref[0, :, 0:W] = _dotf(rb, ones_bd) * v
    aux_ref[0, :, W:2 * W] = gate


def _rwkv_prep(cols, prm, l, tm, n_ctx):
    B, T, NB = cols.shape
    W = BRANCH_W
    nblk8 = T // 8
    r8 = tm // 8
    vec = lambda n: pl.BlockSpec((None, 1, n), lambda b, i: (l, 0, 0))
    pair = lambda n: pl.BlockSpec((None, 2, n), lambda b, i: (l, 0, 0))
    mat = lambda m, n: _const_spec((None, m, n), lambda b, i: (l, 0, 0))
    return pl.pallas_call(
        functools.partial(_rwprep_kernel, tm=tm, n_ctx=n_ctx, T=T),
        grid=(B, T // tm),
        in_specs=[pl.BlockSpec((1, tm, NB), lambda b, i: (b, i, 0)),
                  pl.BlockSpec((1, 8, NB), lambda b, i: (b, jnp.maximum(i * r8 - 1, 0), 0)),
                  pl.BlockSpec((1, 8, NB), lambda b, i: (b, jnp.minimum((i + 1) * r8, nblk8 - 1), 0)),
                  vec(NB), vec(W), vec(W), vec(W), pair(W), pair(W),
                  mat(W, 2 * W), mat(W, 2 * W), mat(NB - 4 * W, W),
                  _const_spec((W, W), lambda b, i: (0, 0))],
        out_specs=[pl.BlockSpec((1, tm, 3 * W), lambda b, i: (b, i, 0)),
                   pl.BlockSpec((1, 2, tm, 3 * W), lambda b, i: (b, 0, i, 0)),
                   pl.BlockSpec((1, tm, 2 * W), lambda b, i: (b, i, 0))],
        out_shape=[jax.ShapeDtypeStruct((B, T, 3 * W), F32),
                   jax.ShapeDtypeStruct((B, 2, T, 3 * W), F32),
                   jax.ShapeDtypeStruct((B, T, 2 * W), F32)],
        compiler_params=_cparams(("parallel", "parallel")),
        name="rwkv_prep",
    )(cols, cols, cols, prm["mu"], prm["kk"], prm["ka"], prm["rk"], prm["w0"], prm["a0"],
      prm["w2"], prm["a2"], prm["g2"], prm["ones_bd"])


def _rw_chunks(chains):
    C, W = RW_CHUNK, BRANCH_W
    n = len(chains)
    each = lambda f, *cols: [f(*a) for a in zip(*cols)]
    r, v, kk, lw, kd, b, s0, sgn = (list(col) for col in zip(*chains))

    ti = lax.broadcasted_iota(jnp.int32, (C, C), 0)
    tj = lax.broadcasted_iota(jnp.int32, (C, C), 1)
    lane = lax.broadcasted_iota(jnp.int32, (1, W), 1)
    head_of_lane = lane >> 6
    stack = lambda x: jnp.concatenate(
        [jnp.where(head_of_lane == h, x, 0.0) for h in range(N_HEADS)], axis=0)
    tile = lambda x: jnp.concatenate([x] * N_HEADS, axis=0)
    ri = lax.broadcasted_iota(jnp.int32, (W, W), 0)
    ci = lax.broadcasted_iota(jnp.int32, (W, W), 1)
    same_head = (ri >> 6) == (ci >> 6)
    dt = (ci & (C - 1)) - (ri & (C - 1))
    masks = {sg: (jnp.where((tj - ti) * sg <= 0, 1.0, 0.0),
                  jnp.logical_and(same_head, dt * sg < 0),
                  jnp.logical_and(same_head, dt * sg <= 0)) for sg in set(sgn)}
    tri = [masks[sg][0] for sg in sgn]
    strict = [masks[sg][1] for sg in sgn]
    incl = [masks[sg][2] for sg in sgn]
    blk16 = (ri >> 4) == (ci >> 4)
    blk32 = (ri >> 5) == (ci >> 5)
    eye = jnp.where(ri == ci, 1.0, 0.0)

    log_g = each(_dotf, tri, lw)
    log_gc = [jnp.sum(x, axis=0, keepdims=True) for x in lw]
    kk_t = each(lambda kk_, lg, lw_: kk_ * jnp.exp(lg - lw_), kk, log_g, lw)
    r_t = each(lambda r_, lg: r_ * jnp.exp(lg), r, log_g)
    g_inv = [jnp.exp(-lg) for lg in log_g]
    g_tail = each(lambda lgc, lg: jnp.exp(lgc - lg), log_gc, log_g)

    left = each(lambda a, c: jnp.concatenate([stack(a), stack(c)], axis=0).astype(BF16), kk_t, r_t)
    right = each(lambda b_, kd_, gi: jnp.concatenate([stack(b_ * gi), stack(kd_ * gi)], axis=0).astype(BF16),
                 b, kd, g_inv)
    g = each(_dot_nt, left, right)
    m_sl = each(lambda st, g_: jnp.where(st, g_[:W, :W], 0.0), strict, g)
    n_sl = each(lambda st, g_: jnp.where(st, g_[:W, W:], 0.0), strict, g)
    pq_l = each(lambda ic, g_: jnp.concatenate([jnp.where(ic, g_[W:, :W], 0.0),
                                                jnp.where(ic, g_[W:, W:], 0.0)], axis=1).astype(BF16), incl, g)

    g2 = each(lambda a, c, s_: _sdot(jnp.concatenate([a, c], axis=0), s_), kk_t, r_t, s0)
    vs = [tile(x).astype(BF16) for x in v]
    rhs = each(lambda g2_, n_, vs_: -(tile(g2_[:C]) + _sdot(n_, vs_)), g2, n_sl, vs)

    d0 = [jnp.where(blk16, m_, 0.0).astype(BF16) for m_ in m_sl]
    a2 = each(_sdot, d0, d0)
    a4 = each(_sdot, a2, a2)
    a8 = each(_sdot, a4, a4)
    t = [eye - d_ for d_ in d0]
    for a_ in (a2, a4, a8):
        t = each(lambda t_, x: t_ + _sdot(t_, x), t, a_)
    for cm in (jnp.logical_and(blk32, jnp.logical_not(blk16)), jnp.logical_not(blk32)):
        u = each(lambda m_, t_: _sdot(jnp.where(cm, m_, 0.0), t_), m_sl, t)
        t = each(lambda t_, u_: t_ - _sdot(t_, u_), t, u)

    sa = each(_sdot, t, rhs)
    sav = each(lambda sa_, vs_: jnp.concatenate([sa_.astype(BF16), vs_], axis=0), sa, vs)
    ys = each(lambda g2_, pq, sav_: tile(g2_[C:]) + _sdot(pq, sav_), g2, pq_l, sav)
    ds = each(lambda b_, kd_, gt, sav_: _sdot_tn(jnp.concatenate([stack(b_ * gt), stack(kd_ * gt)], axis=0),
                                                 sav_), b, kd, g_tail, sav)
    out = []
    for i in range(n):
        y = jnp.zeros((C, W), F32)
        for h in range(N_HEADS):
            y = y + jnp.where(head_of_lane == h, ys[i][h * C:(h + 1) * C], 0.0)
        gc_col = jnp.sum(jnp.where(ri == ci, jnp.exp(log_gc[i]), 0.0), axis=1, keepdims=True)
        out.append((y, jnp.where(same_head, s0[i] * gc_col + ds[i], 0.0)))
    return out


def _rwscan_kernel(rvf_ref, rvb_ref, df_ref, db_ref, yf_ref, yb_ref, s_ref):
    W = BRANCH_W

    @pl.when(pl.program_id(0) == 0)
    def _():
        s_ref[...] = jnp.zeros(s_ref.shape, F32)

    ids = [(bi, d) for bi in range(rvf_ref.shape[0]) for d in range(2)]
    refs = ((rvf_ref, df_ref, yf_ref), (rvb_ref, db_ref, yb_ref))
    chains = []
    for bi, d in ids:
        rv, dr, _ = refs[d]
        chains.append((rv[bi, :, 0:W], rv[bi, :, W:2 * W], rv[bi, :, 2 * W:3 * W],
                       dr[bi, 0, :, 0:W], dr[bi, 0, :, W:2 * W], dr[bi, 0, :, 2 * W:3 * W],
                       s_ref[2 * bi + d], 1 - 2 * d))
    for (bi, d), (y, s_new) in zip(ids, _rw_chunks(chains)):
        refs[d][2][bi] = y
        s_ref[2 * bi + d] = s_new


def _chunk_order(s, d, n_ctx_chunks, n_chunks):
    bwd = jnp.where(s < n_ctx_chunks, n_ctx_chunks - 1 - s, n_chunks - 1 - (s - n_ctx_chunks))
    return jnp.where(d == 0, s, bwd)


def _rwkv_scan(rvk, dirs, n_ctx):
    B, T, _ = rvk.shape
    C, W = RW_CHUNK, BRANCH_W
    nc, ncx = T // C, n_ctx // C
    ch = lambda s, d: _chunk_order(s, d, ncx, nc)
    rv_spec = lambda d: pl.BlockSpec((B, C, 3 * W), lambda s: (0, ch(s, d), 0))
    dir_spec = lambda d: pl.BlockSpec((B, 1, C, 3 * W), lambda s: (0, d, ch(s, d), 0))
    y_spec = lambda d: pl.BlockSpec((B, C, W), lambda s: (0, ch(s, d), 0))
    return pl.pallas_call(
        _rwscan_kernel,
        grid=(nc,),
        in_specs=[rv_spec(0), rv_spec(1), dir_spec(0), dir_spec(1)],
        out_specs=[y_spec(0), y_spec(1)],
        out_shape=[jax.ShapeDtypeStruct((B, T, W), F32)] * 2,
        scratch_shapes=[pltpu.VMEM((2 * B, W, W), F32)],
        compiler_params=_cparams(("arbitrary",)),
        name="rwkv_scan",
    )(rvk, rvk, dirs, dirs)


def _ret_chunk(c, cos, sm, sp, dec, s0, d, half, scale):
    C, W = RET_CHUNK, BRANCH_W
    QW = W // 2
    q = _rope(c[:, 0:QW], cos, sm, sp, half)
    k = _rope(c[:, QW:2 * QW], cos, sm, sp, half) * scale
    v = c[:, 2 * QW:2 * QW + W]
    log_g = -_softplus(-dec)
    lane_q = lax.broadcasted_iota(jnp.int32, (1, QW), 1) >> 5
    lane_v = lax.broadcasted_iota(jnp.int32, (1, W), 1) >> 6
    lg_q = jnp.zeros((1, QW), F32)
    lg_v = jnp.zeros((1, W), F32)
    for h in range(N_HEADS):
        lg_q = jnp.where(lane_q == h, log_g[:, h:h + 1], lg_q)
        lg_v = jnp.where(lane_v == h, log_g[:, h:h + 1], lg_v)
    idx = lax.broadcasted_iota(jnp.int32, (C, 1), 0).astype(F32)
    q_dec = jnp.exp((idx + 1.0 if d == 0 else C - idx) * lg_q)
    k_dec = jnp.exp((C - 1.0 - idx if d == 0 else idx) * lg_q)
    ti = lax.broadcasted_iota(jnp.int32, (C, C), 0)
    tj = lax.broadcasted_iota(jnp.int32, (C, C), 1)
    rel = ((ti - tj) * (1 - 2 * d)).astype(F32)

    o = _dotb(q * q_dec, s0)
    kb = k.astype(BF16)
    vb = v.astype(BF16)
    for h in range(N_HEADS):
        dm = jnp.where(rel >= 0, jnp.exp(jnp.maximum(rel, 0.0) * log_g[:, h:h + 1]), 0.0)
        qh = jnp.where(lane_q == h, q, 0.0).astype(BF16)
        sc = _dot_nt(qh, kb) * dm
        o = o + jnp.where(lane_v == h, jnp.dot(sc.astype(BF16), vb, preferred_element_type=F32), 0.0)

    ri = lax.broadcasted_iota(jnp.int32, (QW, W), 0) >> 5
    ci = lax.broadcasted_iota(jnp.int32, (QW, W), 1) >> 6
    upd = _dot_tn((k * k_dec).astype(BF16), vb)
    return o, jnp.where(ri == ci, s0 * jnp.exp(C * lg_v) + upd, 0.0)


def _ret_kernel(cf_ref, cb_ref, cosf_ref, smf_ref, spf_ref, cosb_ref, smb_ref, spb_ref, dec_ref,
                of_ref, ob_ref, s_ref, *, half, scale):
    @pl.when(pl.program_id(0) == 0)
    def _():
        s_ref[...] = jnp.zeros(s_ref.shape, F32)

    dirs = ((cf_ref, (cosf_ref, smf_ref, spf_ref), of_ref), (cb_ref, (cosb_ref, smb_ref, spb_ref), ob_ref))
    for bi in range(cf_ref.shape[0]):
        for d, (c_ref, tabs, o_ref) in enumerate(dirs):
            o, s_new = _ret_chunk(c_ref[bi], tabs[0][...], tabs[1][...], tabs[2][...], dec_ref[d],
                                  s_ref[2 * bi + d], d, half, scale)
            o_ref[bi] = o
            s_ref[2 * bi + d] = s_new


def _retention(cols, tabs, decay, l, n_ctx):
    B, T, NCOL = cols.shape
    C, W = RET_CHUNK, BRANCH_W
    cos, sm, sp, half = tabs
    nc, ncx = T // C, n_ctx // C
    ch = lambda s, d: _chunk_order(s, d, ncx, nc)
    c_spec = lambda d: pl.BlockSpec((B, C, NCOL), lambda s: (0, ch(s, d), 0))
    tab_spec = lambda d: pl.BlockSpec((C, W // 2), lambda s: (ch(s, d), 0))
    o_spec = lambda d: pl.BlockSpec((B, C, W), lambda s: (0, ch(s, d), 0))
    return pl.pallas_call(
        functools.partial(_ret_kernel, half=half, scale=(HEAD_W // 2) ** -0.5),
        grid=(nc,),
        in_specs=[c_spec(0), c_spec(1), tab_spec(0), tab_spec(0), tab_spec(0),
                  tab_spec(1), tab_spec(1), tab_spec(1),
                  pl.BlockSpec((None, 2, 1, N_HEADS), lambda s: (l, 0, 0, 0))],
        out_specs=[o_spec(0), o_spec(1)],
        out_shape=[jax.ShapeDtypeStruct((B, T, W), F32)] * 2,
        scratch_shapes=[pltpu.VMEM((2 * B, W // 2, W), F32)],
        compiler_params=_cparams(("arbitrary",)),
        name="retention",
    )(cols, cols, cos, sm, sp, cos, sm, sp, decay)


def _dprep_kernel(c_ref, cos_ref, sm_ref, sp_ref, q_ref, k_ref, v_ref, *, half, scale):
    W = BRANCH_W
    c = c_ref[0]
    cos, sm, sp = cos_ref[...], sm_ref[...], sp_ref[...]
    q = (_rope(c[:, 0:W], cos, sm, sp, half) * scale).astype(BF16)
    kw = W // 2
    k = _rope(c[:, W:W + kw], cos[:, :kw], sm[:, :kw], sp[:, :kw], half).astype(BF16)
    vt = c[:, W + kw:W + 2 * kw].T.astype(BF16)
    for kvh in range(2):
        for g in range(2):
            h = kvh * 2 + g
            q_ref[0, kvh, g] = q[:, h * HEAD_W:(h + 1) * HEAD_W]
        k_ref[0, kvh] = k[:, kvh * HEAD_W:(kvh + 1) * HEAD_W]
        v_ref[0, kvh] = vt[kvh * HEAD_W:(kvh + 1) * HEAD_W, :]


def _win_prep(cols, tabs, tm):
    B, T, NCOL = cols.shape
    cos, sm, sp, half = tabs
    tab_spec = pl.BlockSpec((tm, BRANCH_W), lambda b, i: (i, 0))
    return pl.pallas_call(
        functools.partial(_dprep_kernel, half=half, scale=HEAD_W ** -0.5),
        grid=(B, T // tm),
        in_specs=[pl.BlockSpec((1, tm, NCOL), lambda b, i: (b, i, 0)), tab_spec, tab_spec, tab_spec],
        out_specs=[pl.BlockSpec((1, 2, 2, tm, HEAD_W), lambda b, i: (b, 0, 0, i, 0)),
                   pl.BlockSpec((1, 2, tm, HEAD_W), lambda b, i: (b, 0, i, 0)),
                   pl.BlockSpec((1, 2, HEAD_W, tm), lambda b, i: (b, 0, 0, i))],
        out_shape=[jax.ShapeDtypeStruct((B, 2, 2, T, HEAD_W), BF16),
                   jax.ShapeDtypeStruct((B, 2, T, HEAD_W), BF16),
                   jax.ShapeDtypeStruct((B, 2, HEAD_W, T), BF16)],
        compiler_params=_cparams(("parallel", "parallel")),
        name="win_prep",
    )(cols, cos, sm, sp)


def _winattn_kernel(sink_ref, q_ref, k_ref, v_ref, o_ref, *, tq, n_ctx, T):
    i = pl.program_id(2)
    band = tq + 2 * WINDOW
    q = q_ref[0, 0].reshape(2 * tq, HEAD_W)
    ws = pl.multiple_of(jnp.clip(i * tq - WINDOW, n_ctx, T - band), WINDOW)
    s_ctx = _dot_nt(k_ref[0, 0, 0:n_ctx, :], q)
    s_band = _dot_nt(k_ref[0, 0, pl.ds(ws, band), :], q)
    kpos = ws + lax.broadcasted_iota(jnp.int32, (band, tq), 0)
    qpos = i * tq + lax.broadcasted_iota(jnp.int32, (band, tq), 1)
    valid = jnp.logical_and(jnp.abs(qpos - kpos) <= WINDOW, qpos >= n_ctx)
    s_band = jnp.where(jnp.concatenate([valid, valid], axis=1), s_band, NEG_INF)
    sink = sink_ref[...]
    m = jnp.maximum(jnp.maximum(jnp.max(s_ctx, axis=0, keepdims=True),
                                jnp.max(s_band, axis=0, keepdims=True)), sink)
    p_ctx = jnp.exp(s_ctx - m).astype(BF16)
    p_band = jnp.exp(s_band - m).astype(BF16)
    vx = jnp.concatenate([v_ref[0, 0, :, 0:n_ctx], jnp.ones((8, n_ctx), BF16)], axis=0)
    vb = jnp.concatenate([v_ref[0, 0, :, pl.ds(ws, band)], jnp.ones((8, band), BF16)], axis=0)
    acc = jnp.dot(jnp.concatenate([vx, vb], axis=1), jnp.concatenate([p_ctx, p_band], axis=0),
                  preferred_element_type=F32)
    o = acc[0:HEAD_W] / (acc[HEAD_W:HEAD_W + 1] + jnp.exp(sink - m))
    for g in range(2):
        o_ref[0, 0, g] = o[:, g * tq:(g + 1) * tq].T


def _window_attention(q, k, v, sink_col, l, tq, n_ctx):
    B, KVH, G, T, _ = q.shape
    return pl.pallas_call(
        functools.partial(_winattn_kernel, tq=tq, n_ctx=n_ctx, T=T),
        grid=(B, KVH, T // tq),
        in_specs=[pl.BlockSpec((None, None, 1, G * tq), lambda b, h, i: (l, h, 0, 0)),
                  pl.BlockSpec((1, 1, G, tq, HEAD_W), lambda b, h, i: (b, h, 0, i, 0)),
                  pl.BlockSpec((1, 1, T, HEAD_W), lambda b, h, i: (b, h, 0, 0)),
                  pl.BlockSpec((1, 1, HEAD_W, T), lambda b, h, i: (b, h, 0, 0))],
        out_specs=pl.BlockSpec((1, 1, G, tq, HEAD_W), lambda b, h, i: (b, h, 0, i, 0)),
        out_shape=jax.ShapeDtypeStruct((B, KVH, G, T, HEAD_W), F32),
        compiler_params=_cparams(("parallel", "parallel", "arbitrary")),
        name="window_attention",
    )(sink_col, q, k, v)


def _group_norm(x, ones_bd, eps, center):
    inv = 1.0 / HEAD_W
    if center:
        x = x - _dotf(x, ones_bd) * inv
    return x * lax.rsqrt(_dotf(x * x, ones_bd) * inv + eps)


def _merge_kernel(x_ref, gates_ref, ya_ref, ybf_ref, ybb_ref, aux_ref, ocf_ref, ocb_ref, gc_ref, yd_ref,
                  lnx_g_ref, lnx_b_ref, gn_ref, wb_ref, wo_ref, pn_ref, mod_ref, ones_ref, o_ref):
    W = BRANCH_W
    D = x_ref.shape[-1]
    ones_bd = ones_ref[...]
    yb = _group_norm(ybf_ref[0] + ybb_ref[0], ones_bd, RW_GN_EPS, True)
    yb = yb * lnx_g_ref[...] + lnx_b_ref[...]
    yb = (yb + aux_ref[0, :, 0:W]) * aux_ref[0, :, W:2 * W]
    yc = _group_norm(ocf_ref[0] + ocb_ref[0], ones_bd, 1e-5, True) * gn_ref[...]
    gc = gc_ref[0]
    yc = yc * (gc * _sigmoid(gc))

    def lift_heads(y_ref4, n):
        acc = None
        for h in range(N_HEADS):
            t = jnp.dot(y_ref4(h).astype(BF16), wb_ref[n, h * HEAD_W:(h + 1) * HEAD_W, :],
                        preferred_element_type=F32)
            acc = t if acc is None else acc + t
        return acc

    lifted = [lift_heads(lambda h: ya_ref[0, h], 0),
              jnp.dot(yb.astype(BF16), wb_ref[1], preferred_element_type=F32),
              jnp.dot(yc.astype(BF16), wb_ref[2], preferred_element_type=F32),
              lift_heads(lambda h: yd_ref[0, h // 2, h % 2], 3)]
    m = None
    for n in range(4):
        t = _sigmoid(gates_ref[0, :, n * D:(n + 1) * D].astype(F32)) * lifted[n]
        m = t if m is None else m + t
    out = jnp.dot(m.astype(BF16), wo_ref[...], preferred_element_type=F32)
    o_ref[0] = x_ref[0] + mod_ref[2:3, :] * (_rms(out, 1e-6) * pn_ref[...])


def _merge(xs, gates, ya, yb, aux, oc, cols_c, yd, prm, mods, l, tm, nct):
    B, T, D = xs.shape
    W = BRANCH_W
    vec = lambda n: pl.BlockSpec((None, 1, n), lambda b, i: (l, 0, 0))
    return pl.pallas_call(
        _merge_kernel,
        grid=(B, T // tm),
        in_specs=[pl.BlockSpec((1, tm, D), lambda b, i: (b, i, 0)),
                  pl.BlockSpec((1, tm, 4 * D), lambda b, i: (b, i, 0)),
                  pl.BlockSpec((1, N_HEADS, tm, HEAD_W), lambda b, i: (b, 0, i, 0)),
                  pl.BlockSpec((1, tm, W), lambda b, i: (b, i, 0)),
                  pl.BlockSpec((1, tm, W), lambda b, i: (b, i, 0)),
                  pl.BlockSpec((1, tm, 2 * W), lambda b, i: (b, i, 0)),
                  pl.BlockSpec((1, tm, W), lambda b, i: (b, i, 0)),
                  pl.BlockSpec((1, tm, W), lambda b, i: (b, i, 0)),
                  pl.BlockSpec((1, tm, W), lambda b, i: (b, i, 2)),
                  pl.BlockSpec((1, 2, 2, tm, HEAD_W), lambda b, i: (b, 0, 0, i, 0)),
                  vec(W), vec(W), vec(W),
                  _const_spec((None, 4, W, D), lambda b, i: (l, 0, 0, 0)),
                  _const_spec((None, D, D), lambda b, i: (l, 0, 0)),
                  vec(D),
                  pl.BlockSpec((None, None, None, 8, D),
                               lambda b, i: (l, b, jnp.minimum(i // nct, 1), 0, 0)),
                  _const_spec((W, W), lambda b, i: (0, 0))],
        out_specs=pl.BlockSpec((1, tm, D), lambda b, i: (b, i, 0)),
        out_shape=jax.ShapeDtypeStruct((B, T, D), F32),
        compiler_params=_cparams(("parallel", "parallel")),
        name="merge",
    )(xs, gates, ya, yb[0], yb[1], aux, oc[0], oc[1], cols_c, yd, prm["lnx_g"], prm["lnx_b"], prm["ret_gn"],
      prm["w_branch"], prm["w_out"], prm["norm_post_mix"], mods, prm["ones_bd"])


def _mlp_kernel(x_ref, g_ref, pn_ref, mod_ref, wu_ref, wd_ref, o_ref):
    x = x_ref[0]
    h = _rms(x, 1e-6) * g_ref[...]
    h = h * (1.0 + mod_ref[4:5, :]) + mod_ref[3:4, :]
    u = jnp.dot(h.astype(BF16), wu_ref[...], preferred_element_type=F32)
    u = jnp.square(jnp.maximum(u, 0.0))
    y = jnp.dot(u.astype(BF16), wd_ref[...], preferred_element_type=F32)
    o_ref[0] = x + mod_ref[5:6, :] * (_rms(y, 1e-6) * pn_ref[...])


def _mlp(xs, prm, mods, l, tm, nct):
    B, T, D = xs.shape
    F = prm["w_up"].shape[-1]
    vec = lambda n: pl.BlockSpec((None, 1, n), lambda b, i: (l, 0, 0))
    return pl.pallas_call(
        _mlp_kernel,
        grid=(B, T // tm),
        in_specs=[pl.BlockSpec((1, tm, D), lambda b, i: (b, i, 0)),
                  vec(D), vec(D),
                  pl.BlockSpec((None, None, None, 8, D),
                               lambda b, i: (l, b, jnp.minimum(i // nct, 1), 0, 0)),
                  _const_spec((None, D, F), lambda b, i: (l, 0, 0)),
                  _const_spec((None, F, D), lambda b, i: (l, 0, 0))],
        out_specs=pl.BlockSpec((1, tm, D), lambda b, i: (b, i, 0)),
        out_shape=jax.ShapeDtypeStruct((B, T, D), F32),
        compiler_params=_cparams(("parallel", "parallel")),
        name="mlp",
    )(xs, prm["norm_pre_mlp"], prm["norm_post_mlp"], mods, prm["w_up"], prm["w_down"])


def kernel(x, c, ctx, c_ctx, ada_w, ada_b, norm_pre_mix, norm_post_mix, norm_pre_mlp, norm_post_mlp,
           w_in, diff_lam_q, diff_lam_k, diff_subln, rwkv_mu, rwkv_w0, rwkv_w2, rwkv_a0, rwkv_a2,
           rwkv_g2, rwkv_kk, rwkv_ka, rwkv_rk, rwkv_lnx_g, rwkv_lnx_b, ret_decay, ret_gn, win_sink,
           w_branch, w_out, w_up, w_down):
    B, S, D = x.shape
    n_ctx = ctx.shape[1]
    T = n_ctx + S
    L = ada_w.shape[0]
    W = BRANCH_W
    TM = 256
    nct = n_ctx // TM
    assert n_ctx % TM == 0 and S % TM == 0 and S % GRID_W == 0 and B + 1 <= 8

    cc = jnp.zeros((8, D), F32).at[:B].set(c).at[B].set(c_ctx)
    mod = _ada_mod(cc, ada_w, ada_b).reshape(L, 8, 6, D)
    mod_ctx = jnp.broadcast_to(mod[:, B][:, None], (L, B, 6, D))
    mods = jnp.stack([mod_ctx, mod[:, :B]], axis=2)
    mods = jnp.pad(mods, ((0, 0), (0, 0), (0, 0), (0, 2), (0, 0)))

    lat = jnp.arange(S, dtype=jnp.int32)
    row = (lat // GRID_W).astype(F32)
    col = (lat % GRID_W).astype(F32)
    tabs_a = _rope_tables([row, col], W, HEAD_W // 2, n_ctx)
    tabs_d = _rope_tables([row, col], W, HEAD_W, n_ctx)
    tabs_c = _rope_tables([lat.astype(F32)], W // 2, HEAD_W // 2, n_ctx)

    sizes = (3 * W, rwkv_mu.shape[-1], 3 * W, 2 * W, 4 * D)
    offs = np.concatenate([[0], np.cumsum(sizes)])
    w_groups = [w_in[:, :, int(offs[n]):int(offs[n + 1])].astype(BF16) for n in range(5)]

    lw_w, la_w = rwkv_w2.shape[2], rwkv_a2.shape[2]
    zw = jnp.zeros((L, W, 2 * W), F32)
    w2cat = (zw.at[:, 0:lw_w, 0:W].set(rwkv_w2[:, 0])
             .at[:, lw_w:2 * lw_w, W:2 * W].set(rwkv_w2[:, 1]))
    a2cat = (zw.at[:, 2 * lw_w:2 * lw_w + la_w, 0:W].set(rwkv_a2[:, 0])
             .at[:, 2 * lw_w + la_w:2 * lw_w + 2 * la_w, W:2 * W].set(rwkv_a2[:, 1]))
    blk = np.arange(W) // HEAD_W
    ones_bd = jnp.asarray((blk[:, None] == blk[None, :]).astype(np.float32))
    v3 = lambda a: a.reshape(L, 1, -1)
    prm = dict(mu=v3(rwkv_mu), kk=v3(rwkv_kk), ka=v3(rwkv_ka), rk=v3(rwkv_rk), w0=rwkv_w0, a0=rwkv_a0,
               w2=w2cat, a2=a2cat, g2=rwkv_g2, ones_bd=ones_bd,
               lnx_g=v3(rwkv_lnx_g), lnx_b=v3(rwkv_lnx_b), ret_gn=v3(ret_gn),
               w_branch=w_branch.astype(BF16), w_out=w_out.astype(BF16),
               norm_post_mix=v3(norm_post_mix), norm_pre_mlp=v3(norm_pre_mlp),
               norm_post_mlp=v3(norm_post_mlp), w_up=w_up.astype(BF16), w_down=w_down.astype(BF16))
    subln = diff_subln.reshape(L, N_HEADS, HEAD_W, 1)
    decay = ret_decay.reshape(L, 2, 1, N_HEADS)
    TQ_D = 256
    sink_col = jnp.broadcast_to(win_sink.reshape(L, 2, 1, 2, 1), (L, 2, 1, 2, TQ_D)).reshape(L, 2, 1, 2 * TQ_D)
    gain_pre = v3(norm_pre_mix)

    xs = jnp.concatenate([ctx, x], axis=1)
    for l in range(L):
        cols_a, cols_b, cols_c, cols_d, gates = _input_proj(
            xs, gain_pre, mods, w_groups, (F32, F32, F32, F32, BF16), l, TM, nct)
        qa, ka, va = _diff_prep(cols_a, tabs_a, TM)
        ya = _diff_attention(qa, ka, va, diff_lam_q, diff_lam_k, subln, l, ATT_TQ, ATT_TK, ATT_UNROLL, n_ctx)
        rvk, dirs, aux = _rwkv_prep(cols_b, prm, l, TM, n_ctx)
        yb = _rwkv_scan(rvk, dirs, n_ctx)
        oc = _retention(cols_c, tabs_c, decay, l, n_ctx)
        qd, kd, vd = _win_prep(cols_d, tabs_d, TM)
        yd = _window_attention(qd, kd, vd, sink_col, l, TQ_D, n_ctx)
        xs = _merge(xs, gates, ya, yb, aux, oc, cols_c, yd, prm, mods, l, TM, nct)
        xs = _mlp(xs, prm, mods, l, TM, nct)
    return xs[:, n_ctx:]
```

```python
import functools
import math

import numpy as np
import jax
import jax.numpy as jnp
from jax import lax
from jax.experimental import pallas as pl
from jax.experimental.pallas import tpu as pltpu

F32 = jnp.float32
BF16 = jnp.bfloat16
HI = lax.Precision.HIGHEST

GRID_W = 64
ROPE_BASE = 10000.0
NEG_INF = -1e30
WINDOW = 128
N_HEADS = 4
HEAD_W = 64
BRANCH_W = N_HEADS * HEAD_W
RW_CHUNK = 64
RET_CHUNK = 128
RW_GN_EPS = 64e-5
ATT_TQ, ATT_TK, ATT_UNROLL = 256, 256, 64
ATT_LOOKAHEAD, ATT_SLOTS = 2, 4
DENSE_TILES = (640, 512, 384, 256, 128)
MLP_SPLIT = 2
VMEM_LIMIT = 56 * 1024 * 1024


def _cparams(sem, vmem=None):
    return pltpu.CompilerParams(dimension_semantics=sem, vmem_limit_bytes=vmem or VMEM_LIMIT)


def _dotf(a, b):
    return jnp.dot(a, b, precision=HI, preferred_element_type=F32)


def _dotb(a, b):
    return jnp.dot(a.astype(BF16), b.astype(BF16), preferred_element_type=F32)


def _dot_nt(a, b, precision=None):
    return lax.dot_general(a, b, (((1,), (1,)), ((), ())), precision=precision,
                           preferred_element_type=F32)


def _dot_tn(a, b, precision=None):
    return lax.dot_general(a, b, (((0,), (0,)), ((), ())), precision=precision,
                           preferred_element_type=F32)


def _sdot(a, b):
    return jnp.dot(a.astype(BF16), b.astype(BF16), preferred_element_type=F32)


def _sdot_nt(a, b):
    return _dot_nt(a.astype(BF16), b.astype(BF16))


def _sdot_tn(a, b):
    return _dot_tn(a.astype(BF16), b.astype(BF16))


def _sigmoid(x):
    return 1.0 / (1.0 + jnp.exp(-x))


def _softplus(x):
    return jnp.maximum(x, 0.0) + jnp.log1p(jnp.exp(-jnp.abs(x)))


def _rms(x, eps):
    return x * lax.rsqrt(jnp.mean(x * x, axis=-1, keepdims=True) + eps)


def _const_spec(shape, index):
    return pl.BlockSpec(shape, index, pipeline_mode=pl.Buffered(1))


def _ada_kernel(c_ref, w_ref, b_ref, o_ref):
    c = c_ref[...]
    o_ref[0] = _dotf(c * _sigmoid(c), w_ref[0]) + b_ref[0]


def _ada_mod(cc, ada_w, ada_b):
    L, D, N = ada_w.shape
    tn = N // 4
    return pl.pallas_call(
        _ada_kernel,
        grid=(L, N // tn),
        in_specs=[pl.BlockSpec((8, D), lambda l, n: (0, 0)),
                  pl.BlockSpec((1, D, tn), lambda l, n: (l, 0, n)),
                  pl.BlockSpec((1, 1, tn), lambda l, n: (l, 0, n))],
        out_specs=pl.BlockSpec((1, 8, tn), lambda l, n: (l, 0, n)),
        out_shape=jax.ShapeDtypeStruct((L, 8, N), F32),
        compiler_params=_cparams(("arbitrary", "arbitrary")),
        name="ada_mod",
    )(cc, ada_w, ada_b.reshape(L, 1, N))


def _mod_row(mod_ref, j, tm, n_ctx):
    tok = pl.program_id(1) * tm + lax.broadcasted_iota(jnp.int32, (tm, 1), 0)
    return jnp.where(tok < n_ctx, mod_ref[0, j:j + 1, :], mod_ref[1, j:j + 1, :])


def _mod_spec(l, D):
    return pl.BlockSpec((None, None, 2, 8, D), lambda b, i: (l, b, 0, 0, 0))


def _proj_kernel(x_ref, g_ref, mod_ref, wa_ref, wb_ref, wc_ref, wd_ref, wg_ref,
                 cosa_ref, sma_ref, spa_ref, cosd_ref, smd_ref, spd_ref,
                 qa_ref, ka_ref, va_ref, cb_ref, cc_ref, qd_ref, kd_ref, vd_ref, gates_ref, *,
                 tm, n_ctx, half_a, half_d):
    h = _rms(x_ref[0], 1e-6) * g_ref[...]
    h = h * (1.0 + _mod_row(mod_ref, 1, tm, n_ctx)) + _mod_row(mod_ref, 0, tm, n_ctx)
    hb = h.astype(BF16)
    proj = lambda w_ref: jnp.dot(hb, w_ref[...], preferred_element_type=F32)
    _store_diff_qkv(proj(wa_ref), cosa_ref[...], sma_ref[...], spa_ref[...], half_a, qa_ref, ka_ref, va_ref)
    cb_ref[0] = proj(wb_ref)
    cc_ref[0] = proj(wc_ref)
    _store_win_qkv(proj(wd_ref), cosd_ref[...], smd_ref[...], spd_ref[...], half_d, qd_ref, kd_ref, vd_ref)
    gates_ref[0] = proj(wg_ref).astype(gates_ref.dtype)


def _input_proj(xs, gain, mods, weights, tabs_a, tabs_d, l, tm, n_ctx):
    B, T, D = xs.shape
    W = BRANCH_W
    tok = lambda n: pl.BlockSpec((1, tm, n), lambda b, i: (b, i, 0))
    tab = pl.BlockSpec((tm, W), lambda b, i: (i, 0))
    in_specs = [tok(D), pl.BlockSpec((None, 1, D), lambda b, i: (l, 0, 0)), _mod_spec(l, D)]
    in_specs += [_const_spec((None, D, w.shape[-1]), lambda b, i: (l, 0, 0)) for w in weights]
    in_specs += [tab] * 6
    heads = lambda *lead: pl.BlockSpec((1,) + lead + (tm, HEAD_W), lambda b, i: (b,) + (0,) * len(lead) + (i, 0))
    heads_t = lambda n: pl.BlockSpec((1, n, HEAD_W, tm), lambda b, i: (b, 0, 0, i))
    out_specs = [heads(N_HEADS, 2), heads(N_HEADS), heads_t(N_HEADS), tok(weights[1].shape[-1]),
                 tok(weights[2].shape[-1]), heads(2, 2), heads(2), heads_t(2), tok(weights[4].shape[-1])]
    sds = jax.ShapeDtypeStruct
    out_shape = [sds((B, N_HEADS, 2, T, HEAD_W), BF16), sds((B, N_HEADS, T, HEAD_W), BF16),
                 sds((B, N_HEADS, HEAD_W, T), BF16), sds((B, T, weights[1].shape[-1]), F32),
                 sds((B, T, weights[2].shape[-1]), F32), sds((B, 2, 2, T, HEAD_W), BF16),
                 sds((B, 2, T, HEAD_W), BF16), sds((B, 2, HEAD_W, T), BF16),
                 sds((B, T, weights[4].shape[-1]), BF16)]
    return pl.pallas_call(
        functools.partial(_proj_kernel, tm=tm, n_ctx=n_ctx, half_a=tabs_a[3], half_d=tabs_d[3]),
        grid=(B, T // tm),
        in_specs=in_specs, out_specs=out_specs, out_shape=out_shape,
        compiler_params=_cparams(("parallel", "parallel")),
        name="input_proj",
    )(xs, gain, mods, *weights, *tabs_a[:3], *tabs_d[:3])


def _rope_tables(pos_sets, lanes, group, n_ctx):
    n_sets = len(pos_sets)
    sub = group // n_sets
    half = sub // 2
    angs = []
    for pos in pos_sets:
        inv_freq = ROPE_BASE ** (-jnp.arange(half, dtype=F32) / half)
        ang = pos[:, None] * inv_freq[None, :]
        angs.append(jnp.concatenate([ang, ang], axis=-1))
    ang = jnp.concatenate(angs, axis=-1)
    ang = jnp.tile(ang, (1, lanes // group))
    ang = jnp.concatenate([jnp.zeros((n_ctx, lanes), F32), ang], axis=0)
    first = (np.arange(lanes) % sub) < half
    cos, sin = jnp.cos(ang), jnp.sin(ang)
    sin_minus = jnp.where(first[None, :], -sin, 0.0)
    sin_plus = jnp.where(first[None, :], 0.0, sin)
    return cos, sin_minus, sin_plus, half


def _rope(x, cos, sin_minus, sin_plus, half):
    n = x.shape[-1]
    return x * cos + pltpu.roll(x, n - half, 1) * sin_minus + pltpu.roll(x, half, 1) * sin_plus


def _store_diff_qkv(c, cos, sm, sp, half, q_ref, k_ref, v_ref):
    scale = (HEAD_W // 2) ** -0.5 * math.log2(math.e)
    q = _rope(c[:, 0:BRANCH_W], cos, sm, sp, half) * scale
    k = _rope(c[:, BRANCH_W:2 * BRANCH_W], cos, sm, sp, half)
    v = c[:, 2 * BRANCH_W:3 * BRANCH_W]
    lane = lax.broadcasted_iota(jnp.int32, q.shape, 1)
    comp = (lane >> 5) & 1
    q0 = jnp.where(comp == 0, q, 0.0).astype(BF16)
    q1 = jnp.where(comp == 1, q, 0.0).astype(BF16)
    kb, vt = k.astype(BF16), v.T.astype(BF16)
    for h in range(N_HEADS):
        sl = slice(h * HEAD_W, (h + 1) * HEAD_W)
        q_ref[0, h, 0] = q0[:, sl]
        q_ref[0, h, 1] = q1[:, sl]
        k_ref[0, h] = kb[:, sl]
        v_ref[0, h] = vt[sl, :]


def _flash_kernel(lq_ref, lk_ref, g_ref, q_ref, k_ref, vt_ref, o_ref, s_sc, acc_sc, *,
                  tq, tk, unroll, n_ctx, T, lam_init):
    qi = pl.program_id(2)
    q = q_ref[0, 0].reshape(2 * tq, HEAD_W)
    def scores(slot, off, size):
        s_sc[slot, 0:size] = _dot_nt(k_ref[0, 0, pl.ds(off, size), :], q)

    def absorb(slot, off, size, m):
        s = s_sc[slot, 0:size]
        m_new = jnp.maximum(m, jnp.max(s, axis=0, keepdims=True))
        p = jnp.exp2(s - m_new).astype(BF16)
        vt = jnp.concatenate([vt_ref[0, 0, :, pl.ds(off, size)], jnp.ones((8, size), BF16)], axis=0)
        acc_sc[...] = acc_sc[...] * jnp.exp2(m - m_new) + jnp.dot(vt, p, preferred_element_type=F32)
        return m_new

    n_slots, look = s_sc.shape[0], ATT_LOOKAHEAD
    lat_off = lambda c: n_ctx + (c - 1) * tk
    acc_sc[...] = jnp.zeros(acc_sc.shape, F32)
    m = jnp.full((1, 2 * tq), NEG_INF, F32)
    scores(0, 0, n_ctx)
    for c in range(1, look + 1):
        scores(c % n_slots, min(lat_off(c), T - tk), tk)
    m = absorb(0, 0, n_ctx, m)

    def body(jj, m):
        c0 = 1 + jj * unroll
        for u in range(unroll):
            off = pl.multiple_of(lat_off(c0 + u), math.gcd(n_ctx, tk))
            nxt = pl.multiple_of(jnp.minimum(lat_off(c0 + u + look), T - tk), math.gcd(n_ctx, tk))
            scores((1 + u + look) % n_slots, nxt, tk)
            m = absorb((1 + u) % n_slots, off, tk, m)
        return m

    n_trips = (T - n_ctx) // (unroll * tk)
    lax.fori_loop(0, jnp.where(qi * tq < n_ctx, 0, n_trips), body, m)
    acc = acc_sc[...]
    o = acc[0:HEAD_W] / acc[HEAD_W:HEAD_W + 1]
    e0 = jnp.exp(jnp.sum(lq_ref[0:1, :] * lk_ref[0:1, :], axis=1, keepdims=True))
    e1 = jnp.exp(jnp.sum(lq_ref[1:2, :] * lk_ref[1:2, :], axis=1, keepdims=True))
    lam = e0 - e1 + lam_init
    y = o[:, :tq] - lam * o[:, tq:]
    o_ref[0] = y * lax.rsqrt(jnp.mean(y * y, axis=0, keepdims=True) + 1e-5) * g_ref[...] * (1.0 - lam_init)


def _diff_attention(q, k, v, lam_q, lam_k, subln, l, tq, tk, unroll, n_ctx):
    B, H, _, T, _ = q.shape
    dqk = lam_q.shape[-1]
    lam_init = 0.8 - 0.6 * math.exp(-0.3 * l)
    unroll = math.gcd(unroll, (T - n_ctx) // tk)
    assert unroll % ATT_SLOTS == 0 and ATT_LOOKAHEAD < ATT_SLOTS
    assert (T - n_ctx) % (unroll * tk) == 0 and n_ctx % tq == 0 and n_ctx % 128 == 0 and tk % 128 == 0
    kern = functools.partial(_flash_kernel, tq=tq, tk=tk, unroll=unroll, n_ctx=n_ctx, T=T,
                             lam_init=lam_init)
    return pl.pallas_call(
        kern,
        grid=(B, H, T // tq),
        in_specs=[pl.BlockSpec((None, 2, dqk), lambda b, h, i: (l, 0, 0)),
                  pl.BlockSpec((None, 2, dqk), lambda b, h, i: (l, 0, 0)),
                  pl.BlockSpec((None, None, HEAD_W, 1), lambda b, h, i: (l, h, 0, 0)),
                  pl.BlockSpec((1, 1, 2, tq, HEAD_W), lambda b, h, i: (b, h, 0, i, 0)),
                  pl.BlockSpec((1, 1, T, HEAD_W), lambda b, h, i: (b, h, 0, 0)),
                  pl.BlockSpec((1, 1, HEAD_W, T), lambda b, h, i: (b, h, 0, 0))],
        out_specs=pl.BlockSpec((1, HEAD_W, tq), lambda b, h, i: (b, h, i)),
        out_shape=jax.ShapeDtypeStruct((B, H * HEAD_W, T), F32),
        scratch_shapes=[pltpu.VMEM((ATT_SLOTS, max(tk, n_ctx), 2 * tq), F32),
                        pltpu.VMEM((HEAD_W + 8, 2 * tq), F32)],
        compiler_params=_cparams(("parallel", "parallel", "arbitrary")),
        name="diff_attention",
    )(lam_q, lam_k, subln, q, k, v)


def _rwprep_kernel(c_ref, p_ref, n_ref, mu_ref, kk_ref, ka_ref, rk_ref, w0_ref, a0_ref,
                   w2_ref, a2_ref, g2_ref, ones_ref, rvk_ref, dir_ref, aux_ref, *, tm, n_ctx, T):
    i = pl.program_id(1)
    c = c_ref[0]
    t0 = i * tm
    seg_start = jnp.logical_or(t0 == 0, t0 == n_ctx)
    seg_end = jnp.logical_or(t0 + tm == n_ctx, t0 + tm == T)
    pv = jnp.where(seg_start, 0.0, p_ref[0][7:8, :])
    nx = jnp.where(seg_end, 0.0, n_ref[0][0:1, :])
    row = lax.broadcasted_iota(jnp.int32, c.shape, 0)
    prev = jnp.where(row == 0, pv, pltpu.roll(c, 1, 0))
    nxt = jnp.where(row == tm - 1, nx, pltpu.roll(c, tm - 1, 0))
    xs = c + (0.5 * (prev + nxt) - c) * mu_ref[...]
    W = BRANCH_W
    r, k, v = xs[:, 0:W], xs[:, W:2 * W], xs[:, 2 * W:3 * W]
    lo, gl = xs[:, 3 * W:4 * W], xs[:, 4 * W:]
    ones_bd = ones_ref[...]
    kk = k * kk_ref[...]
    kk = kk * lax.rsqrt(jnp.maximum(_dotf(kk * kk, ones_bd), 1e-12))
    wpre = _dotf(jnp.tanh(lo), w2_ref[...])
    apre = _dotf(lo, a2_ref[...])
    gate = _dotf(_sigmoid(gl), g2_ref[...])
    rb = jnp.zeros_like(r)
    for d in range(2):
        z = w0_ref[d:d + 1, :] + wpre[:, d * W:(d + 1) * W]
        w = -_softplus(-z) - 0.5
        a = _sigmoid(a0_ref[d:d + 1, :] + apre[:, d * W:(d + 1) * W])
        kd = k * (1.0 + (a - 1.0) * ka_ref[...])
        dir_ref[0, d, :, 0:W] = -jnp.exp(w)
        dir_ref[0, d, :, W:2 * W] = kd
        dir_ref[0, d, :, 2 * W:3 * W] = kk * a
        rb = rb + r * kd * rk_ref[...]
    rvk_ref[0, :, 0:W] = r
    rvk_ref[0, :, W:2 * W] = v
    rvk_ref[0, :, 2 * W:3 * W] = kk
    aux_ref[0, :, 0:W] = _dotf(rb, ones_bd) * v
    aux_ref[0, :, W:2 * W] = gate


def _rwkv_prep(cols, prm, l, tm, n_ctx):
    B, T, NB = cols.shape
    W = BRANCH_W
    nblk8 = T // 8
    r8 = tm // 8
    vec = lambda n: pl.BlockSpec((None, 1, n), lambda b, i: (l, 0, 0))
    pair = lambda n: pl.BlockSpec((None, 2, n), lambda b, i: (l, 0, 0))
    mat = lambda m, n: _const_spec((None, m, n), lambda b, i: (l, 0, 0))
    return pl.pallas_call(
        functools.partial(_rwprep_kernel, tm=tm, n_ctx=n_ctx, T=T),
        grid=(B, T // tm),
        in_specs=[pl.BlockSpec((1, tm, NB), lambda b, i: (b, i, 0)),
                  pl.BlockSpec((1, 8, NB), lambda b, i: (b, jnp.maximum(i * r8 - 1, 0), 0)),
                  pl.BlockSpec((1, 8, NB), lambda b, i: (b, jnp.minimum((i + 1) * r8, nblk8 - 1), 0)),
                  vec(NB), vec(W), vec(W), vec(W), pair(W), pair(W),
                  mat(W, 2 * W), mat(W, 2 * W), mat(NB - 4 * W, W),
                  _const_spec((W, W), lambda b, i: (0, 0))],
        out_specs=[pl.BlockSpec((1, tm, 3 * W), lambda b, i: (b, i, 0)),
                   pl.BlockSpec((1, 2, tm, 3 * W), lambda b, i: (b, 0, i, 0)),
                   pl.BlockSpec((1, tm, 2 * W), lambda b, i: (b, i, 0))],
        out_shape=[jax.ShapeDtypeStruct((B, T, 3 * W), F32),
                   jax.ShapeDtypeStruct((B, 2, T, 3 * W), F32),
                   jax.ShapeDtypeStruct((B, T, 2 * W), F32)],
        compiler_params=_cparams(("parallel", "parallel")),
        name="rwkv_prep",
    )(cols, cols, cols, prm["mu"], prm["kk"], prm["ka"], prm["rk"], prm["w0"], prm["a0"],
      prm["w2"], prm["a2"], prm["g2"], prm["ones_bd"])


def _rw_chunks(chains):
    C, W = RW_CHUNK, BRANCH_W
    n = len(chains)
    each = lambda f, *cols: [f(*a) for a in zip(*cols)]
    r, v, kk, lw, kd, b, s0, sgn = (list(col) for col in zip(*chains))

    ti = lax.broadcasted_iota(jnp.int32, (C, C), 0)
    tj = lax.broadcasted_iota(jnp.int32, (C, C), 1)
    lane = lax.broadcasted_iota(jnp.int32, (1, W), 1)
    head_of_lane = lane >> 6
    stack = lambda x: jnp.concatenate(
        [jnp.where(head_of_lane == h, x, 0.0) for h in range(N_HEADS)], axis=0)
    tile = lambda x: jnp.concatenate([x] * N_HEADS, axis=0)
    ri = lax.broadcasted_iota(jnp.int32, (W, W), 0)
    ci = lax.broadcasted_iota(jnp.int32, (W, W), 1)
    same_head = (ri >> 6) == (ci >> 6)
    dt = (ci & (C - 1)) - (ri & (C - 1))
    masks = {sg: (jnp.where((tj - ti) * sg <= 0, 1.0, 0.0),
                  jnp.logical_and(same_head, dt * sg < 0),
                  jnp.logical_and(same_head, dt * sg <= 0)) for sg in set(sgn)}
    tri = [masks[sg][0] for sg in sgn]
    strict = [masks[sg][1] for sg in sgn]
    incl = [masks[sg][2] for sg in sgn]
    blk16 = (ri >> 4) == (ci >> 4)
    blk32 = (ri >> 5) == (ci >> 5)
    eye = jnp.where(ri == ci, 1.0, 0.0)

    log_g = each(_dotf, tri, lw)
    log_gc = [jnp.sum(x, axis=0, keepdims=True) for x in lw]
    kk_t = each(lambda kk_, lg, lw_: kk_ * jnp.exp(lg - lw_), kk, log_g, lw)
    r_t = each(lambda r_, lg: r_ * jnp.exp(lg), r, log_g)
    g_inv = [jnp.exp(-lg) for lg in log_g]
    g_tail = each(lambda lgc, lg: jnp.exp(lgc - lg), log_gc, log_g)

    left = each(lambda a, c: jnp.concatenate([stack(a), stack(c)], axis=0).astype(BF16), kk_t, r_t)
    right = each(lambda b_, kd_, gi: jnp.concatenate([stack(b_ * gi), stack(kd_ * gi)], axis=0).astype(BF16),
                 b, kd, g_inv)
    g = each(_dot_nt, left, right)
    m_sl = each(lambda st, g_: jnp.where(st, g_[:W, :W], 0.0), strict, g)
    n_sl = each(lambda st, g_: jnp.where(st, g_[:W, W:], 0.0), strict, g)
    pq_l = each(lambda ic, g_: jnp.concatenate([jnp.where(ic, g_[W:, :W], 0.0),
                                                jnp.where(ic, g_[W:, W:], 0.0)], axis=1).astype(BF16), incl, g)

    g2 = each(lambda a, c, s_: _sdot(jnp.concatenate([a, c], axis=0), s_), kk_t, r_t, s0)
    vs = [tile(x).astype(BF16) for x in v]
    rhs = each(lambda g2_, n_, vs_: -(tile(g2_[:C]) + _sdot(n_, vs_)), g2, n_sl, vs)

    d0 = [jnp.where(blk16, m_, 0.0).astype(BF16) for m_ in m_sl]
    a2 = each(_sdot, d0, d0)
    a4 = each(_sdot, a2, a2)
    a8 = each(_sdot, a4, a4)
    t = [eye - d_ for d_ in d0]
    for a_ in (a2, a4, a8):
        t = each(lambda t_, x: t_ + _sdot(t_, x), t, a_)
    for cm in (jnp.logical_and(blk32, jnp.logical_not(blk16)), jnp.logical_not(blk32)):
        u = each(lambda m_, t_: _sdot(jnp.where(cm, m_, 0.0), t_), m_sl, t)
        t = each(lambda t_, u_: t_ - _sdot(t_, u_), t, u)

    sa = each(_sdot, t, rhs)
    sav = each(lambda sa_, vs_: jnp.concatenate([sa_.astype(BF16), vs_], axis=0), sa, vs)
    ys = each(lambda g2_, pq, sav_: tile(g2_[C:]) + _sdot(pq, sav_), g2, pq_l, sav)
    ds = each(lambda b_, kd_, gt, sav_: _sdot_tn(jnp.concatenate([stack(b_ * gt), stack(kd_ * gt)], axis=0),
                                                 sav_), b, kd, g_tail, sav)
    out = []
    for i in range(n):
        y = jnp.zeros((C, W), F32)
        for h in range(N_HEADS):
            y = y + jnp.where(head_of_lane == h, ys[i][h * C:(h + 1) * C], 0.0)
        gc_col = jnp.sum(jnp.where(ri == ci, jnp.exp(log_gc[i]), 0.0), axis=1, keepdims=True)
        out.append((y, jnp.where(same_head, s0[i] * gc_col + ds[i], 0.0)))
    return out


def _rwscan_kernel(rvf_ref, rvb_ref, df_ref, db_ref, yf_ref, yb_ref, s_ref):
    W = BRANCH_W

    @pl.when(pl.program_id(0) == 0)
    def _():
        s_ref[...] = jnp.zeros(s_ref.shape, F32)

    ids = [(bi, d) for bi in range(rvf_ref.shape[0]) for d in range(2)]
    refs = ((rvf_ref, df_ref, yf_ref), (rvb_ref, db_ref, yb_ref))
    chains = []
    for bi, d in ids:
        rv, dr, _ = refs[d]
        chains.append((rv[bi, :, 0:W], rv[bi, :, W:2 * W], rv[bi, :, 2 * W:3 * W],
                       dr[bi, 0, :, 0:W], dr[bi, 0, :, W:2 * W], dr[bi, 0, :, 2 * W:3 * W],
                       s_ref[2 * bi + d], 1 - 2 * d))
    for (bi, d), (y, s_new) in zip(ids, _rw_chunks(chains)):
        refs[d][2][bi] = y
        s_ref[2 * bi + d] = s_new


def _chunk_order(s, d, n_ctx_chunks, n_chunks):
    bwd = jnp.where(s < n_ctx_chunks, n_ctx_chunks - 1 - s, n_chunks - 1 - (s - n_ctx_chunks))
    return jnp.where(d == 0, s, bwd)


def _rwkv_scan(rvk, dirs, n_ctx):
    B, T, _ = rvk.shape
    C, W = RW_CHUNK, BRANCH_W
    nc, ncx = T // C, n_ctx // C
    ch = lambda s, d: _chunk_order(s, d, ncx, nc)
    rv_spec = lambda d: pl.BlockSpec((B, C, 3 * W), lambda s: (0, ch(s, d), 0))
    dir_spec = lambda d: pl.BlockSpec((B, 1, C, 3 * W), lambda s: (0, d, ch(s, d), 0))
    y_spec = lambda d: pl.BlockSpec((B, C, W), lambda s: (0, ch(s, d), 0))
    return pl.pallas_call(
        _rwscan_kernel,
        grid=(nc,),
        in_specs=[rv_spec(0), rv_spec(1), dir_spec(0), dir_spec(1)],
        out_specs=[y_spec(0), y_spec(1)],
        out_shape=[jax.ShapeDtypeStruct((B, T, W), F32)] * 2,
        scratch_shapes=[pltpu.VMEM((2 * B, W, W), F32)],
        compiler_params=_cparams(("arbitrary",)),
        name="rwkv_scan",
    )(rvk, rvk, dirs, dirs)


def _ret_chunk(c, cos, sm, sp, dec, s0, d, half, scale):
    C, W = RET_CHUNK, BRANCH_W
    QW = W // 2
    q = _rope(c[:, 0:QW], cos, sm, sp, half)
    k = _rope(c[:, QW:2 * QW], cos, sm, sp, half) * scale
    v = c[:, 2 * QW:2 * QW + W]
    log_g = -_softplus(-dec)
    lane_q = lax.broadcasted_iota(jnp.int32, (1, QW), 1) >> 5
    lane_v = lax.broadcasted_iota(jnp.int32, (1, W), 1) >> 6
    lg_q = jnp.zeros((1, QW), F32)
    lg_v = jnp.zeros((1, W), F32)
    for h in range(N_HEADS):
        lg_q = jnp.where(lane_q == h, log_g[:, h:h + 1], lg_q)
        lg_v = jnp.where(lane_v == h, log_g[:, h:h + 1], lg_v)
    idx = lax.broadcasted_iota(jnp.int32, (C, 1), 0).astype(F32)
    q_dec = jnp.exp((idx + 1.0 if d == 0 else C - idx) * lg_q)
    k_dec = jnp.exp((C - 1.0 - idx if d == 0 else idx) * lg_q)
    ti = lax.broadcasted_iota(jnp.int32, (C, C), 0)
    tj = lax.broadcasted_iota(jnp.int32, (C, C), 1)
    rel = ((ti - tj) * (1 - 2 * d)).astype(F32)

    o = _dotb(q * q_dec, s0)
    kb = k.astype(BF16)
    vb = v.astype(BF16)
    for h in range(N_HEADS):
        dm = jnp.where(rel >= 0, jnp.exp(jnp.maximum(rel, 0.0) * log_g[:, h:h + 1]), 0.0)
        qh = jnp.where(lane_q == h, q, 0.0).astype(BF16)
        sc = _dot_nt(qh, kb) * dm
        o = o + jnp.where(lane_v == h, jnp.dot(sc.astype(BF16), vb, preferred_element_type=F32), 0.0)

    ri = lax.broadcasted_iota(jnp.int32, (QW, W), 0) >> 5
    ci = lax.broadcasted_iota(jnp.int32, (QW, W), 1) >> 6
    upd = _dot_tn((k * k_dec).astype(BF16), vb)
    return o, jnp.where(ri == ci, s0 * jnp.exp(C * lg_v) + upd, 0.0)


def _ret_kernel(cf_ref, cb_ref, cosf_ref, smf_ref, spf_ref, cosb_ref, smb_ref, spb_ref, dec_ref,
                of_ref, ob_ref, s_ref, *, half, scale):
    @pl.when(pl.program_id(0) == 0)
    def _():
        s_ref[...] = jnp.zeros(s_ref.shape, F32)

    dirs = ((cf_ref, (cosf_ref, smf_ref, spf_ref), of_ref), (cb_ref, (cosb_ref, smb_ref, spb_ref), ob_ref))
    for bi in range(cf_ref.shape[0]):
        for d, (c_ref, tabs, o_ref) in enumerate(dirs):
            o, s_new = _ret_chunk(c_ref[bi], tabs[0][...], tabs[1][...], tabs[2][...], dec_ref[d],
                                  s_ref[2 * bi + d], d, half, scale)
            o_ref[bi] = o
            s_ref[2 * bi + d] = s_new


def _retention(cols, tabs, decay, l, n_ctx):
    B, T, NCOL = cols.shape
    C, W = RET_CHUNK, BRANCH_W
    cos, sm, sp, half = tabs
    nc, ncx = T // C, n_ctx // C
    ch = lambda s, d: _chunk_order(s, d, ncx, nc)
    c_spec = lambda d: pl.BlockSpec((B, C, NCOL), lambda s: (0, ch(s, d), 0))
    tab_spec = lambda d: pl.BlockSpec((C, W // 2), lambda s: (ch(s, d), 0))
    o_spec = lambda d: pl.BlockSpec((B, C, W), lambda s: (0, ch(s, d), 0))
    return pl.pallas_call(
        functools.partial(_ret_kernel, half=half, scale=(HEAD_W // 2) ** -0.5),
        grid=(nc,),
        in_specs=[c_spec(0), c_spec(1), tab_spec(0), tab_spec(0), tab_spec(0),
                  tab_spec(1), tab_spec(1), tab_spec(1),
                  pl.BlockSpec((None, 2, 1, N_HEADS), lambda s: (l, 0, 0, 0))],
        out_specs=[o_spec(0), o_spec(1)],
        out_shape=[jax.ShapeDtypeStruct((B, T, W), F32)] * 2,
        scratch_shapes=[pltpu.VMEM((2 * B, W // 2, W), F32)],
        compiler_params=_cparams(("arbitrary",)),
        name="retention",
    )(cols, cols, cos, sm, sp, cos, sm, sp, decay)


def _store_win_qkv(c, cos, sm, sp, half, q_ref, k_ref, v_ref):
    W = BRANCH_W
    q = (_rope(c[:, 0:W], cos, sm, sp, half) * HEAD_W ** -0.5).astype(BF16)
    kw = W // 2
    k = _rope(c[:, W:W + kw], cos[:, :kw], sm[:, :kw], sp[:, :kw], half).astype(BF16)
    vt = c[:, W + kw:W + 2 * kw].T.astype(BF16)
    for kvh in range(2):
        for g in range(2):
            h = kvh * 2 + g
            q_ref[0, kvh, g] = q[:, h * HEAD_W:(h + 1) * HEAD_W]
        k_ref[0, kvh] = k[:, kvh * HEAD_W:(kvh + 1) * HEAD_W]
        v_ref[0, kvh] = vt[kvh * HEAD_W:(kvh + 1) * HEAD_W, :]


def _winattn_kernel(sink_ref, q_ref, k_ref, v_ref, o_ref, *, tq, n_ctx, T):
    i = pl.program_id(2)
    band = tq + 2 * WINDOW
    q = q_ref[0, 0].reshape(2 * tq, HEAD_W)
    ws = pl.multiple_of(jnp.clip(i * tq - WINDOW, n_ctx, T - band), WINDOW)
    s_ctx = _dot_nt(k_ref[0, 0, 0:n_ctx, :], q)
    s_band = _dot_nt(k_ref[0, 0, pl.ds(ws, band), :], q)
    kpos = ws + lax.broadcasted_iota(jnp.int32, (band, tq), 0)
    qpos = i * tq + lax.broadcasted_iota(jnp.int32, (band, tq), 1)
    valid = jnp.logical_and(jnp.abs(qpos - kpos) <= WINDOW, qpos >= n_ctx)
    s_band = jnp.where(jnp.concatenate([valid, valid], axis=1), s_band, NEG_INF)
    sink = sink_ref[...]
    m = jnp.maximum(jnp.maximum(jnp.max(s_ctx, axis=0, keepdims=True),
                                jnp.max(s_band, axis=0, keepdims=True)), sink)
    p_ctx = jnp.exp(s_ctx - m).astype(BF16)
    p_band = jnp.exp(s_band - m).astype(BF16)
    vx = jnp.concatenate([v_ref[0, 0, :, 0:n_ctx], jnp.ones((8, n_ctx), BF16)], axis=0)
    vb = jnp.concatenate([v_ref[0, 0, :, pl.ds(ws, band)], jnp.ones((8, band), BF16)], axis=0)
    acc = jnp.dot(jnp.concatenate([vx, vb], axis=1), jnp.concatenate([p_ctx, p_band], axis=0),
                  preferred_element_type=F32)
    o = acc[0:HEAD_W] / (acc[HEAD_W:HEAD_W + 1] + jnp.exp(sink - m))
    for g in range(2):
        o_ref[0, g * HEAD_W:(g + 1) * HEAD_W, :] = o[:, g * tq:(g + 1) * tq]


def _window_attention(q, k, v, sink_col, l, tq, n_ctx):
    B, KVH, G, T, _ = q.shape
    return pl.pallas_call(
        functools.partial(_winattn_kernel, tq=tq, n_ctx=n_ctx, T=T),
        grid=(B, KVH, T // tq),
        in_specs=[pl.BlockSpec((None, None, 1, G * tq), lambda b, h, i: (l, h, 0, 0)),
                  pl.BlockSpec((1, 1, G, tq, HEAD_W), lambda b, h, i: (b, h, 0, i, 0)),
                  pl.BlockSpec((1, 1, T, HEAD_W), lambda b, h, i: (b, h, 0, 0)),
                  pl.BlockSpec((1, 1, HEAD_W, T), lambda b, h, i: (b, h, 0, 0))],
        out_specs=pl.BlockSpec((1, G * HEAD_W, tq), lambda b, h, i: (b, h, i)),
        out_shape=jax.ShapeDtypeStruct((B, KVH * G * HEAD_W, T), F32),
        compiler_params=_cparams(("parallel", "parallel", "arbitrary")),
        name="window_attention",
    )(sink_col, q, k, v)


def _group_norm(x, ones_bd, eps, center):
    inv = 1.0 / HEAD_W
    if center:
        x = x - _dotf(x, ones_bd) * inv
    return x * lax.rsqrt(_dotf(x * x, ones_bd) * inv + eps)


def _merge_kernel(x_ref, gates_ref, ya_ref, ybf_ref, ybb_ref, aux_ref, ocf_ref, ocb_ref, gc_ref, yd_ref,
                  lnx_g_ref, lnx_b_ref, gn_ref, wb_ref, wo_ref, pn_ref, mod_ref, ones_ref, o_ref, *,
                  tm, n_ctx):
    W = BRANCH_W
    D = x_ref.shape[-1]
    ones_bd = ones_ref[...]
    yb = _group_norm(ybf_ref[0] + ybb_ref[0], ones_bd, RW_GN_EPS, True)
    yb = yb * lnx_g_ref[...] + lnx_b_ref[...]
    yb = (yb + aux_ref[0, :, 0:W]) * aux_ref[0, :, W:2 * W]
    yc = _group_norm(ocf_ref[0] + ocb_ref[0], ones_bd, 1e-5, True) * gn_ref[...]
    gc = gc_ref[0]
    yc = yc * (gc * _sigmoid(gc))

    lifted = [_dot_tn(ya_ref[0].astype(BF16), wb_ref[0]),
              jnp.dot(yb.astype(BF16), wb_ref[1], preferred_element_type=F32),
              jnp.dot(yc.astype(BF16), wb_ref[2], preferred_element_type=F32),
              _dot_tn(yd_ref[0].astype(BF16), wb_ref[3])]
    m = None
    for n in range(4):
        t = _sigmoid(gates_ref[0, :, n * D:(n + 1) * D].astype(F32)) * lifted[n]
        m = t if m is None else m + t
    out = jnp.dot(m.astype(BF16), wo_ref[...], preferred_element_type=F32)
    o_ref[0] = x_ref[0] + _mod_row(mod_ref, 2, tm, n_ctx) * (_rms(out, 1e-6) * pn_ref[...])


def _merge(xs, gates, ya, yb, aux, oc, cols_c, yd, prm, mods, l, tm, n_ctx):
    B, T, D = xs.shape
    W = BRANCH_W
    vec = lambda n: pl.BlockSpec((None, 1, n), lambda b, i: (l, 0, 0))
    tok = lambda n: pl.BlockSpec((1, tm, n), lambda b, i: (b, i, 0))
    feat = pl.BlockSpec((1, W, tm), lambda b, i: (b, 0, i))
    return pl.pallas_call(
        functools.partial(_merge_kernel, tm=tm, n_ctx=n_ctx),
        grid=(B, T // tm),
        in_specs=[tok(D), tok(4 * D), feat, tok(W), tok(W), tok(2 * W), tok(W), tok(W),
                  pl.BlockSpec((1, tm, W), lambda b, i: (b, i, 2)),
                  feat,
                  vec(W), vec(W), vec(W),
                  _const_spec((None, 4, W, D), lambda b, i: (l, 0, 0, 0)),
                  _const_spec((None, D, D), lambda b, i: (l, 0, 0)),
                  vec(D), _mod_spec(l, D),
                  _const_spec((W, W), lambda b, i: (0, 0))],
        out_specs=tok(D),
        out_shape=jax.ShapeDtypeStruct((B, T, D), F32),
        compiler_params=_cparams(("parallel", "parallel")),
        name="merge",
    )(xs, gates, ya, yb[0], yb[1], aux, oc[0], oc[1], cols_c, yd, prm["lnx_g"], prm["lnx_b"], prm["ret_gn"],
      prm["w_branch"], prm["w_out"], prm["norm_post_mix"], mods, prm["ones_bd"])


def _mlp_kernel(x_ref, g_ref, pn_ref, mod_ref, wu_ref, wd_ref, o_ref, *, tm, n_ctx):
    x = x_ref[0]
    h = _rms(x, 1e-6) * g_ref[...]
    hb = (h * (1.0 + _mod_row(mod_ref, 4, tm, n_ctx)) + _mod_row(mod_ref, 3, tm, n_ctx)).astype(BF16)
    F = wu_ref.shape[-1]
    y = None
    for f0 in range(0, F, F // MLP_SPLIT):
        sl = slice(f0, f0 + F // MLP_SPLIT)
        u = jnp.dot(hb, wu_ref[:, sl], preferred_element_type=F32)
        u = jnp.square(jnp.maximum(u, 0.0)).astype(BF16)
        t = jnp.dot(u, wd_ref[sl, :], preferred_element_type=F32)
        y = t if y is None else y + t
    o_ref[0] = x + _mod_row(mod_ref, 5, tm, n_ctx) * (_rms(y, 1e-6) * pn_ref[...])


def _mlp(xs, prm, mods, l, tm, n_ctx):
    B, T, D = xs.shape
    F = prm["w_up"].shape[-1]
    vec = lambda n: pl.BlockSpec((None, 1, n), lambda b, i: (l, 0, 0))
    return pl.pallas_call(
        functools.partial(_mlp_kernel, tm=tm, n_ctx=n_ctx),
        grid=(B, T // tm),
        in_specs=[pl.BlockSpec((1, tm, D), lambda b, i: (b, i, 0)),
                  vec(D), vec(D), _mod_spec(l, D),
                  _const_spec((None, D, F), lambda b, i: (l, 0, 0)),
                  _const_spec((None, F, D), lambda b, i: (l, 0, 0))],
        out_specs=pl.BlockSpec((1, tm, D), lambda b, i: (b, i, 0)),
        out_shape=jax.ShapeDtypeStruct((B, T, D), F32),
        compiler_params=_cparams(("parallel", "parallel")),
        name="mlp",
    )(xs, prm["norm_pre_mlp"], prm["norm_post_mlp"], mods, prm["w_up"], prm["w_down"])


def kernel(x, c, ctx, c_ctx, ada_w, ada_b, norm_pre_mix, norm_post_mix, norm_pre_mlp, norm_post_mlp,
           w_in, diff_lam_q, diff_lam_k, diff_subln, rwkv_mu, rwkv_w0, rwkv_w2, rwkv_a0, rwkv_a2,
           rwkv_g2, rwkv_kk, rwkv_ka, rwkv_rk, rwkv_lnx_g, rwkv_lnx_b, ret_decay, ret_gn, win_sink,
           w_branch, w_out, w_up, w_down):
    B, S, D = x.shape
    n_ctx = ctx.shape[1]
    T = n_ctx + S
    L = ada_w.shape[0]
    W = BRANCH_W
    TM = 256
    TD = next(t for t in DENSE_TILES if T % t == 0)
    assert n_ctx % TM == 0 and S % TM == 0 and S % GRID_W == 0 and B + 1 <= 8

    cc = jnp.zeros((8, D), F32).at[:B].set(c).at[B].set(c_ctx)
    mod = _ada_mod(cc, ada_w, ada_b).reshape(L, 8, 6, D)
    mod_ctx = jnp.broadcast_to(mod[:, B][:, None], (L, B, 6, D))
    mods = jnp.stack([mod_ctx, mod[:, :B]], axis=2)
    mods = jnp.pad(mods, ((0, 0), (0, 0), (0, 0), (0, 2), (0, 0)))

    lat = jnp.arange(S, dtype=jnp.int32)
    row = (lat // GRID_W).astype(F32)
    col = (lat % GRID_W).astype(F32)
    tabs_a = _rope_tables([row, col], W, HEAD_W // 2, n_ctx)
    tabs_d = _rope_tables([row, col], W, HEAD_W, n_ctx)
    tabs_c = _rope_tables([lat.astype(F32)], W // 2, HEAD_W // 2, n_ctx)

    sizes = (3 * W, rwkv_mu.shape[-1], 3 * W, 2 * W, 4 * D)
    offs = np.concatenate([[0], np.cumsum(sizes)])
    w_groups = [w_in[:, :, int(offs[n]):int(offs[n + 1])].astype(BF16) for n in range(5)]

    lw_w, la_w = rwkv_w2.shape[2], rwkv_a2.shape[2]
    zw = jnp.zeros((L, W, 2 * W), F32)
    w2cat = (zw.at[:, 0:lw_w, 0:W].set(rwkv_w2[:, 0])
             .at[:, lw_w:2 * lw_w, W:2 * W].set(rwkv_w2[:, 1]))
    a2cat = (zw.at[:, 2 * lw_w:2 * lw_w + la_w, 0:W].set(rwkv_a2[:, 0])
             .at[:, 2 * lw_w + la_w:2 * lw_w + 2 * la_w, W:2 * W].set(rwkv_a2[:, 1]))
    blk = np.arange(W) // HEAD_W
    ones_bd = jnp.asarray((blk[:, None] == blk[None, :]).astype(np.float32))
    v3 = lambda a: a.reshape(L, 1, -1)
    prm = dict(mu=v3(rwkv_mu), kk=v3(rwkv_kk), ka=v3(rwkv_ka), rk=v3(rwkv_rk), w0=rwkv_w0, a0=rwkv_a0,
               w2=w2cat, a2=a2cat, g2=rwkv_g2, ones_bd=ones_bd,
               lnx_g=v3(rwkv_lnx_g), lnx_b=v3(rwkv_lnx_b), ret_gn=v3(ret_gn),
               w_branch=w_branch.astype(BF16), w_out=w_out.astype(BF16),
               norm_post_mix=v3(norm_post_mix), norm_pre_mlp=v3(norm_pre_mlp),
               norm_post_mlp=v3(norm_post_mlp), w_up=w_up.astype(BF16), w_down=w_down.astype(BF16))
    subln = diff_subln.reshape(L, N_HEADS, HEAD_W, 1)
    decay = ret_decay.reshape(L, 2, 1, N_HEADS)
    TQ_D = 256
    sink_col = jnp.broadcast_to(win_sink.reshape(L, 2, 1, 2, 1), (L, 2, 1, 2, TQ_D)).reshape(L, 2, 1, 2 * TQ_D)
    gain_pre = v3(norm_pre_mix)

    xs = jnp.concatenate([ctx, x], axis=1)
    for l in range(L):
        qa, ka, va, cols_b, cols_c, qd, kd, vd, gates = _input_proj(
            xs, gain_pre, mods, w_groups, tabs_a, tabs_d, l, TD, n_ctx)
        ya = _diff_attention(qa, ka, va, diff_lam_q, diff_lam_k, subln, l, ATT_TQ, ATT_TK, ATT_UNROLL, n_ctx)
        rvk, dirs, aux = _rwkv_prep(cols_b, prm, l, TM, n_ctx)
        yb = _rwkv_scan(rvk, dirs, n_ctx)
        oc = _retention(cols_c, tabs_c, decay, l, n_ctx)
        yd = _window_attention(qd, kd, vd, sink_col, l, TQ_D, n_ctx)
        xs = _merge(xs, gates, ya, yb, aux, oc, cols_c, yd, prm, mods, l, TD, n_ctx)
        xs = _mlp(xs, prm, mods, l, TD, n_ctx)
    return xs[:, n_ctx:]
```

```python
import functools
import math

import numpy as np
import jax
import jax.numpy as jnp
from jax import lax
from jax.experimental import pallas as pl
from jax.experimental.pallas import tpu as pltpu

F32 = jnp.float32
BF16 = jnp.bfloat16
HI = lax.Precision.HIGHEST

GRID_W = 64
ROPE_BASE = 10000.0
NEG_INF = -1e30
WINDOW = 128
N_HEADS = 4
HEAD_W = 64
BRANCH_W = N_HEADS * HEAD_W
RW_CHUNK = 64
RET_CHUNK = 128
RW_GN_EPS = 64e-5
ATT_TQ, ATT_TK, ATT_UNROLL = 256, 256, 64
ATT_LOOKAHEAD, ATT_SLOTS = 2, 4
DENSE_TILES = (640, 512, 384, 256, 128)
MLP_SPLIT = 2
VMEM_LIMIT = 56 * 1024 * 1024


def _cparams(sem, vmem=None):
    return pltpu.CompilerParams(dimension_semantics=sem, vmem_limit_bytes=vmem or VMEM_LIMIT)


def _dotf(a, b):
    return jnp.dot(a, b, precision=HI, preferred_element_type=F32)


def _dotb(a, b):
    return jnp.dot(a.astype(BF16), b.astype(BF16), preferred_element_type=F32)


def _dot_nt(a, b, precision=None):
    return lax.dot_general(a, b, (((1,), (1,)), ((), ())), precision=precision,
                           preferred_element_type=F32)


def _dot_tn(a, b, precision=None):
    return lax.dot_general(a, b, (((0,), (0,)), ((), ())), precision=precision,
                           preferred_element_type=F32)


def _split(x):
    hi = x.astype(BF16)
    return hi, (x - hi.astype(F32)).astype(BF16)


def _dot_hl(x, w3):
    hi, lo = _split(x)
    return jnp.dot(jnp.concatenate([hi, hi, lo], axis=1), w3, preferred_element_type=F32)


def _group_sum(x, ones2):
    hi, lo = _split(x)
    return jnp.dot(jnp.concatenate([hi, lo], axis=1), ones2, preferred_element_type=F32)


def _sdot(a, b):
    return jnp.dot(a.astype(BF16), b.astype(BF16), preferred_element_type=F32)


def _sdot_nt(a, b):
    return _dot_nt(a.astype(BF16), b.astype(BF16))


def _sdot_tn(a, b):
    return _dot_tn(a.astype(BF16), b.astype(BF16))


def _sigmoid(x):
    return 1.0 / (1.0 + jnp.exp(-x))


def _softplus(x):
    return jnp.maximum(x, 0.0) + jnp.log1p(jnp.exp(-jnp.abs(x)))


def _rms(x, eps):
    return x * lax.rsqrt(jnp.mean(x * x, axis=-1, keepdims=True) + eps)


def _const_spec(shape, index):
    return pl.BlockSpec(shape, index, pipeline_mode=pl.Buffered(1))


def _ada_kernel(c_ref, w_ref, b_ref, o_ref):
    c = c_ref[...]
    o_ref[0] = _dotf(c * _sigmoid(c), w_ref[0]) + b_ref[0]


def _ada_mod(cc, ada_w, ada_b):
    L, D, N = ada_w.shape
    tn = N // 4
    return pl.pallas_call(
        _ada_kernel,
        grid=(L, N // tn),
        in_specs=[pl.BlockSpec((8, D), lambda l, n: (0, 0)),
                  pl.BlockSpec((1, D, tn), lambda l, n: (l, 0, n)),
                  pl.BlockSpec((1, 1, tn), lambda l, n: (l, 0, n))],
        out_specs=pl.BlockSpec((1, 8, tn), lambda l, n: (l, 0, n)),
        out_shape=jax.ShapeDtypeStruct((L, 8, N), F32),
        compiler_params=_cparams(("arbitrary", "arbitrary")),
        name="ada_mod",
    )(cc, ada_w, ada_b.reshape(L, 1, N))


def _mod_row(mod_ref, j, tm, n_ctx):
    tok = pl.program_id(1) * tm + lax.broadcasted_iota(jnp.int32, (tm, 1), 0)
    return jnp.where(tok < n_ctx, mod_ref[0, j:j + 1, :], mod_ref[1, j:j + 1, :])


def _mod_spec(l, D):
    return pl.BlockSpec((None, None, 2, 8, D), lambda b, i: (l, b, 0, 0, 0))


def _proj_kernel(x_ref, g_ref, mod_ref, wa_ref, wb_ref, wc_ref, wd_ref, wg_ref,
                 cosa_ref, sma_ref, spa_ref, cosd_ref, smd_ref, spd_ref,
                 qa_ref, ka_ref, va_ref, cb_ref, cc_ref, qd_ref, kd_ref, vd_ref, gates_ref, *,
                 tm, n_ctx, half_a, half_d):
    h = _rms(x_ref[0], 1e-6) * g_ref[...]
    h = h * (1.0 + _mod_row(mod_ref, 1, tm, n_ctx)) + _mod_row(mod_ref, 0, tm, n_ctx)
    hb = h.astype(BF16)
    proj = lambda w_ref: jnp.dot(hb, w_ref[...], preferred_element_type=F32)
    _store_diff_qkv(proj(wa_ref), cosa_ref[...], sma_ref[...], spa_ref[...], half_a, qa_ref, ka_ref, va_ref)
    cb_ref[0] = proj(wb_ref)
    cc_ref[0] = proj(wc_ref)
    _store_win_qkv(proj(wd_ref), cosd_ref[...], smd_ref[...], spd_ref[...], half_d, qd_ref, kd_ref, vd_ref)
    gates_ref[0] = proj(wg_ref).astype(gates_ref.dtype)


def _input_proj(xs, gain, mods, weights, tabs_a, tabs_d, l, tm, n_ctx):
    B, T, D = xs.shape
    W = BRANCH_W
    tok = lambda n: pl.BlockSpec((1, tm, n), lambda b, i: (b, i, 0))
    tab = pl.BlockSpec((tm, W), lambda b, i: (i, 0))
    in_specs = [tok(D), pl.BlockSpec((None, 1, D), lambda b, i: (l, 0, 0)), _mod_spec(l, D)]
    in_specs += [_const_spec((None, D, w.shape[-1]), lambda b, i: (l, 0, 0)) for w in weights]
    in_specs += [tab] * 6
    heads = lambda *lead: pl.BlockSpec((1,) + lead + (tm, HEAD_W), lambda b, i: (b,) + (0,) * len(lead) + (i, 0))
    heads_t = lambda n: pl.BlockSpec((1, n, HEAD_W, tm), lambda b, i: (b, 0, 0, i))
    out_specs = [heads(N_HEADS, 2), heads(N_HEADS), heads_t(N_HEADS), tok(weights[1].shape[-1]),
                 tok(weights[2].shape[-1]), heads(2, 2), heads(2), heads_t(2), tok(weights[4].shape[-1])]
    sds = jax.ShapeDtypeStruct
    out_shape = [sds((B, N_HEADS, 2, T, HEAD_W), BF16), sds((B, N_HEADS, T, HEAD_W), BF16),
                 sds((B, N_HEADS, HEAD_W, T), BF16), sds((B, T, weights[1].shape[-1]), F32),
                 sds((B, T, weights[2].shape[-1]), F32), sds((B, 2, 2, T, HEAD_W), BF16),
                 sds((B, 2, T, HEAD_W), BF16), sds((B, 2, HEAD_W, T), BF16),
                 sds((B, T, weights[4].shape[-1]), BF16)]
    return pl.pallas_call(
        functools.partial(_proj_kernel, tm=tm, n_ctx=n_ctx, half_a=tabs_a[3], half_d=tabs_d[3]),
        grid=(B, T // tm),
        in_specs=in_specs, out_specs=out_specs, out_shape=out_shape,
        compiler_params=_cparams(("parallel", "parallel")),
        name="input_proj",
    )(xs, gain, mods, *weights, *tabs_a[:3], *tabs_d[:3])


def _rope_tables(pos_sets, lanes, group, n_ctx):
    n_sets = len(pos_sets)
    sub = group // n_sets
    half = sub // 2
    angs = []
    for pos in pos_sets:
        inv_freq = ROPE_BASE ** (-jnp.arange(half, dtype=F32) / half)
        ang = pos[:, None] * inv_freq[None, :]
        angs.append(jnp.concatenate([ang, ang], axis=-1))
    ang = jnp.concatenate(angs, axis=-1)
    ang = jnp.tile(ang, (1, lanes // group))
    ang = jnp.concatenate([jnp.zeros((n_ctx, lanes), F32), ang], axis=0)
    first = (np.arange(lanes) % sub) < half
    cos, sin = jnp.cos(ang), jnp.sin(ang)
    sin_minus = jnp.where(first[None, :], -sin, 0.0)
    sin_plus = jnp.where(first[None, :], 0.0, sin)
    return cos, sin_minus, sin_plus, half


def _rope(x, cos, sin_minus, sin_plus, half):
    n = x.shape[-1]
    return x * cos + pltpu.roll(x, n - half, 1) * sin_minus + pltpu.roll(x, half, 1) * sin_plus


def _store_diff_qkv(c, cos, sm, sp, half, q_ref, k_ref, v_ref):
    scale = (HEAD_W // 2) ** -0.5 * math.log2(math.e)
    q = _rope(c[:, 0:BRANCH_W], cos, sm, sp, half) * scale
    k = _rope(c[:, BRANCH_W:2 * BRANCH_W], cos, sm, sp, half)
    v = c[:, 2 * BRANCH_W:3 * BRANCH_W]
    lane = lax.broadcasted_iota(jnp.int32, q.shape, 1)
    comp = (lane >> 5) & 1
    q0 = jnp.where(comp == 0, q, 0.0).astype(BF16)
    q1 = jnp.where(comp == 1, q, 0.0).astype(BF16)
    kb, vt = k.astype(BF16), v.T.astype(BF16)
    for h in range(N_HEADS):
        sl = slice(h * HEAD_W, (h + 1) * HEAD_W)
        q_ref[0, h, 0] = q0[:, sl]
        q_ref[0, h, 1] = q1[:, sl]
        k_ref[0, h] = kb[:, sl]
        v_ref[0, h] = vt[sl, :]


def _flash_kernel(lq_ref, lk_ref, g_ref, q_ref, k_ref, vt_ref, o_ref, s_sc, acc_sc, *,
                  tq, tk, unroll, n_ctx, T, lam_init):
    qi = pl.program_id(2)
    q = q_ref[0, 0].reshape(2 * tq, HEAD_W)
    def scores(slot, off, size):
        s_sc[slot, 0:size] = _dot_nt(k_ref[0, 0, pl.ds(off, size), :], q)

    def absorb(slot, off, size, m):
        s = s_sc[slot, 0:size]
        m_new = jnp.maximum(m, jnp.max(s, axis=0, keepdims=True))
        p = jnp.exp2(s - m_new).astype(BF16)
        vt = jnp.concatenate([vt_ref[0, 0, :, pl.ds(off, size)], jnp.ones((8, size), BF16)], axis=0)
        acc_sc[...] = acc_sc[...] * jnp.exp2(m - m_new) + jnp.dot(vt, p, preferred_element_type=F32)
        return m_new

    n_slots, look = s_sc.shape[0], ATT_LOOKAHEAD
    lat_off = lambda c: n_ctx + (c - 1) * tk
    acc_sc[...] = jnp.zeros(acc_sc.shape, F32)
    m = jnp.full((1, 2 * tq), NEG_INF, F32)
    scores(0, 0, n_ctx)
    for c in range(1, look + 1):
        scores(c % n_slots, min(lat_off(c), T - tk), tk)
    m = absorb(0, 0, n_ctx, m)

    def body(jj, m):
        c0 = 1 + jj * unroll
        for u in range(unroll):
            off = pl.multiple_of(lat_off(c0 + u), math.gcd(n_ctx, tk))
            nxt = pl.multiple_of(jnp.minimum(lat_off(c0 + u + look), T - tk), math.gcd(n_ctx, tk))
            scores((1 + u + look) % n_slots, nxt, tk)
            m = absorb((1 + u) % n_slots, off, tk, m)
        return m

    n_trips = (T - n_ctx) // (unroll * tk)
    lax.fori_loop(0, jnp.where(qi * tq < n_ctx, 0, n_trips), body, m)
    acc = acc_sc[...]
    o = acc[0:HEAD_W] / acc[HEAD_W:HEAD_W + 1]
    e0 = jnp.exp(jnp.sum(lq_ref[0:1, :] * lk_ref[0:1, :], axis=1, keepdims=True))
    e1 = jnp.exp(jnp.sum(lq_ref[1:2, :] * lk_ref[1:2, :], axis=1, keepdims=True))
    lam = e0 - e1 + lam_init
    y = o[:, :tq] - lam * o[:, tq:]
    o_ref[0] = y * lax.rsqrt(jnp.mean(y * y, axis=0, keepdims=True) + 1e-5) * g_ref[...] * (1.0 - lam_init)


def _diff_attention(q, k, v, lam_q, lam_k, subln, l, tq, tk, unroll, n_ctx):
    B, H, _, T, _ = q.shape
    dqk = lam_q.shape[-1]
    lam_init = 0.8 - 0.6 * math.exp(-0.3 * l)
    unroll = math.gcd(unroll, (T - n_ctx) // tk)
    assert unroll % ATT_SLOTS == 0 and ATT_LOOKAHEAD < ATT_SLOTS
    assert (T - n_ctx) % (unroll * tk) == 0 and n_ctx % tq == 0 and n_ctx % 128 == 0 and tk % 128 == 0
    kern = functools.partial(_flash_kernel, tq=tq, tk=tk, unroll=unroll, n_ctx=n_ctx, T=T,
                             lam_init=lam_init)
    return pl.pallas_call(
        kern,
        grid=(B, H, T // tq),
        in_specs=[pl.BlockSpec((None, 2, dqk), lambda b, h, i: (l, 0, 0)),
                  pl.BlockSpec((None, 2, dqk), lambda b, h, i: (l, 0, 0)),
                  pl.BlockSpec((None, None, HEAD_W, 1), lambda b, h, i: (l, h, 0, 0)),
                  pl.BlockSpec((1, 1, 2, tq, HEAD_W), lambda b, h, i: (b, h, 0, i, 0)),
                  pl.BlockSpec((1, 1, T, HEAD_W), lambda b, h, i: (b, h, 0, 0)),
                  pl.BlockSpec((1, 1, HEAD_W, T), lambda b, h, i: (b, h, 0, 0))],
        out_specs=pl.BlockSpec((1, HEAD_W, tq), lambda b, h, i: (b, h, i)),
        out_shape=jax.ShapeDtypeStruct((B, H * HEAD_W, T), F32),
        scratch_shapes=[pltpu.VMEM((ATT_SLOTS, max(tk, n_ctx), 2 * tq), F32),
                        pltpu.VMEM((HEAD_W + 8, 2 * tq), F32)],
        compiler_params=_cparams(("parallel", "parallel", "arbitrary")),
        name="diff_attention",
    )(lam_q, lam_k, subln, q, k, v)


def _rwprep_kernel(c_ref, p_ref, n_ref, mu_ref, kk_ref, ka_ref, rk_ref, w0_ref, a0_ref,
                   w2_ref, a2_ref, g2_ref, ones_ref, rvk_ref, dir_ref, aux_ref, *, tm, n_ctx, T):
    i = pl.program_id(1)
    c = c_ref[0]
    t0 = i * tm
    seg_start = jnp.logical_or(t0 == 0, t0 == n_ctx)
    seg_end = jnp.logical_or(t0 + tm == n_ctx, t0 + tm == T)
    pv = jnp.where(seg_start, 0.0, p_ref[0][7:8, :])
    nx = jnp.where(seg_end, 0.0, n_ref[0][0:1, :])
    row = lax.broadcasted_iota(jnp.int32, c.shape, 0)
    prev = jnp.where(row == 0, pv, pltpu.roll(c, 1, 0))
    nxt = jnp.where(row == tm - 1, nx, pltpu.roll(c, tm - 1, 0))
    xs = c + (0.5 * (prev + nxt) - c) * mu_ref[...]
    W = BRANCH_W
    r, k, v = xs[:, 0:W], xs[:, W:2 * W], xs[:, 2 * W:3 * W]
    lo, gl = xs[:, 3 * W:4 * W], xs[:, 4 * W:]
    ones2 = ones_ref[...]
    kk = k * kk_ref[...]
    kk = kk * lax.rsqrt(jnp.maximum(_group_sum(kk * kk, ones2), 1e-12))
    wpre = _dot_hl(jnp.tanh(lo), w2_ref[...])
    apre = _dot_hl(lo, a2_ref[...])
    gate = _dot_hl(_sigmoid(gl), g2_ref[...])
    rb = jnp.zeros_like(r)
    for d in range(2):
        z = w0_ref[d:d + 1, :] + wpre[:, d * W:(d + 1) * W]
        w = -_softplus(-z) - 0.5
        a = _sigmoid(a0_ref[d:d + 1, :] + apre[:, d * W:(d + 1) * W])
        kd = k * (1.0 + (a - 1.0) * ka_ref[...])
        dir_ref[0, d, :, 0:W] = -jnp.exp(w)
        dir_ref[0, d, :, W:2 * W] = kd
        dir_ref[0, d, :, 2 * W:3 * W] = kk * a
        rb = rb + r * kd * rk_ref[...]
    rvk_ref[0, :, 0:W] = r
    rvk_ref[0, :, W:2 * W] = v
    rvk_ref[0, :, 2 * W:3 * W] = kk
    aux_ref[0, :, 0:W] = _group_sum(rb, ones2) * v
    aux_ref[0, :, W:2 * W] = gate


def _rwkv_prep(cols, prm, l, tm, n_ctx):
    B, T, NB = cols.shape
    W = BRANCH_W
    nblk8 = T // 8
    r8 = tm // 8
    vec = lambda n: pl.BlockSpec((None, 1, n), lambda b, i: (l, 0, 0))
    pair = lambda n: pl.BlockSpec((None, 2, n), lambda b, i: (l, 0, 0))
    mat = lambda m, n: _const_spec((None, m, n), lambda b, i: (l, 0, 0))
    return pl.pallas_call(
        functools.partial(_rwprep_kernel, tm=tm, n_ctx=n_ctx, T=T),
        grid=(B, T // tm),
        in_specs=[pl.BlockSpec((1, tm, NB), lambda b, i: (b, i, 0)),
                  pl.BlockSpec((1, 8, NB), lambda b, i: (b, jnp.maximum(i * r8 - 1, 0), 0)),
                  pl.BlockSpec((1, 8, NB), lambda b, i: (b, jnp.minimum((i + 1) * r8, nblk8 - 1), 0)),
                  vec(NB), vec(W), vec(W), vec(W), pair(W), pair(W),
                  mat(3 * W, 2 * W), mat(3 * W, 2 * W), mat(3 * (NB - 4 * W), W),
                  _const_spec((2 * W, W), lambda b, i: (0, 0))],
        out_specs=[pl.BlockSpec((1, tm, 3 * W), lambda b, i: (b, i, 0)),
                   pl.BlockSpec((1, 2, tm, 3 * W), lambda b, i: (b, 0, i, 0)),
                   pl.BlockSpec((1, tm, 2 * W), lambda b, i: (b, i, 0))],
        out_shape=[jax.ShapeDtypeStruct((B, T, 3 * W), F32),
                   jax.ShapeDtypeStruct((B, 2, T, 3 * W), F32),
                   jax.ShapeDtypeStruct((B, T, 2 * W), F32)],
        compiler_params=_cparams(("parallel", "parallel")),
        name="rwkv_prep",
    )(cols, cols, cols, prm["mu"], prm["kk"], prm["ka"], prm["rk"], prm["w0"], prm["a0"],
      prm["w2"], prm["a2"], prm["g2"], prm["ones_bd"])


def _rw_chunks(chains):
    C, W = RW_CHUNK, BRANCH_W
    n = len(chains)
    each = lambda f, *cols: [f(*a) for a in zip(*cols)]
    r, v, kk, lw, kd, b, s0, sgn = (list(col) for col in zip(*chains))

    ti = lax.broadcasted_iota(jnp.int32, (C, C), 0)
    tj = lax.broadcasted_iota(jnp.int32, (C, C), 1)
    ri = lax.broadcasted_iota(jnp.int32, (W, W), 0)
    ci = lax.broadcasted_iota(jnp.int32, (W, W), 1)
    same_head = (ri >> 6) == (ci >> 6)
    bd = lambda x: jnp.where(same_head, jnp.concatenate([x] * N_HEADS, axis=0), 0.0)
    row = lax.broadcasted_iota(jnp.int32, (C, W), 0)
    col = lax.broadcasted_iota(jnp.int32, (C, W), 1) & (C - 1)
    masks = {sg: (jnp.where((tj - ti) * sg <= 0, 1.0, 0.0), (col - row) * sg < 0, (col - row) * sg <= 0)
             for sg in set(sgn)}
    tri = [masks[sg][0] for sg in sgn]
    strict = [masks[sg][1] for sg in sgn]
    incl = [masks[sg][2] for sg in sgn]
    blk16 = (row >> 4) == (col >> 4)
    blk32 = (row >> 5) == (col >> 5)
    eye = jnp.where(row == col, 1.0, 0.0)

    log_g = each(_dotf, tri, lw)
    log_gc = [jnp.sum(x, axis=0, keepdims=True) for x in lw]
    kk_t = each(lambda kk_, lg, lw_: kk_ * jnp.exp(lg - lw_), kk, log_g, lw)
    r_t = each(lambda r_, lg: r_ * jnp.exp(lg), r, log_g)
    g_inv = [jnp.exp(-lg) for lg in log_g]
    g_tail = each(lambda lgc, lg: jnp.exp(lgc - lg), log_gc, log_g)

    bdb = lambda x: bd(x.astype(BF16))
    left = each(lambda a, c: jnp.concatenate([a, c], axis=0).astype(BF16), kk_t, r_t)
    right = each(lambda b_, kd_, gi: jnp.concatenate([bdb(b_ * gi), bdb(kd_ * gi)], axis=0), b, kd, g_inv)
    g = each(_dot_nt, left, right)
    m_sl = each(lambda st, g_: jnp.where(st, g_[:C, :W], 0.0), strict, g)
    n_sl = each(lambda st, g_: jnp.where(st, g_[:C, W:], 0.0), strict, g)
    pq_l = each(lambda ic, g_: jnp.concatenate([jnp.where(ic, g_[C:, :W], 0.0),
                                                jnp.where(ic, g_[C:, W:], 0.0)], axis=1).astype(BF16), incl, g)

    g2 = each(_sdot, left, s0)
    v_bd = [bdb(x) for x in v]
    rhs = each(lambda g2_, n_, vb: -(g2_[:C] + _sdot(n_, vb)), g2, n_sl, v_bd)

    d0 = [jnp.where(blk16, m_, 0.0) for m_ in m_sl]
    sq = lambda x: _sdot(x, bdb(x))
    a2 = each(sq, d0)
    a4 = each(sq, a2)
    a8 = each(sq, a4)
    t = [eye - d_ for d_ in d0]
    for a_ in (a2, a4, a8):
        t = each(lambda t_, x: t_ + _sdot(t_, bdb(x)), t, a_)
    for cm in (jnp.logical_and(blk32, jnp.logical_not(blk16)), jnp.logical_not(blk32)):
        u = each(lambda m_, t_: _sdot(jnp.where(cm, m_, 0.0), bdb(t_)), m_sl, t)
        t = each(lambda t_, u_: t_ - _sdot(t_, bdb(u_)), t, u)

    sa = each(lambda t_, rh: _sdot(t_, bdb(rh)), t, rhs)
    y = each(lambda g2_, pq, sa_, vb: g2_[C:] + _sdot(pq, jnp.concatenate([bdb(sa_), vb], axis=0)),
             g2, pq_l, sa, v_bd)
    ds = each(lambda b_, kd_, gt, sa_, v_: _sdot_tn(jnp.concatenate([b_ * gt, kd_ * gt], axis=0),
                                                    jnp.concatenate([sa_, v_], axis=0)),
              b, kd, g_tail, sa, v)
    out = []
    for i in range(n):
        gc_col = jnp.sum(jnp.where(ri == ci, jnp.exp(log_gc[i]), 0.0), axis=1, keepdims=True)
        out.append((y[i], jnp.where(same_head, s0[i] * gc_col + ds[i], 0.0)))
    return out


def _rwscan_kernel(rvf_ref, rvb_ref, df_ref, db_ref, yf_ref, yb_ref, s_ref):
    W = BRANCH_W

    @pl.when(pl.program_id(0) == 0)
    def _():
        s_ref[...] = jnp.zeros(s_ref.shape, F32)

    ids = [(bi, d) for bi in range(rvf_ref.shape[0]) for d in range(2)]
    refs = ((rvf_ref, df_ref, yf_ref), (rvb_ref, db_ref, yb_ref))
    chains = []
    for bi, d in ids:
        rv, dr, _ = refs[d]
        chains.append((rv[bi, :, 0:W], rv[bi, :, W:2 * W], rv[bi, :, 2 * W:3 * W],
                       dr[bi, 0, :, 0:W], dr[bi, 0, :, W:2 * W], dr[bi, 0, :, 2 * W:3 * W],
                       s_ref[2 * bi + d], 1 - 2 * d))
    for (bi, d), (y, s_new) in zip(ids, _rw_chunks(chains)):
        refs[d][2][bi] = y
        s_ref[2 * bi + d] = s_new


def _chunk_order(s, d, n_ctx_chunks, n_chunks):
    bwd = jnp.where(s < n_ctx_chunks, n_ctx_chunks - 1 - s, n_chunks - 1 - (s - n_ctx_chunks))
    return jnp.where(d == 0, s, bwd)


def _rwkv_scan(rvk, dirs, n_ctx):
    B, T, _ = rvk.shape
    C, W = RW_CHUNK, BRANCH_W
    nc, ncx = T // C, n_ctx // C
    ch = lambda s, d: _chunk_order(s, d, ncx, nc)
    rv_spec = lambda d: pl.BlockSpec((B, C, 3 * W), lambda s: (0, ch(s, d), 0))
    dir_spec = lambda d: pl.BlockSpec((B, 1, C, 3 * W), lambda s: (0, d, ch(s, d), 0))
    y_spec = lambda d: pl.BlockSpec((B, C, W), lambda s: (0, ch(s, d), 0))
    return pl.pallas_call(
        _rwscan_kernel,
        grid=(nc,),
        in_specs=[rv_spec(0), rv_spec(1), dir_spec(0), dir_spec(1)],
        out_specs=[y_spec(0), y_spec(1)],
        out_shape=[jax.ShapeDtypeStruct((B, T, W), F32)] * 2,
        scratch_shapes=[pltpu.VMEM((2 * B, W, W), F32)],
        compiler_params=_cparams(("arbitrary",)),
        name="rwkv_scan",
    )(rvk, rvk, dirs, dirs)


def _ret_chunk(c, cos, sm, sp, dec, s0, d, half, scale):
    C, W = RET_CHUNK, BRANCH_W
    QW = W // 2
    q = _rope(c[:, 0:QW], cos, sm, sp, half)
    k = _rope(c[:, QW:2 * QW], cos, sm, sp, half) * scale
    v = c[:, 2 * QW:2 * QW + W]
    log_g = -_softplus(-dec)
    lane_q = lax.broadcasted_iota(jnp.int32, (1, QW), 1) >> 5
    lane_v = lax.broadcasted_iota(jnp.int32, (1, W), 1) >> 6
    lg_q = jnp.zeros((1, QW), F32)
    lg_v = jnp.zeros((1, W), F32)
    for h in range(N_HEADS):
        lg_q = jnp.where(lane_q == h, log_g[:, h:h + 1], lg_q)
        lg_v = jnp.where(lane_v == h, log_g[:, h:h + 1], lg_v)
    idx = lax.broadcasted_iota(jnp.int32, (C, 1), 0).astype(F32)
    q_dec = jnp.exp((idx + 1.0 if d == 0 else C - idx) * lg_q)
    k_dec = jnp.exp((C - 1.0 - idx if d == 0 else idx) * lg_q)
    ti = lax.broadcasted_iota(jnp.int32, (C, C), 0)
    tj = lax.broadcasted_iota(jnp.int32, (C, C), 1)
    rel = ((ti - tj) * (1 - 2 * d)).astype(F32)

    o = _dotb(q * q_dec, s0)
    kb = k.astype(BF16)
    vb = v.astype(BF16)
    for h in range(N_HEADS):
        dm = jnp.where(rel >= 0, jnp.exp(jnp.maximum(rel, 0.0) * log_g[:, h:h + 1]), 0.0)
        qh = jnp.where(lane_q == h, q, 0.0).astype(BF16)
        sc = _dot_nt(qh, kb) * dm
        o = o + jnp.where(lane_v == h, jnp.dot(sc.astype(BF16), vb, preferred_element_type=F32), 0.0)

    ri = lax.broadcasted_iota(jnp.int32, (QW, W), 0) >> 5
    ci = lax.broadcasted_iota(jnp.int32, (QW, W), 1) >> 6
    upd = _dot_tn((k * k_dec).astype(BF16), vb)
    return o, jnp.where(ri == ci, s0 * jnp.exp(C * lg_v) + upd, 0.0)


def _ret_kernel(cf_ref, cb_ref, cosf_ref, smf_ref, spf_ref, cosb_ref, smb_ref, spb_ref, dec_ref,
                of_ref, ob_ref, s_ref, *, half, scale):
    @pl.when(pl.program_id(0) == 0)
    def _():
        s_ref[...] = jnp.zeros(s_ref.shape, F32)

    dirs = ((cf_ref, (cosf_ref, smf_ref, spf_ref), of_ref), (cb_ref, (cosb_ref, smb_ref, spb_ref), ob_ref))
    for bi in range(cf_ref.shape[0]):
        for d, (c_ref, tabs, o_ref) in enumerate(dirs):
            o, s_new = _ret_chunk(c_ref[bi], tabs[0][...], tabs[1][...], tabs[2][...], dec_ref[d],
                                  s_ref[2 * bi + d], d, half, scale)
            o_ref[bi] = o
            s_ref[2 * bi + d] = s_new


def _retention(cols, tabs, decay, l, n_ctx):
    B, T, NCOL = cols.shape
    C, W = RET_CHUNK, BRANCH_W
    cos, sm, sp, half = tabs
    nc, ncx = T // C, n_ctx // C
    ch = lambda s, d: _chunk_order(s, d, ncx, nc)
    c_spec = lambda d: pl.BlockSpec((B, C, NCOL), lambda s: (0, ch(s, d), 0))
    tab_spec = lambda d: pl.BlockSpec((C, W // 2), lambda s: (ch(s, d), 0))
    o_spec = lambda d: pl.BlockSpec((B, C, W), lambda s: (0, ch(s, d), 0))
    return pl.pallas_call(
        functools.partial(_ret_kernel, half=half, scale=(HEAD_W // 2) ** -0.5),
        grid=(nc,),
        in_specs=[c_spec(0), c_spec(1), tab_spec(0), tab_spec(0), tab_spec(0),
                  tab_spec(1), tab_spec(1), tab_spec(1),
                  pl.BlockSpec((None, 2, 1, N_HEADS), lambda s: (l, 0, 0, 0))],
        out_specs=[o_spec(0), o_spec(1)],
        out_shape=[jax.ShapeDtypeStruct((B, T, W), F32)] * 2,
        scratch_shapes=[pltpu.VMEM((2 * B, W // 2, W), F32)],
        compiler_params=_cparams(("arbitrary",)),
        name="retention",
    )(cols, cols, cos, sm, sp, cos, sm, sp, decay)


def _store_win_qkv(c, cos, sm, sp, half, q_ref, k_ref, v_ref):
    W = BRANCH_W
    q = (_rope(c[:, 0:W], cos, sm, sp, half) * HEAD_W ** -0.5).astype(BF16)
    kw = W // 2
    k = _rope(c[:, W:W + kw], cos[:, :kw], sm[:, :kw], sp[:, :kw], half).astype(BF16)
    vt = c[:, W + kw:W + 2 * kw].T.astype(BF16)
    for kvh in range(2):
        for g in range(2):
            h = kvh * 2 + g
            q_ref[0, kvh, g] = q[:, h * HEAD_W:(h + 1) * HEAD_W]
        k_ref[0, kvh] = k[:, kvh * HEAD_W:(kvh + 1) * HEAD_W]
        v_ref[0, kvh] = vt[kvh * HEAD_W:(kvh + 1) * HEAD_W, :]


def _winattn_kernel(sink_ref, q_ref, k_ref, v_ref, o_ref, *, tq, n_ctx, T):
    i = pl.program_id(2)
    band = tq + 2 * WINDOW
    q = q_ref[0, 0].reshape(2 * tq, HEAD_W)
    ws = pl.multiple_of(jnp.clip(i * tq - WINDOW, n_ctx, T - band), WINDOW)
    s_ctx = _dot_nt(k_ref[0, 0, 0:n_ctx, :], q)
    s_band = _dot_nt(k_ref[0, 0, pl.ds(ws, band), :], q)
    kpos = ws + lax.broadcasted_iota(jnp.int32, (band, tq), 0)
    qpos = i * tq + lax.broadcasted_iota(jnp.int32, (band, tq), 1)
    valid = jnp.logical_and(jnp.abs(qpos - kpos) <= WINDOW, qpos >= n_ctx)
    s_band = jnp.where(jnp.concatenate([valid, valid], axis=1), s_band, NEG_INF)
    sink = sink_ref[...]
    m = jnp.maximum(jnp.maximum(jnp.max(s_ctx, axis=0, keepdims=True),
                                jnp.max(s_band, axis=0, keepdims=True)), sink)
    p_ctx = jnp.exp(s_ctx - m).astype(BF16)
    p_band = jnp.exp(s_band - m).astype(BF16)
    vx = jnp.concatenate([v_ref[0, 0, :, 0:n_ctx], jnp.ones((8, n_ctx), BF16)], axis=0)
    vb = jnp.concatenate([v_ref[0, 0, :, pl.ds(ws, band)], jnp.ones((8, band), BF16)], axis=0)
    acc = jnp.dot(jnp.concatenate([vx, vb], axis=1), jnp.concatenate([p_ctx, p_band], axis=0),
                  preferred_element_type=F32)
    o = acc[0:HEAD_W] / (acc[HEAD_W:HEAD_W + 1] + jnp.exp(sink - m))
    for g in range(2):
        o_ref[0, g * HEAD_W:(g + 1) * HEAD_W, :] = o[:, g * tq:(g + 1) * tq]


def _window_attention(q, k, v, sink_col, l, tq, n_ctx):
    B, KVH, G, T, _ = q.shape
    return pl.pallas_call(
        functools.partial(_winattn_kernel, tq=tq, n_ctx=n_ctx, T=T),
        grid=(B, KVH, T // tq),
        in_specs=[pl.BlockSpec((None, None, 1, G * tq), lambda b, h, i: (l, h, 0, 0)),
                  pl.BlockSpec((1, 1, G, tq, HEAD_W), lambda b, h, i: (b, h, 0, i, 0)),
                  pl.BlockSpec((1, 1, T, HEAD_W), lambda b, h, i: (b, h, 0, 0)),
                  pl.BlockSpec((1, 1, HEAD_W, T), lambda b, h, i: (b, h, 0, 0))],
        out_specs=pl.BlockSpec((1, G * HEAD_W, tq), lambda b, h, i: (b, h, i)),
        out_shape=jax.ShapeDtypeStruct((B, KVH * G * HEAD_W, T), F32),
        compiler_params=_cparams(("parallel", "parallel", "arbitrary")),
        name="window_attention",
    )(sink_col, q, k, v)


def _group_norm(x, ones2, eps, center):
    inv = 1.0 / HEAD_W
    if center:
        x = x - _group_sum(x, ones2) * inv
    return x * lax.rsqrt(_group_sum(x * x, ones2) * inv + eps)


def _merge_kernel(x_ref, gates_ref, ya_ref, ybf_ref, ybb_ref, aux_ref, ocf_ref, ocb_ref, gc_ref, yd_ref,
                  lnx_g_ref, lnx_b_ref, gn_ref, wb_ref, wo_ref, pn_ref, mod_ref, ones_ref, o_ref, *,
                  tm, n_ctx):
    W = BRANCH_W
    D = x_ref.shape[-1]
    ones_bd = ones_ref[...]
    yb = _group_norm(ybf_ref[0] + ybb_ref[0], ones_bd, RW_GN_EPS, True)
    yb = yb * lnx_g_ref[...] + lnx_b_ref[...]
    yb = (yb + aux_ref[0, :, 0:W]) * aux_ref[0, :, W:2 * W]
    yc = _group_norm(ocf_ref[0] + ocb_ref[0], ones_bd, 1e-5, True) * gn_ref[...]
    gc = gc_ref[0]
    yc = yc * (gc * _sigmoid(gc))

    lifted = [_dot_tn(ya_ref[0].astype(BF16), wb_ref[0]),
              jnp.dot(yb.astype(BF16), wb_ref[1], preferred_element_type=F32),
              jnp.dot(yc.astype(BF16), wb_ref[2], preferred_element_type=F32),
              _dot_tn(yd_ref[0].astype(BF16), wb_ref[3])]
    m = None
    for n in range(4):
        t = _sigmoid(gates_ref[0, :, n * D:(n + 1) * D].astype(F32)) * lifted[n]
        m = t if m is None else m + t
    out = jnp.dot(m.astype(BF16), wo_ref[...], preferred_element_type=F32)
    o_ref[0] = x_ref[0] + _mod_row(mod_ref, 2, tm, n_ctx) * (_rms(out, 1e-6) * pn_ref[...])


def _merge(xs, gates, ya, yb, aux, oc, cols_c, yd, prm, mods, l, tm, n_ctx):
    B, T, D = xs.shape
    W = BRANCH_W
    vec = lambda n: pl.BlockSpec((None, 1, n), lambda b, i: (l, 0, 0))
    tok = lambda n: pl.BlockSpec((1, tm, n), lambda b, i: (b, i, 0))
    feat = pl.BlockSpec((1, W, tm), lambda b, i: (b, 0, i))
    return pl.pallas_call(
        functools.partial(_merge_kernel, tm=tm, n_ctx=n_ctx),
        grid=(B, T // tm),
        in_specs=[tok(D), tok(4 * D), feat, tok(W), tok(W), tok(2 * W), tok(W), tok(W),
                  pl.BlockSpec((1, tm, W), lambda b, i: (b, i, 2)),
                  feat,
                  vec(W), vec(W), vec(W),
                  _const_spec((None, 4, W, D), lambda b, i: (l, 0, 0, 0)),
                  _const_spec((None, D, D), lambda b, i: (l, 0, 0)),
                  vec(D), _mod_spec(l, D),
                  _const_spec((2 * W, W), lambda b, i: (0, 0))],
        out_specs=tok(D),
        out_shape=jax.ShapeDtypeStruct((B, T, D), F32),
        compiler_params=_cparams(("parallel", "parallel")),
        name="merge",
    )(xs, gates, ya, yb[0], yb[1], aux, oc[0], oc[1], cols_c, yd, prm["lnx_g"], prm["lnx_b"], prm["ret_gn"],
      prm["w_branch"], prm["w_out"], prm["norm_post_mix"], mods, prm["ones_bd"])


def _mlp_kernel(x_ref, g_ref, pn_ref, mod_ref, wu_ref, wd_ref, o_ref, *, tm, n_ctx):
    x = x_ref[0]
    h = _rms(x, 1e-6) * g_ref[...]
    hb = (h * (1.0 + _mod_row(mod_ref, 4, tm, n_ctx)) + _mod_row(mod_ref, 3, tm, n_ctx)).astype(BF16)
    F = wu_ref.shape[-1]
    y = None
    for f0 in range(0, F, F // MLP_SPLIT):
        sl = slice(f0, f0 + F // MLP_SPLIT)
        u = jnp.dot(hb, wu_ref[:, sl], preferred_element_type=F32)
        u = jnp.square(jnp.maximum(u, 0.0)).astype(BF16)
        t = jnp.dot(u, wd_ref[sl, :], preferred_element_type=F32)
        y = t if y is None else y + t
    o_ref[0] = x + _mod_row(mod_ref, 5, tm, n_ctx) * (_rms(y, 1e-6) * pn_ref[...])


def _mlp(xs, prm, mods, l, tm, n_ctx):
    B, T, D = xs.shape
    F = prm["w_up"].shape[-1]
    vec = lambda n: pl.BlockSpec((None, 1, n), lambda b, i: (l, 0, 0))
    return pl.pallas_call(
        functools.partial(_mlp_kernel, tm=tm, n_ctx=n_ctx),
        grid=(B, T // tm),
        in_specs=[pl.BlockSpec((1, tm, D), lambda b, i: (b, i, 0)),
                  vec(D), vec(D), _mod_spec(l, D),
                  _const_spec((None, D, F), lambda b, i: (l, 0, 0)),
                  _const_spec((None, F, D), lambda b, i: (l, 0, 0))],
        out_specs=pl.BlockSpec((1, tm, D), lambda b, i: (b, i, 0)),
        out_shape=jax.ShapeDtypeStruct((B, T, D), F32),
        compiler_params=_cparams(("parallel", "parallel")),
        name="mlp",
    )(xs, prm["norm_pre_mlp"], prm["norm_post_mlp"], mods, prm["w_up"], prm["w_down"])


def kernel(x, c, ctx, c_ctx, ada_w, ada_b, norm_pre_mix, norm_post_mix, norm_pre_mlp, norm_post_mlp,
           w_in, diff_lam_q, diff_lam_k, diff_subln, rwkv_mu, rwkv_w0, rwkv_w2, rwkv_a0, rwkv_a2,
           rwkv_g2, rwkv_kk, rwkv_ka, rwkv_rk, rwkv_lnx_g, rwkv_lnx_b, ret_decay, ret_gn, win_sink,
           w_branch, w_out, w_up, w_down):
    B, S, D = x.shape
    n_ctx = ctx.shape[1]
    T = n_ctx + S
    L = ada_w.shape[0]
    W = BRANCH_W
    TM = 256
    TD = next(t for t in DENSE_TILES if T % t == 0)
    assert n_ctx % TM == 0 and S % TM == 0 and S % GRID_W == 0 and B + 1 <= 8

    cc = jnp.zeros((8, D), F32).at[:B].set(c).at[B].set(c_ctx)
    mod = _ada_mod(cc, ada_w, ada_b).reshape(L, 8, 6, D)
    mod_ctx = jnp.broadcast_to(mod[:, B][:, None], (L, B, 6, D))
    mods = jnp.stack([mod_ctx, mod[:, :B]], axis=2)
    mods = jnp.pad(mods, ((0, 0), (0, 0), (0, 0), (0, 2), (0, 0)))

    lat = jnp.arange(S, dtype=jnp.int32)
    row = (lat // GRID_W).astype(F32)
    col = (lat % GRID_W).astype(F32)
    tabs_a = _rope_tables([row, col], W, HEAD_W // 2, n_ctx)
    tabs_d = _rope_tables([row, col], W, HEAD_W, n_ctx)
    tabs_c = _rope_tables([lat.astype(F32)], W // 2, HEAD_W // 2, n_ctx)

    sizes = (3 * W, rwkv_mu.shape[-1], 3 * W, 2 * W, 4 * D)
    offs = np.concatenate([[0], np.cumsum(sizes)])
    w_groups = [w_in[:, :, int(offs[n]):int(offs[n + 1])].astype(BF16) for n in range(5)]

    lw_w, la_w = rwkv_w2.shape[2], rwkv_a2.shape[2]
    zw = jnp.zeros((L, W, 2 * W), F32)
    w2cat = (zw.at[:, 0:lw_w, 0:W].set(rwkv_w2[:, 0])
             .at[:, lw_w:2 * lw_w, W:2 * W].set(rwkv_w2[:, 1]))
    a2cat = (zw.at[:, 2 * lw_w:2 * lw_w + la_w, 0:W].set(rwkv_a2[:, 0])
             .at[:, 2 * lw_w + la_w:2 * lw_w + 2 * la_w, W:2 * W].set(rwkv_a2[:, 1]))
    blk = np.arange(W) // HEAD_W
    ones_bd = (blk[:, None] == blk[None, :]).astype(np.float32)
    ones_bd = jnp.asarray(np.concatenate([ones_bd, ones_bd], axis=0), BF16)
    v3 = lambda a: a.reshape(L, 1, -1)

    def hl3(w):
        hi = w.astype(BF16)
        return jnp.concatenate([hi, (w - hi.astype(F32)).astype(BF16), hi], axis=-2)

    prm = dict(mu=v3(rwkv_mu), kk=v3(rwkv_kk), ka=v3(rwkv_ka), rk=v3(rwkv_rk), w0=rwkv_w0, a0=rwkv_a0,
               w2=hl3(w2cat), a2=hl3(a2cat), g2=hl3(rwkv_g2), ones_bd=ones_bd,
               lnx_g=v3(rwkv_lnx_g), lnx_b=v3(rwkv_lnx_b), ret_gn=v3(ret_gn),
               w_branch=w_branch.astype(BF16), w_out=w_out.astype(BF16),
               norm_post_mix=v3(norm_post_mix), norm_pre_mlp=v3(norm_pre_mlp),
               norm_post_mlp=v3(norm_post_mlp), w_up=w_up.astype(BF16), w_down=w_down.astype(BF16))
    subln = diff_subln.reshape(L, N_HEADS, HEAD_W, 1)
    decay = ret_decay.reshape(L, 2, 1, N_HEADS)
    TQ_D = 256
    sink_col = jnp.broadcast_to(win_sink.reshape(L, 2, 1, 2, 1), (L, 2, 1, 2, TQ_D)).reshape(L, 2, 1, 2 * TQ_D)
    gain_pre = v3(norm_pre_mix)

    xs = jnp.concatenate([ctx, x], axis=1)
    for l in range(L):
        qa, ka, va, cols_b, cols_c, qd, kd, vd, gates = _input_proj(
            xs, gain_pre, mods, w_groups, tabs_a, tabs_d, l, TD, n_ctx)
        ya = _diff_attention(qa, ka, va, diff_lam_q, diff_lam_k, subln, l, ATT_TQ, ATT_TK, ATT_UNROLL, n_ctx)
        rvk, dirs, aux = _rwkv_prep(cols_b, prm, l, TM, n_ctx)
        yb = _rwkv_scan(rvk, dirs, n_ctx)
        oc = _retention(cols_c, tabs_c, decay, l, n_ctx)
        yd = _window_attention(qd, kd, vd, sink_col, l, TQ_D, n_ctx)
        xs = _merge(xs, gates, ya, yb, aux, oc, cols_c, yd, prm, mods, l, TD, n_ctx)
        xs = _mlp(xs, prm, mods, l, TD, n_ctx)
    return xs[:, n_ctx:]
```

```python
import functools
import math

import numpy as np
import jax
import jax.numpy as jnp
from jax import lax
from jax.experimental import pallas as pl
from jax.experimental.pallas import tpu as pltpu

F32 = jnp.float32
BF16 = jnp.bfloat16
HI = lax.Precision.HIGHEST

GRID_W = 64
ROPE_BASE = 10000.0
NEG_INF = -1e30
WINDOW = 128
N_HEADS = 4
HEAD_W = 64
BRANCH_W = N_HEADS * HEAD_W
RW_CHUNK = 64
RET_CHUNK = 128
RW_GN_EPS = 64e-5
ATT_TQ, ATT_TK, ATT_UNROLL = 256, 256, 64
ATT_LOOKAHEAD, ATT_SLOTS = 2, 4
DENSE_TILES = (640, 512, 384, 256, 128)
MLP_SPLIT = 2
VMEM_LIMIT = 56 * 1024 * 1024


def _cparams(sem, vmem=None):
    return pltpu.CompilerParams(dimension_semantics=sem, vmem_limit_bytes=vmem or VMEM_LIMIT)


def _dotf(a, b):
    return jnp.dot(a, b, precision=HI, preferred_element_type=F32)


def _dotb(a, b):
    return jnp.dot(a.astype(BF16), b.astype(BF16), preferred_element_type=F32)


def _dot_nt(a, b, precision=None):
    return lax.dot_general(a, b, (((1,), (1,)), ((), ())), precision=precision,
                           preferred_element_type=F32)


def _dot_tn(a, b, precision=None):
    return lax.dot_general(a, b, (((0,), (0,)), ((), ())), precision=precision,
                           preferred_element_type=F32)


def _split(x):
    hi = x.astype(BF16)
    return hi, (x - hi.astype(F32)).astype(BF16)


def _dot_hl(x, w3):
    hi, lo = _split(x)
    return jnp.dot(jnp.concatenate([hi, hi, lo], axis=1), w3, preferred_element_type=F32)


def _group_sum(x, ones2):
    hi, lo = _split(x)
    return jnp.dot(jnp.concatenate([hi, lo], axis=1), ones2, preferred_element_type=F32)


def _sdot(a, b):
    return jnp.dot(a.astype(BF16), b.astype(BF16), preferred_element_type=F32)


def _sdot_nt(a, b):
    return _dot_nt(a.astype(BF16), b.astype(BF16))


def _sdot_tn(a, b):
    return _dot_tn(a.astype(BF16), b.astype(BF16))


def _sigmoid(x):
    return 1.0 / (1.0 + jnp.exp(-x))


def _softplus(x):
    return jnp.maximum(x, 0.0) + jnp.log1p(jnp.exp(-jnp.abs(x)))


def _rms(x, eps):
    return x * lax.rsqrt(jnp.mean(x * x, axis=-1, keepdims=True) + eps)


def _const_spec(shape, index):
    return pl.BlockSpec(shape, index, pipeline_mode=pl.Buffered(1))


def _ada_kernel(c_ref, w_ref, b_ref, o_ref):
    c = c_ref[...]
    o_ref[0] = _dotf(c * _sigmoid(c), w_ref[0]) + b_ref[0]


def _ada_mod(cc, ada_w, ada_b):
    L, D, N = ada_w.shape
    tn = N // 4
    return pl.pallas_call(
        _ada_kernel,
        grid=(L, N // tn),
        in_specs=[pl.BlockSpec((8, D), lambda l, n: (0, 0)),
                  pl.BlockSpec((1, D, tn), lambda l, n: (l, 0, n)),
                  pl.BlockSpec((1, 1, tn), lambda l, n: (l, 0, n))],
        out_specs=pl.BlockSpec((1, 8, tn), lambda l, n: (l, 0, n)),
        out_shape=jax.ShapeDtypeStruct((L, 8, N), F32),
        compiler_params=_cparams(("arbitrary", "arbitrary")),
        name="ada_mod",
    )(cc, ada_w, ada_b.reshape(L, 1, N))


def _mod_row(mod_ref, j, tm, n_ctx):
    tok = pl.program_id(1) * tm + lax.broadcasted_iota(jnp.int32, (tm, 1), 0)
    return jnp.where(tok < n_ctx, mod_ref[0, j:j + 1, :], mod_ref[1, j:j + 1, :])


def _mod_spec(l, D):
    return pl.BlockSpec((None, None, 2, 8, D), lambda b, i: (l, b, 0, 0, 0))


def _proj_kernel(x_ref, g_ref, mod_ref, wa_ref, wb_ref, wc_ref, wd_ref, wg_ref,
                 cosa_ref, sma_ref, spa_ref, cosd_ref, smd_ref, spd_ref,
                 qa_ref, ka_ref, va_ref, cb_ref, cc_ref, qd_ref, kd_ref, vd_ref, gates_ref, *,
                 tm, n_ctx, half_a, half_d):
    h = _rms(x_ref[0], 1e-6) * g_ref[...]
    h = h * (1.0 + _mod_row(mod_ref, 1, tm, n_ctx)) + _mod_row(mod_ref, 0, tm, n_ctx)
    hb = h.astype(BF16)
    proj = lambda w_ref: jnp.dot(hb, w_ref[...], preferred_element_type=F32)
    _store_diff_qkv(proj(wa_ref), cosa_ref[...], sma_ref[...], spa_ref[...], half_a, qa_ref, ka_ref, va_ref)
    cb_ref[0] = proj(wb_ref)
    cc_ref[0] = proj(wc_ref)
    _store_win_qkv(proj(wd_ref), cosd_ref[...], smd_ref[...], spd_ref[...], half_d, qd_ref, kd_ref, vd_ref)
    gates_ref[0] = proj(wg_ref).astype(gates_ref.dtype)


def _input_proj(xs, gain, mods, weights, tabs_a, tabs_d, l, tm, n_ctx):
    B, T, D = xs.shape
    W = BRANCH_W
    tok = lambda n: pl.BlockSpec((1, tm, n), lambda b, i: (b, i, 0))
    tab = pl.BlockSpec((tm, W), lambda b, i: (i, 0))
    in_specs = [tok(D), pl.BlockSpec((None, 1, D), lambda b, i: (l, 0, 0)), _mod_spec(l, D)]
    in_specs += [_const_spec((None, D, w.shape[-1]), lambda b, i: (l, 0, 0)) for w in weights]
    in_specs += [tab] * 6
    heads = lambda *lead: pl.BlockSpec((1,) + lead + (tm, HEAD_W), lambda b, i: (b,) + (0,) * len(lead) + (i, 0))
    heads_t = lambda n: pl.BlockSpec((1, n, HEAD_W, tm), lambda b, i: (b, 0, 0, i))
    qa_spec = pl.BlockSpec((1, N_HEADS, 2, HEAD_W, tm), lambda b, i: (b, 0, 0, 0, i))
    out_specs = [qa_spec, heads(N_HEADS), heads_t(N_HEADS), tok(weights[1].shape[-1]),
                 tok(weights[2].shape[-1]), pl.BlockSpec((1, W, tm), lambda b, i: (b, 0, i)),
                 heads(2), heads_t(2), tok(weights[4].shape[-1])]
    sds = jax.ShapeDtypeStruct
    out_shape = [sds((B, N_HEADS, 2, HEAD_W, T), BF16), sds((B, N_HEADS, T, HEAD_W), BF16),
                 sds((B, N_HEADS, HEAD_W, T), BF16), sds((B, T, weights[1].shape[-1]), F32),
                 sds((B, T, weights[2].shape[-1]), F32), sds((B, W, T), BF16),
                 sds((B, 2, T, HEAD_W), BF16), sds((B, 2, HEAD_W, T), BF16),
                 sds((B, T, weights[4].shape[-1]), BF16)]
    return pl.pallas_call(
        functools.partial(_proj_kernel, tm=tm, n_ctx=n_ctx, half_a=tabs_a[3], half_d=tabs_d[3]),
        grid=(B, T // tm),
        in_specs=in_specs, out_specs=out_specs, out_shape=out_shape,
        compiler_params=_cparams(("parallel", "parallel")),
        name="input_proj",
    )(xs, gain, mods, *weights, *tabs_a[:3], *tabs_d[:3])


def _rope_tables(pos_sets, lanes, group, n_ctx):
    n_sets = len(pos_sets)
    sub = group // n_sets
    half = sub // 2
    angs = []
    for pos in pos_sets:
        inv_freq = ROPE_BASE ** (-jnp.arange(half, dtype=F32) / half)
        ang = pos[:, None] * inv_freq[None, :]
        angs.append(jnp.concatenate([ang, ang], axis=-1))
    ang = jnp.concatenate(angs, axis=-1)
    ang = jnp.tile(ang, (1, lanes // group))
    ang = jnp.concatenate([jnp.zeros((n_ctx, lanes), F32), ang], axis=0)
    first = (np.arange(lanes) % sub) < half
    cos, sin = jnp.cos(ang), jnp.sin(ang)
    sin_minus = jnp.where(first[None, :], -sin, 0.0)
    sin_plus = jnp.where(first[None, :], 0.0, sin)
    return cos, sin_minus, sin_plus, half


def _rope(x, cos, sin_minus, sin_plus, half):
    n = x.shape[-1]
    return x * cos + pltpu.roll(x, n - half, 1) * sin_minus + pltpu.roll(x, half, 1) * sin_plus


def _store_diff_qkv(c, cos, sm, sp, half, q_ref, k_ref, v_ref):
    scale = (HEAD_W // 2) ** -0.5 * math.log2(math.e)
    q = _rope(c[:, 0:BRANCH_W], cos, sm, sp, half) * scale
    k = _rope(c[:, BRANCH_W:2 * BRANCH_W], cos, sm, sp, half)
    v = c[:, 2 * BRANCH_W:3 * BRANCH_W]
    lane = lax.broadcasted_iota(jnp.int32, q.shape, 1)
    comp = (lane >> 5) & 1
    q0 = jnp.where(comp == 0, q, 0.0).T.astype(BF16)
    q1 = jnp.where(comp == 1, q, 0.0).T.astype(BF16)
    kb, vt = k.astype(BF16), v.T.astype(BF16)
    for h in range(N_HEADS):
        sl = slice(h * HEAD_W, (h + 1) * HEAD_W)
        q_ref[0, h, 0] = q0[sl, :]
        q_ref[0, h, 1] = q1[sl, :]
        k_ref[0, h] = kb[:, sl]
        v_ref[0, h] = vt[sl, :]


def _flash_kernel(lq_ref, lk_ref, g_ref, q_ref, k_ref, vt_ref, o_ref, s_sc, acc_sc, *,
                  tq, tk, unroll, n_ctx, T, lam_init):
    qi = pl.program_id(2)
    qt = jnp.concatenate([q_ref[0, 0, 0], q_ref[0, 0, 1]], axis=1)

    def scores(slot, off, size):
        s_sc[slot, 0:size] = jnp.dot(k_ref[0, 0, pl.ds(off, size), :], qt, preferred_element_type=F32)

    def absorb(slot, off, size, m):
        s = s_sc[slot, 0:size]
        m_new = jnp.maximum(m, jnp.max(s, axis=0, keepdims=True))
        p = jnp.exp2(s - m_new).astype(BF16)
        vt = jnp.concatenate([vt_ref[0, 0, :, pl.ds(off, size)], jnp.ones((8, size), BF16)], axis=0)
        acc_sc[...] = acc_sc[...] * jnp.exp2(m - m_new) + jnp.dot(vt, p, preferred_element_type=F32)
        return m_new

    n_slots, look = s_sc.shape[0], ATT_LOOKAHEAD
    lat_off = lambda c: n_ctx + (c - 1) * tk
    acc_sc[...] = jnp.zeros(acc_sc.shape, F32)
    m = jnp.full((1, 2 * tq), NEG_INF, F32)
    scores(0, 0, n_ctx)
    for c in range(1, look + 1):
        scores(c % n_slots, min(lat_off(c), T - tk), tk)
    m = absorb(0, 0, n_ctx, m)

    def body(jj, m):
        c0 = 1 + jj * unroll
        for u in range(unroll):
            off = pl.multiple_of(lat_off(c0 + u), math.gcd(n_ctx, tk))
            nxt = pl.multiple_of(jnp.minimum(lat_off(c0 + u + look), T - tk), math.gcd(n_ctx, tk))
            scores((1 + u + look) % n_slots, nxt, tk)
            m = absorb((1 + u) % n_slots, off, tk, m)
        return m

    n_trips = (T - n_ctx) // (unroll * tk)
    lax.fori_loop(0, jnp.where(qi * tq < n_ctx, 0, n_trips), body, m)
    acc = acc_sc[...]
    o = acc[0:HEAD_W] / acc[HEAD_W:HEAD_W + 1]
    e0 = jnp.exp(jnp.sum(lq_ref[0:1, :] * lk_ref[0:1, :], axis=1, keepdims=True))
    e1 = jnp.exp(jnp.sum(lq_ref[1:2, :] * lk_ref[1:2, :], axis=1, keepdims=True))
    lam = e0 - e1 + lam_init
    y = o[:, :tq] - lam * o[:, tq:]
    o_ref[0] = y * lax.rsqrt(jnp.mean(y * y, axis=0, keepdims=True) + 1e-5) * g_ref[...] * (1.0 - lam_init)


def _diff_attention(q, k, v, lam_q, lam_k, subln, l, tq, tk, unroll, n_ctx):
    B, H, _, _, T = q.shape
    dqk = lam_q.shape[-1]
    lam_init = 0.8 - 0.6 * math.exp(-0.3 * l)
    unroll = math.gcd(unroll, (T - n_ctx) // tk)
    assert unroll % ATT_SLOTS == 0 and ATT_LOOKAHEAD < ATT_SLOTS
    assert (T - n_ctx) % (unroll * tk) == 0 and n_ctx % tq == 0 and n_ctx % 128 == 0 and tk % 128 == 0
    kern = functools.partial(_flash_kernel, tq=tq, tk=tk, unroll=unroll, n_ctx=n_ctx, T=T,
                             lam_init=lam_init)
    return pl.pallas_call(
        kern,
        grid=(B, H, T // tq),
        in_specs=[pl.BlockSpec((None, 2, dqk), lambda b, h, i: (l, 0, 0)),
                  pl.BlockSpec((None, 2, dqk), lambda b, h, i: (l, 0, 0)),
                  pl.BlockSpec((None, None, HEAD_W, 1), lambda b, h, i: (l, h, 0, 0)),
                  pl.BlockSpec((1, 1, 2, HEAD_W, tq), lambda b, h, i: (b, h, 0, 0, i)),
                  pl.BlockSpec((1, 1, T, HEAD_W), lambda b, h, i: (b, h, 0, 0)),
                  pl.BlockSpec((1, 1, HEAD_W, T), lambda b, h, i: (b, h, 0, 0))],
        out_specs=pl.BlockSpec((1, HEAD_W, tq), lambda b, h, i: (b, h, i)),
        out_shape=jax.ShapeDtypeStruct((B, H * HEAD_W, T), F32),
        scratch_shapes=[pltpu.VMEM((ATT_SLOTS, max(tk, n_ctx), 2 * tq), F32),
                        pltpu.VMEM((HEAD_W + 8, 2 * tq), F32)],
        compiler_params=_cparams(("parallel", "parallel", "arbitrary")),
        name="diff_attention",
    )(lam_q, lam_k, subln, q, k, v)


def _rwprep_kernel(c_ref, p_ref, n_ref, mu_ref, kk_ref, ka_ref, rk_ref, w0_ref, a0_ref,
                   w2_ref, a2_ref, g2_ref, ones_ref, rvk_ref, dir_ref, aux_ref, *, tm, n_ctx, T):
    i = pl.program_id(1)
    c = c_ref[0]
    t0 = i * tm
    seg_start = jnp.logical_or(t0 == 0, t0 == n_ctx)
    seg_end = jnp.logical_or(t0 + tm == n_ctx, t0 + tm == T)
    pv = jnp.where(seg_start, 0.0, p_ref[0][7:8, :])
    nx = jnp.where(seg_end, 0.0, n_ref[0][0:1, :])
    row = lax.broadcasted_iota(jnp.int32, c.shape, 0)
    prev = jnp.where(row == 0, pv, pltpu.roll(c, 1, 0))
    nxt = jnp.where(row == tm - 1, nx, pltpu.roll(c, tm - 1, 0))
    xs = c + (0.5 * (prev + nxt) - c) * mu_ref[...]
    W = BRANCH_W
    r, k, v = xs[:, 0:W], xs[:, W:2 * W], xs[:, 2 * W:3 * W]
    lo, gl = xs[:, 3 * W:4 * W], xs[:, 4 * W:]
    ones2 = ones_ref[...]
    kk = k * kk_ref[...]
    kk = kk * lax.rsqrt(jnp.maximum(_group_sum(kk * kk, ones2), 1e-12))
    wpre = _dot_hl(jnp.tanh(lo), w2_ref[...])
    apre = _dot_hl(lo, a2_ref[...])
    gate = _dot_hl(_sigmoid(gl), g2_ref[...])
    rb = jnp.zeros_like(r)
    for d in range(2):
        z = w0_ref[d:d + 1, :] + wpre[:, d * W:(d + 1) * W]
        w = -_softplus(-z) - 0.5
        a = _sigmoid(a0_ref[d:d + 1, :] + apre[:, d * W:(d + 1) * W])
        kd = k * (1.0 + (a - 1.0) * ka_ref[...])
        dir_ref[0, d, :, 0:W] = -jnp.exp(w)
        dir_ref[0, d, :, W:2 * W] = kd
        dir_ref[0, d, :, 2 * W:3 * W] = kk * a
        rb = rb + r * kd * rk_ref[...]
    rvk_ref[0, :, 0:W] = r
    rvk_ref[0, :, W:2 * W] = v
    rvk_ref[0, :, 2 * W:3 * W] = kk
    aux_ref[0, :, 0:W] = _group_sum(rb, ones2) * v
    aux_ref[0, :, W:2 * W] = gate


def _rwkv_prep(cols, prm, l, tm, n_ctx):
    B, T, NB = cols.shape
    W = BRANCH_W
    nblk8 = T // 8
    r8 = tm // 8
    vec = lambda n: pl.BlockSpec((None, 1, n), lambda b, i: (l, 0, 0))
    pair = lambda n: pl.BlockSpec((None, 2, n), lambda b, i: (l, 0, 0))
    mat = lambda m, n: _const_spec((None, m, n), lambda b, i: (l, 0, 0))
    return pl.pallas_call(
        functools.partial(_rwprep_kernel, tm=tm, n_ctx=n_ctx, T=T),
        grid=(B, T // tm),
        in_specs=[pl.BlockSpec((1, tm, NB), lambda b, i: (b, i, 0)),
                  pl.BlockSpec((1, 8, NB), lambda b, i: (b, jnp.maximum(i * r8 - 1, 0), 0)),
                  pl.BlockSpec((1, 8, NB), lambda b, i: (b, jnp.minimum((i + 1) * r8, nblk8 - 1), 0)),
                  vec(NB), vec(W), vec(W), vec(W), pair(W), pair(W),
                  mat(3 * W, 2 * W), mat(3 * W, 2 * W), mat(3 * (NB - 4 * W), W),
                  _const_spec((2 * W, W), lambda b, i: (0, 0))],
        out_specs=[pl.BlockSpec((1, tm, 3 * W), lambda b, i: (b, i, 0)),
                   pl.BlockSpec((1, 2, tm, 3 * W), lambda b, i: (b, 0, i, 0)),
                   pl.BlockSpec((1, tm, 2 * W), lambda b, i: (b, i, 0))],
        out_shape=[jax.ShapeDtypeStruct((B, T, 3 * W), F32),
                   jax.ShapeDtypeStruct((B, 2, T, 3 * W), F32),
                   jax.ShapeDtypeStruct((B, T, 2 * W), F32)],
        compiler_params=_cparams(("parallel", "parallel")),
        name="rwkv_prep",
    )(cols, cols, cols, prm["mu"], prm["kk"], prm["ka"], prm["rk"], prm["w0"], prm["a0"],
      prm["w2"], prm["a2"], prm["g2"], prm["ones_bd"])


def _rw_chunks(chains):
    C, W = RW_CHUNK, BRANCH_W
    n = len(chains)
    each = lambda f, *cols: [f(*a) for a in zip(*cols)]
    r, v, kk, lw, kd, b, s0, sgn = (list(col) for col in zip(*chains))

    ti = lax.broadcasted_iota(jnp.int32, (C, C), 0)
    tj = lax.broadcasted_iota(jnp.int32, (C, C), 1)
    ri = lax.broadcasted_iota(jnp.int32, (W, W), 0)
    ci = lax.broadcasted_iota(jnp.int32, (W, W), 1)
    same_head = (ri >> 6) == (ci >> 6)
    bd = lambda x: jnp.where(same_head, jnp.concatenate([x] * N_HEADS, axis=0), 0.0)
    row = lax.broadcasted_iota(jnp.int32, (C, W), 0)
    col = lax.broadcasted_iota(jnp.int32, (C, W), 1) & (C - 1)
    masks = {sg: (jnp.where((tj - ti) * sg <= 0, 1.0, 0.0), (col - row) * sg < 0, (col - row) * sg <= 0)
             for sg in set(sgn)}
    tri = [masks[sg][0] for sg in sgn]
    strict = [masks[sg][1] for sg in sgn]
    incl = [masks[sg][2] for sg in sgn]
    blk16 = (row >> 4) == (col >> 4)
    blk32 = (row >> 5) == (col >> 5)
    eye = jnp.where(row == col, 1.0, 0.0)

    log_g = each(_dotf, tri, lw)
    log_gc = [jnp.sum(x, axis=0, keepdims=True) for x in lw]
    kk_t = each(lambda kk_, lg, lw_: kk_ * jnp.exp(lg - lw_), kk, log_g, lw)
    r_t = each(lambda r_, lg: r_ * jnp.exp(lg), r, log_g)
    g_inv = [jnp.exp(-lg) for lg in log_g]
    g_tail = each(lambda lgc, lg: jnp.exp(lgc - lg), log_gc, log_g)

    bdb = lambda x: bd(x.astype(BF16))
    left = each(lambda a, c: jnp.concatenate([a, c], axis=0).astype(BF16), kk_t, r_t)
    right = each(lambda b_, kd_, gi: jnp.concatenate([bdb(b_ * gi), bdb(kd_ * gi)], axis=0), b, kd, g_inv)
    g = each(_dot_nt, left, right)
    m_sl = each(lambda st, g_: jnp.where(st, g_[:C, :W], 0.0), strict, g)
    n_sl = each(lambda st, g_: jnp.where(st, g_[:C, W:], 0.0), strict, g)
    pq_l = each(lambda ic, g_: jnp.concatenate([jnp.where(ic, g_[C:, :W], 0.0),
                                                jnp.where(ic, g_[C:, W:], 0.0)], axis=1).astype(BF16), incl, g)

    g2 = each(_sdot, left, s0)
    v_bd = [bdb(x) for x in v]
    rhs = each(lambda g2_, n_, vb: -(g2_[:C] + _sdot(n_, vb)), g2, n_sl, v_bd)

    d0 = [jnp.where(blk16, m_, 0.0) for m_ in m_sl]
    sq = lambda x: _sdot(x, bdb(x))
    a2 = each(sq, d0)
    a4 = each(sq, a2)
    a8 = each(sq, a4)
    t = [eye - d_ for d_ in d0]
    for a_ in (a2, a4, a8):
        t = each(lambda t_, x: t_ + _sdot(t_, bdb(x)), t, a_)
    for cm in (jnp.logical_and(blk32, jnp.logical_not(blk16)), jnp.logical_not(blk32)):
        u = each(lambda m_, t_: _sdot(jnp.where(cm, m_, 0.0), bdb(t_)), m_sl, t)
        t = each(lambda t_, u_: t_ - _sdot(t_, bdb(u_)), t, u)

    sa = each(lambda t_, rh: _sdot(t_, bdb(rh)), t, rhs)
    y = each(lambda g2_, pq, sa_, vb: g2_[C:] + _sdot(pq, jnp.concatenate([bdb(sa_), vb], axis=0)),
             g2, pq_l, sa, v_bd)
    ds = each(lambda b_, kd_, gt, sa_, v_: _sdot_tn(jnp.concatenate([b_ * gt, kd_ * gt], axis=0),
                                                    jnp.concatenate([sa_, v_], axis=0)),
              b, kd, g_tail, sa, v)
    out = []
    for i in range(n):
        gc_col = jnp.sum(jnp.where(ri == ci, jnp.exp(log_gc[i]), 0.0), axis=1, keepdims=True)
        out.append((y[i], jnp.where(same_head, s0[i] * gc_col + ds[i], 0.0)))
    return out


def _rwscan_kernel(rvf_ref, rvb_ref, df_ref, db_ref, yf_ref, yb_ref, s_ref):
    W = BRANCH_W

    @pl.when(pl.program_id(0) == 0)
    def _():
        s_ref[...] = jnp.zeros(s_ref.shape, F32)

    ids = [(bi, d) for bi in range(rvf_ref.shape[0]) for d in range(2)]
    refs = ((rvf_ref, df_ref, yf_ref), (rvb_ref, db_ref, yb_ref))
    chains = []
    for bi, d in ids:
        rv, dr, _ = refs[d]
        chains.append((rv[bi, :, 0:W], rv[bi, :, W:2 * W], rv[bi, :, 2 * W:3 * W],
                       dr[bi, 0, :, 0:W], dr[bi, 0, :, W:2 * W], dr[bi, 0, :, 2 * W:3 * W],
                       s_ref[2 * bi + d], 1 - 2 * d))
    for (bi, d), (y, s_new) in zip(ids, _rw_chunks(chains)):
        refs[d][2][bi] = y
        s_ref[2 * bi + d] = s_new


def _chunk_order(s, d, n_ctx_chunks, n_chunks):
    bwd = jnp.where(s < n_ctx_chunks, n_ctx_chunks - 1 - s, n_chunks - 1 - (s - n_ctx_chunks))
    return jnp.where(d == 0, s, bwd)


def _rwkv_scan(rvk, dirs, n_ctx):
    B, T, _ = rvk.shape
    C, W = RW_CHUNK, BRANCH_W
    nc, ncx = T // C, n_ctx // C
    ch = lambda s, d: _chunk_order(s, d, ncx, nc)
    rv_spec = lambda d: pl.BlockSpec((B, C, 3 * W), lambda s: (0, ch(s, d), 0))
    dir_spec = lambda d: pl.BlockSpec((B, 1, C, 3 * W), lambda s: (0, d, ch(s, d), 0))
    y_spec = lambda d: pl.BlockSpec((B, C, W), lambda s: (0, ch(s, d), 0))
    return pl.pallas_call(
        _rwscan_kernel,
        grid=(nc,),
        in_specs=[rv_spec(0), rv_spec(1), dir_spec(0), dir_spec(1)],
        out_specs=[y_spec(0), y_spec(1)],
        out_shape=[jax.ShapeDtypeStruct((B, T, W), F32)] * 2,
        scratch_shapes=[pltpu.VMEM((2 * B, W, W), F32)],
        compiler_params=_cparams(("arbitrary",)),
        name="rwkv_scan",
    )(rvk, rvk, dirs, dirs)


def _ret_chunk(c, cos, sm, sp, dec, s0, d, half, scale):
    C, W = RET_CHUNK, BRANCH_W
    QW = W // 2
    q = _rope(c[:, 0:QW], cos, sm, sp, half)
    k = _rope(c[:, QW:2 * QW], cos, sm, sp, half) * scale
    v = c[:, 2 * QW:2 * QW + W]
    log_g = -_softplus(-dec)
    lane_q = lax.broadcasted_iota(jnp.int32, (1, QW), 1) >> 5
    lane_v = lax.broadcasted_iota(jnp.int32, (1, W), 1) >> 6
    lg_q = jnp.zeros((1, QW), F32)
    lg_v = jnp.zeros((1, W), F32)
    for h in range(N_HEADS):
        lg_q = jnp.where(lane_q == h, log_g[:, h:h + 1], lg_q)
        lg_v = jnp.where(lane_v == h, log_g[:, h:h + 1], lg_v)
    idx = lax.broadcasted_iota(jnp.int32, (C, 1), 0).astype(F32)
    q_dec = jnp.exp((idx + 1.0 if d == 0 else C - idx) * lg_q)
    k_dec = jnp.exp((C - 1.0 - idx if d == 0 else idx) * lg_q)
    ti = lax.broadcasted_iota(jnp.int32, (C, C), 0)
    tj = lax.broadcasted_iota(jnp.int32, (C, C), 1)
    rel = ((ti - tj) * (1 - 2 * d)).astype(F32)

    o = _dotb(q * q_dec, s0)
    kb = k.astype(BF16)
    vb = v.astype(BF16)
    for h in range(N_HEADS):
        dm = jnp.where(rel >= 0, jnp.exp(jnp.maximum(rel, 0.0) * log_g[:, h:h + 1]), 0.0)
        qh = jnp.where(lane_q == h, q, 0.0).astype(BF16)
        sc = _dot_nt(qh, kb) * dm
        o = o + jnp.where(lane_v == h, jnp.dot(sc.astype(BF16), vb, preferred_element_type=F32), 0.0)

    ri = lax.broadcasted_iota(jnp.int32, (QW, W), 0) >> 5
    ci = lax.broadcasted_iota(jnp.int32, (QW, W), 1) >> 6
    upd = _dot_tn((k * k_dec).astype(BF16), vb)
    return o, jnp.where(ri == ci, s0 * jnp.exp(C * lg_v) + upd, 0.0)


def _ret_kernel(cf_ref, cb_ref, cosf_ref, smf_ref, spf_ref, cosb_ref, smb_ref, spb_ref, dec_ref,
                of_ref, ob_ref, s_ref, *, half, scale):
    @pl.when(pl.program_id(0) == 0)
    def _():
        s_ref[...] = jnp.zeros(s_ref.shape, F32)

    dirs = ((cf_ref, (cosf_ref, smf_ref, spf_ref), of_ref), (cb_ref, (cosb_ref, smb_ref, spb_ref), ob_ref))
    for bi in range(cf_ref.shape[0]):
        for d, (c_ref, tabs, o_ref) in enumerate(dirs):
            o, s_new = _ret_chunk(c_ref[bi], tabs[0][...], tabs[1][...], tabs[2][...], dec_ref[d],
                                  s_ref[2 * bi + d], d, half, scale)
            o_ref[bi] = o
            s_ref[2 * bi + d] = s_new


def _retention(cols, tabs, decay, l, n_ctx):
    B, T, NCOL = cols.shape
    C, W = RET_CHUNK, BRANCH_W
    cos, sm, sp, half = tabs
    nc, ncx = T // C, n_ctx // C
    ch = lambda s, d: _chunk_order(s, d, ncx, nc)
    c_spec = lambda d: pl.BlockSpec((B, C, NCOL), lambda s: (0, ch(s, d), 0))
    tab_spec = lambda d: pl.BlockSpec((C, W // 2), lambda s: (ch(s, d), 0))
    o_spec = lambda d: pl.BlockSpec((B, C, W), lambda s: (0, ch(s, d), 0))
    return pl.pallas_call(
        functools.partial(_ret_kernel, half=half, scale=(HEAD_W // 2) ** -0.5),
        grid=(nc,),
        in_specs=[c_spec(0), c_spec(1), tab_spec(0), tab_spec(0), tab_spec(0),
                  tab_spec(1), tab_spec(1), tab_spec(1),
                  pl.BlockSpec((None, 2, 1, N_HEADS), lambda s: (l, 0, 0, 0))],
        out_specs=[o_spec(0), o_spec(1)],
        out_shape=[jax.ShapeDtypeStruct((B, T, W), F32)] * 2,
        scratch_shapes=[pltpu.VMEM((2 * B, W // 2, W), F32)],
        compiler_params=_cparams(("arbitrary",)),
        name="retention",
    )(cols, cols, cos, sm, sp, cos, sm, sp, decay)


def _store_win_qkv(c, cos, sm, sp, half, q_ref, k_ref, v_ref):
    W = BRANCH_W
    q_ref[0] = (_rope(c[:, 0:W], cos, sm, sp, half) * HEAD_W ** -0.5).T.astype(BF16)
    kw = W // 2
    k = _rope(c[:, W:W + kw], cos[:, :kw], sm[:, :kw], sp[:, :kw], half).astype(BF16)
    vt = c[:, W + kw:W + 2 * kw].T.astype(BF16)
    for kvh in range(2):
        k_ref[0, kvh] = k[:, kvh * HEAD_W:(kvh + 1) * HEAD_W]
        v_ref[0, kvh] = vt[kvh * HEAD_W:(kvh + 1) * HEAD_W, :]


def _winattn_kernel(sink_ref, q_ref, k_ref, v_ref, o_ref, *, tq, n_ctx, T):
    i = pl.program_id(2)
    band = tq + 2 * WINDOW
    qt = jnp.concatenate([q_ref[0, 0:HEAD_W, :], q_ref[0, HEAD_W:2 * HEAD_W, :]], axis=1)
    ws = pl.multiple_of(jnp.clip(i * tq - WINDOW, n_ctx, T - band), WINDOW)
    s_ctx = jnp.dot(k_ref[0, 0, 0:n_ctx, :], qt, preferred_element_type=F32)
    s_band = jnp.dot(k_ref[0, 0, pl.ds(ws, band), :], qt, preferred_element_type=F32)
    kpos = ws + lax.broadcasted_iota(jnp.int32, (band, tq), 0)
    qpos = i * tq + lax.broadcasted_iota(jnp.int32, (band, tq), 1)
    valid = jnp.logical_and(jnp.abs(qpos - kpos) <= WINDOW, qpos >= n_ctx)
    s_band = jnp.where(jnp.concatenate([valid, valid], axis=1), s_band, NEG_INF)
    sink = sink_ref[...]
    m = jnp.maximum(jnp.maximum(jnp.max(s_ctx, axis=0, keepdims=True),
                                jnp.max(s_band, axis=0, keepdims=True)), sink)
    p_ctx = jnp.exp(s_ctx - m).astype(BF16)
    p_band = jnp.exp(s_band - m).astype(BF16)
    vx = jnp.concatenate([v_ref[0, 0, :, 0:n_ctx], jnp.ones((8, n_ctx), BF16)], axis=0)
    vb = jnp.concatenate([v_ref[0, 0, :, pl.ds(ws, band)], jnp.ones((8, band), BF16)], axis=0)
    acc = jnp.dot(jnp.concatenate([vx, vb], axis=1), jnp.concatenate([p_ctx, p_band], axis=0),
                  preferred_element_type=F32)
    o = acc[0:HEAD_W] / (acc[HEAD_W:HEAD_W + 1] + jnp.exp(sink - m))
    for g in range(2):
        o_ref[0, g * HEAD_W:(g + 1) * HEAD_W, :] = o[:, g * tq:(g + 1) * tq]


def _window_attention(q, k, v, sink_col, l, tq, n_ctx):
    B, KVH, T, _ = k.shape
    G = q.shape[1] // (KVH * HEAD_W)
    return pl.pallas_call(
        functools.partial(_winattn_kernel, tq=tq, n_ctx=n_ctx, T=T),
        grid=(B, KVH, T // tq),
        in_specs=[pl.BlockSpec((None, None, 1, G * tq), lambda b, h, i: (l, h, 0, 0)),
                  pl.BlockSpec((1, G * HEAD_W, tq), lambda b, h, i: (b, h, i)),
                  pl.BlockSpec((1, 1, T, HEAD_W), lambda b, h, i: (b, h, 0, 0)),
                  pl.BlockSpec((1, 1, HEAD_W, T), lambda b, h, i: (b, h, 0, 0))],
        out_specs=pl.BlockSpec((1, G * HEAD_W, tq), lambda b, h, i: (b, h, i)),
        out_shape=jax.ShapeDtypeStruct((B, KVH * G * HEAD_W, T), F32),
        compiler_params=_cparams(("parallel", "parallel", "arbitrary")),
        name="window_attention",
    )(sink_col, q, k, v)


def _group_norm(x, ones2, eps, center):
    inv = 1.0 / HEAD_W
    if center:
        x = x - _group_sum(x, ones2) * inv
    return x * lax.rsqrt(_group_sum(x * x, ones2) * inv + eps)


def _merge_kernel(x_ref, gates_ref, ya_ref, ybf_ref, ybb_ref, aux_ref, ocf_ref, ocb_ref, gc_ref, yd_ref,
                  lnx_g_ref, lnx_b_ref, gn_ref, wb_ref, wo_ref, pn_ref, mod_ref, ones_ref, o_ref, *,
                  tm, n_ctx):
    W = BRANCH_W
    D = x_ref.shape[-1]
    ones_bd = ones_ref[...]
    yb = _group_norm(ybf_ref[0] + ybb_ref[0], ones_bd, RW_GN_EPS, True)
    yb = yb * lnx_g_ref[...] + lnx_b_ref[...]
    yb = (yb + aux_ref[0, :, 0:W]) * aux_ref[0, :, W:2 * W]
    yc = _group_norm(ocf_ref[0] + ocb_ref[0], ones_bd, 1e-5, True) * gn_ref[...]
    gc = gc_ref[0]
    yc = yc * (gc * _sigmoid(gc))

    lifted = [_dot_tn(ya_ref[0].astype(BF16), wb_ref[0]),
              jnp.dot(yb.astype(BF16), wb_ref[1], preferred_element_type=F32),
              jnp.dot(yc.astype(BF16), wb_ref[2], preferred_element_type=F32),
              _dot_tn(yd_ref[0].astype(BF16), wb_ref[3])]
    m = None
    for n in range(4):
        t = _sigmoid(gates_ref[0, :, n * D:(n + 1) * D].astype(F32)) * lifted[n]
        m = t if m is None else m + t
    out = jnp.dot(m.astype(BF16), wo_ref[...], preferred_element_type=F32)
    o_ref[0] = x_ref[0] + _mod_row(mod_ref, 2, tm, n_ctx) * (_rms(out, 1e-6) * pn_ref[...])


def _merge(xs, gates, ya, yb, aux, oc, cols_c, yd, prm, mods, l, tm, n_ctx):
    B, T, D = xs.shape
    W = BRANCH_W
    vec = lambda n: pl.BlockSpec((None, 1, n), lambda b, i: (l, 0, 0))
    tok = lambda n: pl.BlockSpec((1, tm, n), lambda b, i: (b, i, 0))
    feat = pl.BlockSpec((1, W, tm), lambda b, i: (b, 0, i))
    return pl.pallas_call(
        functools.partial(_merge_kernel, tm=tm, n_ctx=n_ctx),
        grid=(B, T // tm),
        in_specs=[tok(D), tok(4 * D), feat, tok(W), tok(W), tok(2 * W), tok(W), tok(W),
                  pl.BlockSpec((1, tm, W), lambda b, i: (b, i, 2)),
                  feat,
                  vec(W), vec(W), vec(W),
                  _const_spec((None, 4, W, D), lambda b, i: (l, 0, 0, 0)),
                  _const_spec((None, D, D), lambda b, i: (l, 0, 0)),
                  vec(D), _mod_spec(l, D),
                  _const_spec((2 * W, W), lambda b, i: (0, 0))],
        out_specs=tok(D),
        out_shape=jax.ShapeDtypeStruct((B, T, D), F32),
        compiler_params=_cparams(("parallel", "parallel")),
        name="merge",
    )(xs, gates, ya, yb[0], yb[1], aux, oc[0], oc[1], cols_c, yd, prm["lnx_g"], prm["lnx_b"], prm["ret_gn"],
      prm["w_branch"], prm["w_out"], prm["norm_post_mix"], mods, prm["ones_bd"])


def _mlp_kernel(x_ref, g_ref, pn_ref, mod_ref, wu_ref, wd_ref, o_ref, *, tm, n_ctx):
    x = x_ref[0]
    h = _rms(x, 1e-6) * g_ref[...]
    hb = (h * (1.0 + _mod_row(mod_ref, 4, tm, n_ctx)) + _mod_row(mod_ref, 3, tm, n_ctx)).astype(BF16)
    F = wu_ref.shape[-1]
    y = None
    for f0 in range(0, F, F // MLP_SPLIT):
        sl = slice(f0, f0 + F // MLP_SPLIT)
        u = jnp.dot(hb, wu_ref[:, sl], preferred_element_type=F32)
        u = jnp.square(jnp.maximum(u, 0.0)).astype(BF16)
        t = jnp.dot(u, wd_ref[sl, :], preferred_element_type=F32)
        y = t if y is None else y + t
    o_ref[0] = x + _mod_row(mod_ref, 5, tm, n_ctx) * (_rms(y, 1e-6) * pn_ref[...])


def _mlp(xs, prm, mods, l, tm, n_ctx):
    B, T, D = xs.shape
    F = prm["w_up"].shape[-1]
    vec = lambda n: pl.BlockSpec((None, 1, n), lambda b, i: (l, 0, 0))
    return pl.pallas_call(
        functools.partial(_mlp_kernel, tm=tm, n_ctx=n_ctx),
        grid=(B, T // tm),
        in_specs=[pl.BlockSpec((1, tm, D), lambda b, i: (b, i, 0)),
                  vec(D), vec(D), _mod_spec(l, D),
                  _const_spec((None, D, F), lambda b, i: (l, 0, 0)),
                  _const_spec((None, F, D), lambda b, i: (l, 0, 0))],
        out_specs=pl.BlockSpec((1, tm, D), lambda b, i: (b, i, 0)),
        out_shape=jax.ShapeDtypeStruct((B, T, D), F32),
        compiler_params=_cparams(("parallel", "parallel")),
        name="mlp",
    )(xs, prm["norm_pre_mlp"], prm["norm_post_mlp"], mods, prm["w_up"], prm["w_down"])


def kernel(x, c, ctx, c_ctx, ada_w, ada_b, norm_pre_mix, norm_post_mix, norm_pre_mlp, norm_post_mlp,
           w_in, diff_lam_q, diff_lam_k, diff_subln, rwkv_mu, rwkv_w0, rwkv_w2, rwkv_a0, rwkv_a2,
           rwkv_g2, rwkv_kk, rwkv_ka, rwkv_rk, rwkv_lnx_g, rwkv_lnx_b, ret_decay, ret_gn, win_sink,
           w_branch, w_out, w_up, w_down):
    B, S, D = x.shape
    n_ctx = ctx.shape[1]
    T = n_ctx + S
    L = ada_w.shape[0]
    W = BRANCH_W
    TM = 256
    TD = next(t for t in DENSE_TILES if T % t == 0)
    assert n_ctx % TM == 0 and S % TM == 0 and S % GRID_W == 0 and B + 1 <= 8

    cc = jnp.zeros((8, D), F32).at[:B].set(c).at[B].set(c_ctx)
    mod = _ada_mod(cc, ada_w, ada_b).reshape(L, 8, 6, D)
    mod_ctx = jnp.broadcast_to(mod[:, B][:, None], (L, B, 6, D))
    mods = jnp.stack([mod_ctx, mod[:, :B]], axis=2)
    mods = jnp.pad(mods, ((0, 0), (0, 0), (0, 0), (0, 2), (0, 0)))

    lat = jnp.arange(S, dtype=jnp.int32)
    row = (lat // GRID_W).astype(F32)
    col = (lat % GRID_W).astype(F32)
    tabs_a = _rope_tables([row, col], W, HEAD_W // 2, n_ctx)
    tabs_d = _rope_tables([row, col], W, HEAD_W, n_ctx)
    tabs_c = _rope_tables([lat.astype(F32)], W // 2, HEAD_W // 2, n_ctx)

    sizes = (3 * W, rwkv_mu.shape[-1], 3 * W, 2 * W, 4 * D)
    offs = np.concatenate([[0], np.cumsum(sizes)])
    w_groups = [w_in[:, :, int(offs[n]):int(offs[n + 1])].astype(BF16) for n in range(5)]

    lw_w, la_w = rwkv_w2.shape[2], rwkv_a2.shape[2]
    zw = jnp.zeros((L, W, 2 * W), F32)
    w2cat = (zw.at[:, 0:lw_w, 0:W].set(rwkv_w2[:, 0])
             .at[:, lw_w:2 * lw_w, W:2 * W].set(rwkv_w2[:, 1]))
    a2cat = (zw.at[:, 2 * lw_w:2 * lw_w + la_w, 0:W].set(rwkv_a2[:, 0])
             .at[:, 2 * lw_w + la_w:2 * lw_w + 2 * la_w, W:2 * W].set(rwkv_a2[:, 1]))
    blk = np.arange(W) // HEAD_W
    ones_bd = (blk[:, None] == blk[None, :]).astype(np.float32)
    ones_bd = jnp.asarray(np.concatenate([ones_bd, ones_bd], axis=0), BF16)
    v3 = lambda a: a.reshape(L, 1, -1)

    def hl3(w):
        hi = w.astype(BF16)
        return jnp.concatenate([hi, (w - hi.astype(F32)).astype(BF16), hi], axis=-2)

    prm = dict(mu=v3(rwkv_mu), kk=v3(rwkv_kk), ka=v3(rwkv_ka), rk=v3(rwkv_rk), w0=rwkv_w0, a0=rwkv_a0,
               w2=hl3(w2cat), a2=hl3(a2cat), g2=hl3(rwkv_g2), ones_bd=ones_bd,
               lnx_g=v3(rwkv_lnx_g), lnx_b=v3(rwkv_lnx_b), ret_gn=v3(ret_gn),
               w_branch=w_branch.astype(BF16), w_out=w_out.astype(BF16),
               norm_post_mix=v3(norm_post_mix), norm_pre_mlp=v3(norm_pre_mlp),
               norm_post_mlp=v3(norm_post_mlp), w_up=w_up.astype(BF16), w_down=w_down.astype(BF16))
    subln = diff_subln.reshape(L, N_HEADS, HEAD_W, 1)
    decay = ret_decay.reshape(L, 2, 1, N_HEADS)
    TQ_D = 256
    sink_col = jnp.broadcast_to(win_sink.reshape(L, 2, 1, 2, 1), (L, 2, 1, 2, TQ_D)).reshape(L, 2, 1, 2 * TQ_D)
    gain_pre = v3(norm_pre_mix)

    xs = jnp.concatenate([ctx, x], axis=1)
    for l in range(L):
        qa, ka, va, cols_b, cols_c, qd, kd, vd, gates = _input_proj(
            xs, gain_pre, mods, w_groups, tabs_a, tabs_d, l, TD, n_ctx)
        ya = _diff_attention(qa, ka, va, diff_lam_q, diff_lam_k, subln, l, ATT_TQ, ATT_TK, ATT_UNROLL, n_ctx)
        rvk, dirs, aux = _rwkv_prep(cols_b, prm, l, TM, n_ctx)
        yb = _rwkv_scan(rvk, dirs, n_ctx)
        oc = _retention(cols_c, tabs_c, decay, l, n_ctx)
        yd = _window_attention(qd, kd, vd, sink_col, l, TQ_D, n_ctx)
        xs = _merge(xs, gates, ya, yb, aux, oc, cols_c, yd, prm, mods, l, TD, n_ctx)
        xs = _mlp(xs, prm, mods, l, TD, n_ctx)
    return xs[:, n_ctx:]
```

```python
import functools
import math

import numpy as np
import jax
import jax.numpy as jnp
from jax import lax
from jax.experimental import pallas as pl
from jax.experimental.pallas import tpu as pltpu

F32 = jnp.float32
BF16 = jnp.bfloat16
HI = lax.Precision.HIGHEST

GRID_W = 64
ROPE_BASE = 10000.0
NEG_INF = -1e30
WINDOW = 128
N_HEADS = 4
HEAD_W = 64
BRANCH_W = N_HEADS * HEAD_W
RW_CHUNK = 64
RW_SUB = 4
RET_CHUNK = 128
RW_GN_EPS = 64e-5
ATT_TQ, ATT_TK, ATT_UNROLL = 256, 256, 64
ATT_LOOKAHEAD, ATT_SLOTS = 2, 4
DENSE_TILES = (640, 512, 384, 256, 128)
MLP_SPLIT = 2
VMEM_LIMIT = 56 * 1024 * 1024


def _cparams(sem, vmem=None):
    return pltpu.CompilerParams(dimension_semantics=sem, vmem_limit_bytes=vmem or VMEM_LIMIT)


def _dotf(a, b):
    return jnp.dot(a, b, precision=HI, preferred_element_type=F32)


def _dotb(a, b):
    return jnp.dot(a.astype(BF16), b.astype(BF16), preferred_element_type=F32)


def _dot_nt(a, b, precision=None):
    return lax.dot_general(a, b, (((1,), (1,)), ((), ())), precision=precision,
                           preferred_element_type=F32)


def _dot_tn(a, b, precision=None):
    return lax.dot_general(a, b, (((0,), (0,)), ((), ())), precision=precision,
                           preferred_element_type=F32)


def _split(x):
    hi = x.astype(BF16)
    return hi, (x - hi.astype(F32)).astype(BF16)


def _dot_hl(x, w3):
    hi, lo = _split(x)
    return jnp.dot(jnp.concatenate([hi, hi, lo], axis=1), w3, preferred_element_type=F32)


def _group_sum(x, ones2):
    hi, lo = _split(x)
    return jnp.dot(jnp.concatenate([hi, lo], axis=1), ones2, preferred_element_type=F32)


def _sdot(a, b):
    return jnp.dot(a.astype(BF16), b.astype(BF16), preferred_element_type=F32)


def _sdot_nt(a, b):
    return _dot_nt(a.astype(BF16), b.astype(BF16))


def _sdot_tn(a, b):
    return _dot_tn(a.astype(BF16), b.astype(BF16))


def _sigmoid(x):
    return 1.0 / (1.0 + jnp.exp(-x))


def _softplus(x):
    return jnp.maximum(x, 0.0) + jnp.log1p(jnp.exp(-jnp.abs(x)))


def _rms(x, eps):
    return x * lax.rsqrt(jnp.mean(x * x, axis=-1, keepdims=True) + eps)


def _const_spec(shape, index):
    return pl.BlockSpec(shape, index, pipeline_mode=pl.Buffered(1))


def _ada_kernel(c_ref, w_ref, b_ref, o_ref):
    c = c_ref[...]
    o_ref[0] = _dotf(c * _sigmoid(c), w_ref[0]) + b_ref[0]


def _ada_mod(cc, ada_w, ada_b):
    L, D, N = ada_w.shape
    tn = N // 4
    return pl.pallas_call(
        _ada_kernel,
        grid=(L, N // tn),
        in_specs=[pl.BlockSpec((8, D), lambda l, n: (0, 0)),
                  pl.BlockSpec((1, D, tn), lambda l, n: (l, 0, n)),
                  pl.BlockSpec((1, 1, tn), lambda l, n: (l, 0, n))],
        out_specs=pl.BlockSpec((1, 8, tn), lambda l, n: (l, 0, n)),
        out_shape=jax.ShapeDtypeStruct((L, 8, N), F32),
        compiler_params=_cparams(("arbitrary", "arbitrary")),
        name="ada_mod",
    )(cc, ada_w, ada_b.reshape(L, 1, N))


def _mod_row(mod_ref, j, tm, n_ctx):
    tok = pl.program_id(1) * tm + lax.broadcasted_iota(jnp.int32, (tm, 1), 0)
    return jnp.where(tok < n_ctx, mod_ref[0, j:j + 1, :], mod_ref[1, j:j + 1, :])


def _mod_spec(l, D):
    return pl.BlockSpec((None, None, 2, 8, D), lambda b, i: (l, b, 0, 0, 0))


def _proj_kernel(x_ref, g_ref, mod_ref, wa_ref, wb_ref, wc_ref, wd_ref, wg_ref,
                 cosa_ref, sma_ref, spa_ref, cosd_ref, smd_ref, spd_ref,
                 qa_ref, ka_ref, va_ref, cb_ref, cc_ref, qd_ref, kd_ref, vd_ref, gates_ref, *,
                 tm, n_ctx, half_a, half_d):
    h = _rms(x_ref[0], 1e-6) * g_ref[...]
    h = h * (1.0 + _mod_row(mod_ref, 1, tm, n_ctx)) + _mod_row(mod_ref, 0, tm, n_ctx)
    hb = h.astype(BF16)
    proj = lambda w_ref: jnp.dot(hb, w_ref[...], preferred_element_type=F32)
    _store_diff_qkv(proj(wa_ref), cosa_ref[...], sma_ref[...], spa_ref[...], half_a, qa_ref, ka_ref, va_ref)
    cb_ref[0] = proj(wb_ref)
    cc_ref[0] = proj(wc_ref)
    _store_win_qkv(proj(wd_ref), cosd_ref[...], smd_ref[...], spd_ref[...], half_d, qd_ref, kd_ref, vd_ref)
    gates_ref[0] = proj(wg_ref).astype(gates_ref.dtype)


def _input_proj(xs, gain, mods, weights, tabs_a, tabs_d, l, tm, n_ctx):
    B, T, D = xs.shape
    W = BRANCH_W
    tok = lambda n: pl.BlockSpec((1, tm, n), lambda b, i: (b, i, 0))
    tab = pl.BlockSpec((tm, W), lambda b, i: (i, 0))
    in_specs = [tok(D), pl.BlockSpec((None, 1, D), lambda b, i: (l, 0, 0)), _mod_spec(l, D)]
    in_specs += [_const_spec((None, D, w.shape[-1]), lambda b, i: (l, 0, 0)) for w in weights]
    in_specs += [tab] * 6
    heads = lambda *lead: pl.BlockSpec((1,) + lead + (tm, HEAD_W), lambda b, i: (b,) + (0,) * len(lead) + (i, 0))
    heads_t = lambda n: pl.BlockSpec((1, n, HEAD_W, tm), lambda b, i: (b, 0, 0, i))
    qa_spec = pl.BlockSpec((1, N_HEADS, 2, HEAD_W, tm), lambda b, i: (b, 0, 0, 0, i))
    out_specs = [qa_spec, heads(N_HEADS), heads_t(N_HEADS), tok(weights[1].shape[-1]),
                 tok(weights[2].shape[-1]), pl.BlockSpec((1, W, tm), lambda b, i: (b, 0, i)),
                 heads(2), heads_t(2), tok(weights[4].shape[-1])]
    sds = jax.ShapeDtypeStruct
    out_shape = [sds((B, N_HEADS, 2, HEAD_W, T), BF16), sds((B, N_HEADS, T, HEAD_W), BF16),
                 sds((B, N_HEADS, HEAD_W, T), BF16), sds((B, T, weights[1].shape[-1]), F32),
                 sds((B, T, weights[2].shape[-1]), F32), sds((B, W, T), BF16),
                 sds((B, 2, T, HEAD_W), BF16), sds((B, 2, HEAD_W, T), BF16),
                 sds((B, T, weights[4].shape[-1]), BF16)]
    return pl.pallas_call(
        functools.partial(_proj_kernel, tm=tm, n_ctx=n_ctx, half_a=tabs_a[3], half_d=tabs_d[3]),
        grid=(B, T // tm),
        in_specs=in_specs, out_specs=out_specs, out_shape=out_shape,
        compiler_params=_cparams(("parallel", "parallel")),
        name="input_proj",
    )(xs, gain, mods, *weights, *tabs_a[:3], *tabs_d[:3])


def _rope_tables(pos_sets, lanes, group, n_ctx):
    n_sets = len(pos_sets)
    sub = group // n_sets
    half = sub // 2
    angs = []
    for pos in pos_sets:
        inv_freq = ROPE_BASE ** (-jnp.arange(half, dtype=F32) / half)
        ang = pos[:, None] * inv_freq[None, :]
        angs.append(jnp.concatenate([ang, ang], axis=-1))
    ang = jnp.concatenate(angs, axis=-1)
    ang = jnp.tile(ang, (1, lanes // group))
    ang = jnp.concatenate([jnp.zeros((n_ctx, lanes), F32), ang], axis=0)
    first = (np.arange(lanes) % sub) < half
    cos, sin = jnp.cos(ang), jnp.sin(ang)
    sin_minus = jnp.where(first[None, :], -sin, 0.0)
    sin_plus = jnp.where(first[None, :], 0.0, sin)
    return cos, sin_minus, sin_plus, half


def _rope(x, cos, sin_minus, sin_plus, half):
    n = x.shape[-1]
    return x * cos + pltpu.roll(x, n - half, 1) * sin_minus + pltpu.roll(x, half, 1) * sin_plus


def _store_diff_qkv(c, cos, sm, sp, half, q_ref, k_ref, v_ref):
    scale = (HEAD_W // 2) ** -0.5 * math.log2(math.e)
    q = _rope(c[:, 0:BRANCH_W], cos, sm, sp, half) * scale
    k = _rope(c[:, BRANCH_W:2 * BRANCH_W], cos, sm, sp, half)
    v = c[:, 2 * BRANCH_W:3 * BRANCH_W]
    lane = lax.broadcasted_iota(jnp.int32, q.shape, 1)
    comp = (lane >> 5) & 1
    q0 = jnp.where(comp == 0, q, 0.0).T.astype(BF16)
    q1 = jnp.where(comp == 1, q, 0.0).T.astype(BF16)
    kb, vt = k.astype(BF16), v.T.astype(BF16)
    for h in range(N_HEADS):
        sl = slice(h * HEAD_W, (h + 1) * HEAD_W)
        q_ref[0, h, 0] = q0[sl, :]
        q_ref[0, h, 1] = q1[sl, :]
        k_ref[0, h] = kb[:, sl]
        v_ref[0, h] = vt[sl, :]


def _flash_kernel(lq_ref, lk_ref, g_ref, q_ref, k_ref, vt_ref, o_ref, s_sc, acc_sc, *,
                  tq, tk, unroll, n_ctx, T, lam_init):
    qi = pl.program_id(2)
    qt = jnp.concatenate([q_ref[0, 0, 0], q_ref[0, 0, 1]], axis=1)

    def scores(slot, off, size):
        s_sc[slot, 0:size] = jnp.dot(k_ref[0, 0, pl.ds(off, size), :], qt, preferred_element_type=F32)

    def absorb(slot, off, size, m):
        s = s_sc[slot, 0:size]
        m_new = jnp.maximum(m, jnp.max(s, axis=0, keepdims=True))
        p = jnp.exp2(s - m_new).astype(BF16)
        vt = jnp.concatenate([vt_ref[0, 0, :, pl.ds(off, size)], jnp.ones((8, size), BF16)], axis=0)
        acc_sc[...] = acc_sc[...] * jnp.exp2(m - m_new) + jnp.dot(vt, p, preferred_element_type=F32)
        return m_new

    n_slots, look = s_sc.shape[0], ATT_LOOKAHEAD
    lat_off = lambda c: n_ctx + (c - 1) * tk
    acc_sc[...] = jnp.zeros(acc_sc.shape, F32)
    m = jnp.full((1, 2 * tq), NEG_INF, F32)
    scores(0, 0, n_ctx)
    for c in range(1, look + 1):
        scores(c % n_slots, min(lat_off(c), T - tk), tk)
    m = absorb(0, 0, n_ctx, m)

    def body(jj, m):
        c0 = 1 + jj * unroll
        for u in range(unroll):
            off = pl.multiple_of(lat_off(c0 + u), math.gcd(n_ctx, tk))
            nxt = pl.multiple_of(jnp.minimum(lat_off(c0 + u + look), T - tk), math.gcd(n_ctx, tk))
            scores((1 + u + look) % n_slots, nxt, tk)
            m = absorb((1 + u) % n_slots, off, tk, m)
        return m

    n_trips = (T - n_ctx) // (unroll * tk)
    lax.fori_loop(0, jnp.where(qi * tq < n_ctx, 0, n_trips), body, m)
    acc = acc_sc[...]
    o = acc[0:HEAD_W] / acc[HEAD_W:HEAD_W + 1]
    e0 = jnp.exp(jnp.sum(lq_ref[0:1, :] * lk_ref[0:1, :], axis=1, keepdims=True))
    e1 = jnp.exp(jnp.sum(lq_ref[1:2, :] * lk_ref[1:2, :], axis=1, keepdims=True))
    lam = e0 - e1 + lam_init
    y = o[:, :tq] - lam * o[:, tq:]
    o_ref[0] = y * lax.rsqrt(jnp.mean(y * y, axis=0, keepdims=True) + 1e-5) * g_ref[...] * (1.0 - lam_init)


def _diff_attention(q, k, v, lam_q, lam_k, subln, l, tq, tk, unroll, n_ctx):
    B, H, _, _, T = q.shape
    dqk = lam_q.shape[-1]
    lam_init = 0.8 - 0.6 * math.exp(-0.3 * l)
    unroll = math.gcd(unroll, (T - n_ctx) // tk)
    assert unroll % ATT_SLOTS == 0 and ATT_LOOKAHEAD < ATT_SLOTS
    assert (T - n_ctx) % (unroll * tk) == 0 and n_ctx % tq == 0 and n_ctx % 128 == 0 and tk % 128 == 0
    kern = functools.partial(_flash_kernel, tq=tq, tk=tk, unroll=unroll, n_ctx=n_ctx, T=T,
                             lam_init=lam_init)
    return pl.pallas_call(
        kern,
        grid=(B, H, T // tq),
        in_specs=[pl.BlockSpec((None, 2, dqk), lambda b, h, i: (l, 0, 0)),
                  pl.BlockSpec((None, 2, dqk), lambda b, h, i: (l, 0, 0)),
                  pl.BlockSpec((None, None, HEAD_W, 1), lambda b, h, i: (l, h, 0, 0)),
                  pl.BlockSpec((1, 1, 2, HEAD_W, tq), lambda b, h, i: (b, h, 0, 0, i)),
                  pl.BlockSpec((1, 1, T, HEAD_W), lambda b, h, i: (b, h, 0, 0)),
                  pl.BlockSpec((1, 1, HEAD_W, T), lambda b, h, i: (b, h, 0, 0))],
        out_specs=pl.BlockSpec((1, HEAD_W, tq), lambda b, h, i: (b, h, i)),
        out_shape=jax.ShapeDtypeStruct((B, H * HEAD_W, T), F32),
        scratch_shapes=[pltpu.VMEM((ATT_SLOTS, max(tk, n_ctx), 2 * tq), F32),
                        pltpu.VMEM((HEAD_W + 8, 2 * tq), F32)],
        compiler_params=_cparams(("parallel", "parallel", "arbitrary")),
        name="diff_attention",
    )(lam_q, lam_k, subln, q, k, v)


def _rwprep_kernel(c_ref, p_ref, n_ref, mu_ref, kk_ref, ka_ref, rk_ref, w0_ref, a0_ref,
                   w2_ref, a2_ref, g2_ref, ones_ref, rvk_ref, dir_ref, aux_ref, *, tm, n_ctx, T):
    i = pl.program_id(1)
    c = c_ref[0]
    t0 = i * tm
    seg_start = jnp.logical_or(t0 == 0, t0 == n_ctx)
    seg_end = jnp.logical_or(t0 + tm == n_ctx, t0 + tm == T)
    pv = jnp.where(seg_start, 0.0, p_ref[0][7:8, :])
    nx = jnp.where(seg_end, 0.0, n_ref[0][0:1, :])
    row = lax.broadcasted_iota(jnp.int32, c.shape, 0)
    prev = jnp.where(row == 0, pv, pltpu.roll(c, 1, 0))
    nxt = jnp.where(row == tm - 1, nx, pltpu.roll(c, tm - 1, 0))
    xs = c + (0.5 * (prev + nxt) - c) * mu_ref[...]
    W = BRANCH_W
    r, k, v = xs[:, 0:W], xs[:, W:2 * W], xs[:, 2 * W:3 * W]
    lo, gl = xs[:, 3 * W:4 * W], xs[:, 4 * W:]
    ones2 = ones_ref[...]
    kk = k * kk_ref[...]
    kk = kk * lax.rsqrt(jnp.maximum(_group_sum(kk * kk, ones2), 1e-12))
    wpre = _dot_hl(jnp.tanh(lo), w2_ref[...])
    apre = _dot_hl(lo, a2_ref[...])
    gate = _dot_hl(_sigmoid(gl), g2_ref[...])
    rb = jnp.zeros_like(r)
    for d in range(2):
        z = w0_ref[d:d + 1, :] + wpre[:, d * W:(d + 1) * W]
        w = -_softplus(-z) - 0.5
        a = _sigmoid(a0_ref[d:d + 1, :] + apre[:, d * W:(d + 1) * W])
        kd = k * (1.0 + (a - 1.0) * ka_ref[...])
        dir_ref[0, d, :, 0:W] = -jnp.exp(w)
        dir_ref[0, d, :, W:2 * W] = kd
        dir_ref[0, d, :, 2 * W:3 * W] = kk * a
        rb = rb + r * kd * rk_ref[...]
    rvk_ref[0, :, 0:W] = r
    rvk_ref[0, :, W:2 * W] = v
    rvk_ref[0, :, 2 * W:3 * W] = kk
    aux_ref[0, :, 0:W] = _group_sum(rb, ones2) * v
    aux_ref[0, :, W:2 * W] = gate


def _rwkv_prep(cols, prm, l, tm, n_ctx):
    B, T, NB = cols.shape
    W = BRANCH_W
    nblk8 = T // 8
    r8 = tm // 8
    vec = lambda n: pl.BlockSpec((None, 1, n), lambda b, i: (l, 0, 0))
    pair = lambda n: pl.BlockSpec((None, 2, n), lambda b, i: (l, 0, 0))
    mat = lambda m, n: _const_spec((None, m, n), lambda b, i: (l, 0, 0))
    return pl.pallas_call(
        functools.partial(_rwprep_kernel, tm=tm, n_ctx=n_ctx, T=T),
        grid=(B, T // tm),
        in_specs=[pl.BlockSpec((1, tm, NB), lambda b, i: (b, i, 0)),
                  pl.BlockSpec((1, 8, NB), lambda b, i: (b, jnp.maximum(i * r8 - 1, 0), 0)),
                  pl.BlockSpec((1, 8, NB), lambda b, i: (b, jnp.minimum((i + 1) * r8, nblk8 - 1), 0)),
                  vec(NB), vec(W), vec(W), vec(W), pair(W), pair(W),
                  mat(3 * W, 2 * W), mat(3 * W, 2 * W), mat(3 * (NB - 4 * W), W),
                  _const_spec((2 * W, W), lambda b, i: (0, 0))],
        out_specs=[pl.BlockSpec((1, tm, 3 * W), lambda b, i: (b, i, 0)),
                   pl.BlockSpec((1, 2, tm, 3 * W), lambda b, i: (b, 0, i, 0)),
                   pl.BlockSpec((1, tm, 2 * W), lambda b, i: (b, i, 0))],
        out_shape=[jax.ShapeDtypeStruct((B, T, 3 * W), F32),
                   jax.ShapeDtypeStruct((B, 2, T, 3 * W), F32),
                   jax.ShapeDtypeStruct((B, T, 2 * W), F32)],
        compiler_params=_cparams(("parallel", "parallel")),
        name="rwkv_prep",
    )(cols, cols, cols, prm["mu"], prm["kk"], prm["ka"], prm["rk"], prm["w0"], prm["a0"],
      prm["w2"], prm["a2"], prm["g2"], prm["ones_bd"])


def _rw_chunks(chunks, rounds, state_of, states):
    C, W = RW_CHUNK, BRANCH_W
    each = lambda f, *cols: [f(*a) for a in zip(*cols)]
    r, v, kk, lw, kd, b, sgn = (list(col) for col in zip(*chunks))

    ti = lax.broadcasted_iota(jnp.int32, (C, C), 0)
    tj = lax.broadcasted_iota(jnp.int32, (C, C), 1)
    ri = lax.broadcasted_iota(jnp.int32, (W, W), 0)
    ci = lax.broadcasted_iota(jnp.int32, (W, W), 1)
    same_head = (ri >> 6) == (ci >> 6)
    bd = lambda x: jnp.where(same_head, jnp.concatenate([x] * N_HEADS, axis=0), 0.0)
    row = lax.broadcasted_iota(jnp.int32, (C, W), 0)
    col = lax.broadcasted_iota(jnp.int32, (C, W), 1) & (C - 1)
    masks = {sg: (jnp.where((tj - ti) * sg <= 0, 1.0, 0.0), (col - row) * sg < 0, (col - row) * sg <= 0)
             for sg in set(sgn)}
    tri = [masks[sg][0] for sg in sgn]
    strict = [masks[sg][1] for sg in sgn]
    incl = [masks[sg][2] for sg in sgn]
    blk16 = (row >> 4) == (col >> 4)
    blk32 = (row >> 5) == (col >> 5)
    eye = jnp.where(row == col, 1.0, 0.0)

    log_g = each(_dotf, tri, lw)
    log_gc = [jnp.sum(x, axis=0, keepdims=True) for x in lw]
    kk_t = each(lambda kk_, lg, lw_: kk_ * jnp.exp(lg - lw_), kk, log_g, lw)
    r_t = each(lambda r_, lg: r_ * jnp.exp(lg), r, log_g)
    g_inv = [jnp.exp(-lg) for lg in log_g]
    g_tail = each(lambda lgc, lg: jnp.exp(lgc - lg), log_gc, log_g)

    bdb = lambda x: bd(x.astype(BF16))
    left = each(lambda a, c: jnp.concatenate([a, c], axis=0).astype(BF16), kk_t, r_t)
    right = each(lambda b_, kd_, gi: jnp.concatenate([bdb(b_ * gi), bdb(kd_ * gi)], axis=0), b, kd, g_inv)
    g = each(_dot_nt, left, right)
    m_sl = each(lambda st, g_: jnp.where(st, g_[:C, :W], 0.0), strict, g)
    n_sl = each(lambda st, g_: jnp.where(st, g_[:C, W:], 0.0), strict, g)
    pq_l = each(lambda ic, g_: jnp.concatenate([jnp.where(ic, g_[C:, :W], 0.0),
                                                jnp.where(ic, g_[C:, W:], 0.0)], axis=1).astype(BF16), incl, g)

    v_bd = [bdb(x) for x in v]
    nv = each(_sdot, n_sl, v_bd)
    b_tail = each(lambda b_, kd_, gt: jnp.concatenate([b_ * gt, kd_ * gt], axis=0).astype(BF16), b, kd, g_tail)
    gc_col = [jnp.sum(jnp.where(ri == ci, jnp.exp(lgc), 0.0), axis=1, keepdims=True) for lgc in log_gc]

    d0 = [jnp.where(blk16, m_, 0.0) for m_ in m_sl]
    sq = lambda x: _sdot(x, bdb(x))
    a2 = each(sq, d0)
    a4 = each(sq, a2)
    a8 = each(sq, a4)
    t = [eye - d_ for d_ in d0]
    for a_ in (a2, a4, a8):
        t = each(lambda t_, x: t_ + _sdot(t_, bdb(x)), t, a_)
    for cm in (jnp.logical_and(blk32, jnp.logical_not(blk16)), jnp.logical_not(blk32)):
        u = each(lambda m_, t_: _sdot(jnp.where(cm, m_, 0.0), bdb(t_)), m_sl, t)
        t = each(lambda t_, u_: t_ - _sdot(t_, bdb(u_)), t, u)

    states = list(states)
    ys = [None] * len(chunks)
    for ids in rounds:
        pick = lambda xs: [xs[i] for i in ids]
        s0 = [states[state_of[i]] for i in ids]
        g2 = each(_sdot, pick(left), s0)
        rhs = each(lambda g2_, nv_: -(g2_[:C] + nv_), g2, pick(nv))
        sa = each(lambda t_, rh: _sdot(t_, bdb(rh)), pick(t), rhs)
        y = each(lambda g2_, pq, sa_, vb: g2_[C:] + _sdot(pq, jnp.concatenate([bdb(sa_), vb], axis=0)),
                 g2, pick(pq_l), sa, pick(v_bd))
        ds = each(lambda bt, sa_, v_: _sdot_tn(bt, jnp.concatenate([sa_, v_], axis=0)),
                  pick(b_tail), sa, pick(v))
        for i, y_, s_, ds_ in zip(ids, y, s0, ds):
            ys[i] = y_
            states[state_of[i]] = jnp.where(same_head, s_ * gc_col[i] + ds_, 0.0)
    return ys, states


def _rwscan_kernel(rvf_ref, rvb_ref, df_ref, db_ref, yf_ref, yb_ref, s_ref, *, n_sub):
    C, W = RW_CHUNK, BRANCH_W

    @pl.when(pl.program_id(0) == 0)
    def _():
        s_ref[...] = jnp.zeros(s_ref.shape, F32)

    refs = ((rvf_ref, df_ref, yf_ref), (rvb_ref, db_ref, yb_ref))
    chunks, where, state_of = [], [], []
    rounds = [[] for _ in range(n_sub)]
    for bi in range(rvf_ref.shape[0]):
        for d, (rv, dr, _) in enumerate(refs):
            for k in range(n_sub):
                rows = pl.ds((k if d == 0 else n_sub - 1 - k) * C, C)
                rounds[k].append(len(chunks))
                state_of.append(2 * bi + d)
                where.append((d, bi, rows))
                chunks.append((rv[bi, rows, 0:W], rv[bi, rows, W:2 * W], rv[bi, rows, 2 * W:3 * W],
                               dr[bi, 0, rows, 0:W], dr[bi, 0, rows, W:2 * W], dr[bi, 0, rows, 2 * W:3 * W],
                               1 - 2 * d))
    n_states = s_ref.shape[0]
    ys, states = _rw_chunks(chunks, rounds, state_of, [s_ref[i] for i in range(n_states)])
    for (d, bi, rows), y in zip(where, ys):
        refs[d][2][bi, rows, :] = y
    for i in range(n_states):
        s_ref[i] = states[i]


def _chunk_order(s, d, n_ctx_chunks, n_chunks):
    bwd = jnp.where(s < n_ctx_chunks, n_ctx_chunks - 1 - s, n_chunks - 1 - (s - n_ctx_chunks))
    return jnp.where(d == 0, s, bwd)


def _rwkv_scan(rvk, dirs, n_ctx):
    B, T, _ = rvk.shape
    W = BRANCH_W
    C = RW_CHUNK * RW_SUB
    assert T % C == 0 and n_ctx % C == 0
    nc, ncx = T // C, n_ctx // C
    ch = lambda s, d: _chunk_order(s, d, ncx, nc)
    rv_spec = lambda d: pl.BlockSpec((B, C, 3 * W), lambda s: (0, ch(s, d), 0))
    dir_spec = lambda d: pl.BlockSpec((B, 1, C, 3 * W), lambda s: (0, d, ch(s, d), 0))
    y_spec = lambda d: pl.BlockSpec((B, C, W), lambda s: (0, ch(s, d), 0))
    return pl.pallas_call(
        functools.partial(_rwscan_kernel, n_sub=RW_SUB),
        grid=(nc,),
        in_specs=[rv_spec(0), rv_spec(1), dir_spec(0), dir_spec(1)],
        out_specs=[y_spec(0), y_spec(1)],
        out_shape=[jax.ShapeDtypeStruct((B, T, W), F32)] * 2,
        scratch_shapes=[pltpu.VMEM((2 * B, W, W), F32)],
        compiler_params=_cparams(("arbitrary",)),
        name="rwkv_scan",
    )(rvk, rvk, dirs, dirs)


def _ret_chunk(c, cos, sm, sp, dec, s0, d, half, scale):
    C, W = RET_CHUNK, BRANCH_W
    QW = W // 2
    q = _rope(c[:, 0:QW], cos, sm, sp, half)
    k = _rope(c[:, QW:2 * QW], cos, sm, sp, half) * scale
    v = c[:, 2 * QW:2 * QW + W]
    log_g = -_softplus(-dec)
    lane_q = lax.broadcasted_iota(jnp.int32, (1, QW), 1) >> 5
    lane_v = lax.broadcasted_iota(jnp.int32, (1, W), 1) >> 6
    lg_q = jnp.zeros((1, QW), F32)
    lg_v = jnp.zeros((1, W), F32)
    for h in range(N_HEADS):
        lg_q = jnp.where(lane_q == h, log_g[:, h:h + 1], lg_q)
        lg_v = jnp.where(lane_v == h, log_g[:, h:h + 1], lg_v)
    idx = lax.broadcasted_iota(jnp.int32, (C, 1), 0).astype(F32)
    q_dec = jnp.exp((idx + 1.0 if d == 0 else C - idx) * lg_q)
    k_dec = jnp.exp((C - 1.0 - idx if d == 0 else idx) * lg_q)
    ti = lax.broadcasted_iota(jnp.int32, (C, C), 0)
    tj = lax.broadcasted_iota(jnp.int32, (C, C), 1)
    rel = ((ti - tj) * (1 - 2 * d)).astype(F32)

    o = _dotb(q * q_dec, s0)
    kb = k.astype(BF16)
    vb = v.astype(BF16)
    for h in range(N_HEADS):
        dm = jnp.where(rel >= 0, jnp.exp(jnp.maximum(rel, 0.0) * log_g[:, h:h + 1]), 0.0)
        qh = jnp.where(lane_q == h, q, 0.0).astype(BF16)
        sc = _dot_nt(qh, kb) * dm
        o = o + jnp.where(lane_v == h, jnp.dot(sc.astype(BF16), vb, preferred_element_type=F32), 0.0)

    ri = lax.broadcasted_iota(jnp.int32, (QW, W), 0) >> 5
    ci = lax.broadcasted_iota(jnp.int32, (QW, W), 1) >> 6
    upd = _dot_tn((k * k_dec).astype(BF16), vb)
    return o, jnp.where(ri == ci, s0 * jnp.exp(C * lg_v) + upd, 0.0)


def _ret_kernel(cf_ref, cb_ref, cosf_ref, smf_ref, spf_ref, cosb_ref, smb_ref, spb_ref, dec_ref,
                of_ref, ob_ref, s_ref, *, half, scale):
    @pl.when(pl.program_id(0) == 0)
    def _():
        s_ref[...] = jnp.zeros(s_ref.shape, F32)

    dirs = ((cf_ref, (cosf_ref, smf_ref, spf_ref), of_ref), (cb_ref, (cosb_ref, smb_ref, spb_ref), ob_ref))
    for bi in range(cf_ref.shape[0]):
        for d, (c_ref, tabs, o_ref) in enumerate(dirs):
            o, s_new = _ret_chunk(c_ref[bi], tabs[0][...], tabs[1][...], tabs[2][...], dec_ref[d],
                                  s_ref[2 * bi + d], d, half, scale)
            o_ref[bi] = o
            s_ref[2 * bi + d] = s_new


def _retention(cols, tabs, decay, l, n_ctx):
    B, T, NCOL = cols.shape
    C, W = RET_CHUNK, BRANCH_W
    cos, sm, sp, half = tabs
    nc, ncx = T // C, n_ctx // C
    ch = lambda s, d: _chunk_order(s, d, ncx, nc)
    c_spec = lambda d: pl.BlockSpec((B, C, NCOL), lambda s: (0, ch(s, d), 0))
    tab_spec = lambda d: pl.BlockSpec((C, W // 2), lambda s: (ch(s, d), 0))
    o_spec = lambda d: pl.BlockSpec((B, C, W), lambda s: (0, ch(s, d), 0))
    return pl.pallas_call(
        functools.partial(_ret_kernel, half=half, scale=(HEAD_W // 2) ** -0.5),
        grid=(nc,),
        in_specs=[c_spec(0), c_spec(1), tab_spec(0), tab_spec(0), tab_spec(0),
                  tab_spec(1), tab_spec(1), tab_spec(1),
                  pl.BlockSpec((None, 2, 1, N_HEADS), lambda s: (l, 0, 0, 0))],
        out_specs=[o_spec(0), o_spec(1)],
        out_shape=[jax.ShapeDtypeStruct((B, T, W), F32)] * 2,
        scratch_shapes=[pltpu.VMEM((2 * B, W // 2, W), F32)],
        compiler_params=_cparams(("arbitrary",)),
        name="retention",
    )(cols, cols, cos, sm, sp, cos, sm, sp, decay)


def _store_win_qkv(c, cos, sm, sp, half, q_ref, k_ref, v_ref):
    W = BRANCH_W
    q_ref[0] = (_rope(c[:, 0:W], cos, sm, sp, half) * HEAD_W ** -0.5).T.astype(BF16)
    kw = W // 2
    k = _rope(c[:, W:W + kw], cos[:, :kw], sm[:, :kw], sp[:, :kw], half).astype(BF16)
    vt = c[:, W + kw:W + 2 * kw].T.astype(BF16)
    for kvh in range(2):
        k_ref[0, kvh] = k[:, kvh * HEAD_W:(kvh + 1) * HEAD_W]
        v_ref[0, kvh] = vt[kvh * HEAD_W:(kvh + 1) * HEAD_W, :]


def _winattn_kernel(sink_ref, q_ref, k_ref, v_ref, o_ref, *, tq, n_ctx, T):
    i = pl.program_id(2)
    band = tq + 2 * WINDOW
    qt = jnp.concatenate([q_ref[0, 0:HEAD_W, :], q_ref[0, HEAD_W:2 * HEAD_W, :]], axis=1)
    ws = pl.multiple_of(jnp.clip(i * tq - WINDOW, n_ctx, T - band), WINDOW)
    s_ctx = jnp.dot(k_ref[0, 0, 0:n_ctx, :], qt, preferred_element_type=F32)
    s_band = jnp.dot(k_ref[0, 0, pl.ds(ws, band), :], qt, preferred_element_type=F32)
    kpos = ws + lax.broadcasted_iota(jnp.int32, (band, tq), 0)
    qpos = i * tq + lax.broadcasted_iota(jnp.int32, (band, tq), 1)
    valid = jnp.logical_and(jnp.abs(qpos - kpos) <= WINDOW, qpos >= n_ctx)
    s_band = jnp.where(jnp.concatenate([valid, valid], axis=1), s_band, NEG_INF)
    sink = sink_ref[...]
    m = jnp.maximum(jnp.maximum(jnp.max(s_ctx, axis=0, keepdims=True),
                                jnp.max(s_band, axis=0, keepdims=True)), sink)
    p_ctx = jnp.exp(s_ctx - m).astype(BF16)
    p_band = jnp.exp(s_band - m).astype(BF16)
    vx = jnp.concatenate([v_ref[0, 0, :, 0:n_ctx], jnp.ones((8, n_ctx), BF16)], axis=0)
    vb = jnp.concatenate([v_ref[0, 0, :, pl.ds(ws, band)], jnp.ones((8, band), BF16)], axis=0)
    acc = jnp.dot(jnp.concatenate([vx, vb], axis=1), jnp.concatenate([p_ctx, p_band], axis=0),
                  preferred_element_type=F32)
    o = acc[0:HEAD_W] / (acc[HEAD_W:HEAD_W + 1] + jnp.exp(sink - m))
    for g in range(2):
        o_ref[0, g * HEAD_W:(g + 1) * HEAD_W, :] = o[:, g * tq:(g + 1) * tq]


def _window_attention(q, k, v, sink_col, l, tq, n_ctx):
    B, KVH, T, _ = k.shape
    G = q.shape[1] // (KVH * HEAD_W)
    return pl.pallas_call(
        functools.partial(_winattn_kernel, tq=tq, n_ctx=n_ctx, T=T),
        grid=(B, KVH, T // tq),
        in_specs=[pl.BlockSpec((None, None, 1, G * tq), lambda b, h, i: (l, h, 0, 0)),
                  pl.BlockSpec((1, G * HEAD_W, tq), lambda b, h, i: (b, h, i)),
                  pl.BlockSpec((1, 1, T, HEAD_W), lambda b, h, i: (b, h, 0, 0)),
                  pl.BlockSpec((1, 1, HEAD_W, T), lambda b, h, i: (b, h, 0, 0))],
        out_specs=pl.BlockSpec((1, G * HEAD_W, tq), lambda b, h, i: (b, h, i)),
        out_shape=jax.ShapeDtypeStruct((B, KVH * G * HEAD_W, T), F32),
        compiler_params=_cparams(("parallel", "parallel", "arbitrary")),
        name="window_attention",
    )(sink_col, q, k, v)


def _group_norm(x, ones2, eps, center):
    inv = 1.0 / HEAD_W
    if center:
        x = x - _group_sum(x, ones2) * inv
    return x * lax.rsqrt(_group_sum(x * x, ones2) * inv + eps)


def _merge_kernel(x_ref, gates_ref, ya_ref, ybf_ref, ybb_ref, aux_ref, ocf_ref, ocb_ref, gc_ref, yd_ref,
                  lnx_g_ref, lnx_b_ref, gn_ref, wb_ref, wo_ref, pn_ref, mod_ref, ones_ref, o_ref, *,
                  tm, n_ctx):
    W = BRANCH_W
    D = x_ref.shape[-1]
    ones_bd = ones_ref[...]
    yb = _group_norm(ybf_ref[0] + ybb_ref[0], ones_bd, RW_GN_EPS, True)
    yb = yb * lnx_g_ref[...] + lnx_b_ref[...]
    yb = (yb + aux_ref[0, :, 0:W]) * aux_ref[0, :, W:2 * W]
    yc = _group_norm(ocf_ref[0] + ocb_ref[0], ones_bd, 1e-5, True) * gn_ref[...]
    gc = gc_ref[0]
    yc = yc * (gc * _sigmoid(gc))

    lifted = [_dot_tn(ya_ref[0].astype(BF16), wb_ref[0]),
              jnp.dot(yb.astype(BF16), wb_ref[1], preferred_element_type=F32),
              jnp.dot(yc.astype(BF16), wb_ref[2], preferred_element_type=F32),
              _dot_tn(yd_ref[0].astype(BF16), wb_ref[3])]
    m = None
    for n in range(4):
        t = _sigmoid(gates_ref[0, :, n * D:(n + 1) * D].astype(F32)) * lifted[n]
        m = t if m is None else m + t
    out = jnp.dot(m.astype(BF16), wo_ref[...], preferred_element_type=F32)
    o_ref[0] = x_ref[0] + _mod_row(mod_ref, 2, tm, n_ctx) * (_rms(out, 1e-6) * pn_ref[...])


def _merge(xs, gates, ya, yb, aux, oc, cols_c, yd, prm, mods, l, tm, n_ctx):
    B, T, D = xs.shape
    W = BRANCH_W
    vec = lambda n: pl.BlockSpec((None, 1, n), lambda b, i: (l, 0, 0))
    tok = lambda n: pl.BlockSpec((1, tm, n), lambda b, i: (b, i, 0))
    feat = pl.BlockSpec((1, W, tm), lambda b, i: (b, 0, i))
    return pl.pallas_call(
        functools.partial(_merge_kernel, tm=tm, n_ctx=n_ctx),
        grid=(B, T // tm),
        in_specs=[tok(D), tok(4 * D), feat, tok(W), tok(W), tok(2 * W), tok(W), tok(W),
                  pl.BlockSpec((1, tm, W), lambda b, i: (b, i, 2)),
                  feat,
                  vec(W), vec(W), vec(W),
                  _const_spec((None, 4, W, D), lambda b, i: (l, 0, 0, 0)),
                  _const_spec((None, D, D), lambda b, i: (l, 0, 0)),
                  vec(D), _mod_spec(l, D),
                  _const_spec((2 * W, W), lambda b, i: (0, 0))],
        out_specs=tok(D),
        out_shape=jax.ShapeDtypeStruct((B, T, D), F32),
        compiler_params=_cparams(("parallel", "parallel")),
        name="merge",
    )(xs, gates, ya, yb[0], yb[1], aux, oc[0], oc[1], cols_c, yd, prm["lnx_g"], prm["lnx_b"], prm["ret_gn"],
      prm["w_branch"], prm["w_out"], prm["norm_post_mix"], mods, prm["ones_bd"])


def _mlp_kernel(x_ref, g_ref, pn_ref, mod_ref, wu_ref, wd_ref, o_ref, *, tm, n_ctx):
    x = x_ref[0]
    h = _rms(x, 1e-6) * g_ref[...]
    hb = (h * (1.0 + _mod_row(mod_ref, 4, tm, n_ctx)) + _mod_row(mod_ref, 3, tm, n_ctx)).astype(BF16)
    F = wu_ref.shape[-1]
    y = None
    for f0 in range(0, F, F // MLP_SPLIT):
        sl = slice(f0, f0 + F // MLP_SPLIT)
        u = jnp.dot(hb, wu_ref[:, sl], preferred_element_type=F32)
        u = jnp.square(jnp.maximum(u, 0.0)).astype(BF16)
        t = jnp.dot(u, wd_ref[sl, :], preferred_element_type=F32)
        y = t if y is None else y + t
    o_ref[0] = x + _mod_row(mod_ref, 5, tm, n_ctx) * (_rms(y, 1e-6) * pn_ref[...])


def _mlp(xs, prm, mods, l, tm, n_ctx):
    B, T, D = xs.shape
    F = prm["w_up"].shape[-1]
    vec = lambda n: pl.BlockSpec((None, 1, n), lambda b, i: (l, 0, 0))
    return pl.pallas_call(
        functools.partial(_mlp_kernel, tm=tm, n_ctx=n_ctx),
        grid=(B, T // tm),
        in_specs=[pl.BlockSpec((1, tm, D), lambda b, i: (b, i, 0)),
                  vec(D), vec(D), _mod_spec(l, D),
                  _const_spec((None, D, F), lambda b, i: (l, 0, 0)),
                  _const_spec((None, F, D), lambda b, i: (l, 0, 0))],
        out_specs=pl.BlockSpec((1, tm, D), lambda b, i: (b, i, 0)),
        out_shape=jax.ShapeDtypeStruct((B, T, D), F32),
        compiler_params=_cparams(("parallel", "parallel")),
        name="mlp",
    )(xs, prm["norm_pre_mlp"], prm["norm_post_mlp"], mods, prm["w_up"], prm["w_down"])


def kernel(x, c, ctx, c_ctx, ada_w, ada_b, norm_pre_mix, norm_post_mix, norm_pre_mlp, norm_post_mlp,
           w_in, diff_lam_q, diff_lam_k, diff_subln, rwkv_mu, rwkv_w0, rwkv_w2, rwkv_a0, rwkv_a2,
           rwkv_g2, rwkv_kk, rwkv_ka, rwkv_rk, rwkv_lnx_g, rwkv_lnx_b, ret_decay, ret_gn, win_sink,
           w_branch, w_out, w_up, w_down):
    B, S, D = x.shape
    n_ctx = ctx.shape[1]
    T = n_ctx + S
    L = ada_w.shape[0]
    W = BRANCH_W
    TM = 256
    TD = next(t for t in DENSE_TILES if T % t == 0)
    assert n_ctx % TM == 0 and S % TM == 0 and S % GRID_W == 0 and B + 1 <= 8

    cc = jnp.zeros((8, D), F32).at[:B].set(c).at[B].set(c_ctx)
    mod = _ada_mod(cc, ada_w, ada_b).reshape(L, 8, 6, D)
    mod_ctx = jnp.broadcast_to(mod[:, B][:, None], (L, B, 6, D))
    mods = jnp.stack([mod_ctx, mod[:, :B]], axis=2)
    mods = jnp.pad(mods, ((0, 0), (0, 0), (0, 0), (0, 2), (0, 0)))

    lat = jnp.arange(S, dtype=jnp.int32)
    row = (lat // GRID_W).astype(F32)
    col = (lat % GRID_W).astype(F32)
    tabs_a = _rope_tables([row, col], W, HEAD_W // 2, n_ctx)
    tabs_d = _rope_tables([row, col], W, HEAD_W, n_ctx)
    tabs_c = _rope_tables([lat.astype(F32)], W // 2, HEAD_W // 2, n_ctx)

    sizes = (3 * W, rwkv_mu.shape[-1], 3 * W, 2 * W, 4 * D)
    offs = np.concatenate([[0], np.cumsum(sizes)])
    w_groups = [w_in[:, :, int(offs[n]):int(offs[n + 1])].astype(BF16) for n in range(5)]

    lw_w, la_w = rwkv_w2.shape[2], rwkv_a2.shape[2]
    zw = jnp.zeros((L, W, 2 * W), F32)
    w2cat = (zw.at[:, 0:lw_w, 0:W].set(rwkv_w2[:, 0])
             .at[:, lw_w:2 * lw_w, W:2 * W].set(rwkv_w2[:, 1]))
    a2cat = (zw.at[:, 2 * lw_w:2 * lw_w + la_w, 0:W].set(rwkv_a2[:, 0])
             .at[:, 2 * lw_w + la_w:2 * lw_w + 2 * la_w, W:2 * W].set(rwkv_a2[:, 1]))
    blk = np.arange(W) // HEAD_W
    ones_bd = (blk[:, None] == blk[None, :]).astype(np.float32)
    ones_bd = jnp.asarray(np.concatenate([ones_bd, ones_bd], axis=0), BF16)
    v3 = lambda a: a.reshape(L, 1, -1)

    def hl3(w):
        hi = w.astype(BF16)
        return jnp.concatenate([hi, (w - hi.astype(F32)).astype(BF16), hi], axis=-2)

    prm = dict(mu=v3(rwkv_mu), kk=v3(rwkv_kk), ka=v3(rwkv_ka), rk=v3(rwkv_rk), w0=rwkv_w0, a0=rwkv_a0,
               w2=hl3(w2cat), a2=hl3(a2cat), g2=hl3(rwkv_g2), ones_bd=ones_bd,
               lnx_g=v3(rwkv_lnx_g), lnx_b=v3(rwkv_lnx_b), ret_gn=v3(ret_gn),
               w_branch=w_branch.astype(BF16), w_out=w_out.astype(BF16),
               norm_post_mix=v3(norm_post_mix), norm_pre_mlp=v3(norm_pre_mlp),
               norm_post_mlp=v3(norm_post_mlp), w_up=w_up.astype(BF16), w_down=w_down.astype(BF16))
    subln = diff_subln.reshape(L, N_HEADS, HEAD_W, 1)
    decay = ret_decay.reshape(L, 2, 1, N_HEADS)
    TQ_D = 256
    sink_col = jnp.broadcast_to(win_sink.reshape(L, 2, 1, 2, 1), (L, 2, 1, 2, TQ_D)).reshape(L, 2, 1, 2 * TQ_D)
    gain_pre = v3(norm_pre_mix)

    xs = jnp.concatenate([ctx, x], axis=1)
    for l in range(L):
        qa, ka, va, cols_b, cols_c, qd, kd, vd, gates = _input_proj(
            xs, gain_pre, mods, w_groups, tabs_a, tabs_d, l, TD, n_ctx)
        ya = _diff_attention(qa, ka, va, diff_lam_q, diff_lam_k, subln, l, ATT_TQ, ATT_TK, ATT_UNROLL, n_ctx)
        rvk, dirs, aux = _rwkv_prep(cols_b, prm, l, TM, n_ctx)
        yb = _rwkv_scan(rvk, dirs, n_ctx)
        oc = _retention(cols_c, tabs_c, decay, l, n_ctx)
        yd = _window_attention(qd, kd, vd, sink_col, l, TQ_D, n_ctx)
        xs = _merge(xs, gates, ya, yb, aux, oc, cols_c, yd, prm, mods, l, TD, n_ctx)
        xs = _mlp(xs, prm, mods, l, TD, n_ctx)
    return xs[:, n_ctx:]
```

```python
import functools
import math

import numpy as np
import jax
import jax.numpy as jnp
from jax import lax
from jax.experimental import pallas as pl
from jax.experimental.pallas import tpu as pltpu

F32 = jnp.float32
BF16 = jnp.bfloat16
HI = lax.Precision.HIGHEST

GRID_W = 64
ROPE_BASE = 10000.0
NEG_INF = -1e30
WINDOW = 128
N_HEADS = 4
HEAD_W = 64
BRANCH_W = N_HEADS * HEAD_W
RW_CHUNK = 64
RW_SUB = 4
RET_CHUNK = 128
RET_SUB = 2
RW_GN_EPS = 64e-5
ATT_TQ, ATT_TK, ATT_UNROLL = 256, 256, 64
ATT_LOOKAHEAD, ATT_SLOTS = 3, 8
DENSE_TILES = (640, 512, 384, 256, 128)
MLP_SPLIT = 2
VMEM_LIMIT = 56 * 1024 * 1024


def _cparams(sem, vmem=None):
    return pltpu.CompilerParams(dimension_semantics=sem, vmem_limit_bytes=vmem or VMEM_LIMIT)


def _dotf(a, b):
    return jnp.dot(a, b, precision=HI, preferred_element_type=F32)


def _dotb(a, b):
    return jnp.dot(a.astype(BF16), b.astype(BF16), preferred_element_type=F32)


def _dot_nt(a, b, precision=None):
    return lax.dot_general(a, b, (((1,), (1,)), ((), ())), precision=precision,
                           preferred_element_type=F32)


def _dot_tn(a, b, precision=None):
    return lax.dot_general(a, b, (((0,), (0,)), ((), ())), precision=precision,
                           preferred_element_type=F32)


def _split(x):
    hi = x.astype(BF16)
    return hi, (x - hi.astype(F32)).astype(BF16)


def _dot_hl(x, w3):
    hi, lo = _split(x)
    return jnp.dot(jnp.concatenate([hi, hi, lo], axis=1), w3, preferred_element_type=F32)


def _group_sum(x, ones2):
    hi, lo = _split(x)
    return jnp.dot(jnp.concatenate([hi, lo], axis=1), ones2, preferred_element_type=F32)


def _sdot(a, b):
    return jnp.dot(a.astype(BF16), b.astype(BF16), preferred_element_type=F32)


def _sdot_nt(a, b):
    return _dot_nt(a.astype(BF16), b.astype(BF16))


def _sdot_tn(a, b):
    return _dot_tn(a.astype(BF16), b.astype(BF16))


def _sigmoid(x):
    return 1.0 / (1.0 + jnp.exp(-x))


def _softplus(x):
    return jnp.maximum(x, 0.0) + jnp.log1p(jnp.exp(-jnp.abs(x)))


def _rms(x, eps):
    return x * lax.rsqrt(jnp.mean(x * x, axis=-1, keepdims=True) + eps)


def _const_spec(shape, index):
    return pl.BlockSpec(shape, index, pipeline_mode=pl.Buffered(1))


def _ada_kernel(c_ref, w_ref, b_ref, o_ref):
    c = c_ref[...]
    o_ref[0] = _dotf(c * _sigmoid(c), w_ref[0]) + b_ref[0]


def _ada_mod(cc, ada_w, ada_b):
    L, D, N = ada_w.shape
    tn = N // 4
    return pl.pallas_call(
        _ada_kernel,
        grid=(L, N // tn),
        in_specs=[pl.BlockSpec((8, D), lambda l, n: (0, 0)),
                  pl.BlockSpec((1, D, tn), lambda l, n: (l, 0, n)),
                  pl.BlockSpec((1, 1, tn), lambda l, n: (l, 0, n))],
        out_specs=pl.BlockSpec((1, 8, tn), lambda l, n: (l, 0, n)),
        out_shape=jax.ShapeDtypeStruct((L, 8, N), F32),
        compiler_params=_cparams(("arbitrary", "arbitrary")),
        name="ada_mod",
    )(cc, ada_w, ada_b.reshape(L, 1, N))


def _mod_row(mod_ref, j, tm, n_ctx):
    tok = pl.program_id(1) * tm + lax.broadcasted_iota(jnp.int32, (tm, 1), 0)
    return jnp.where(tok < n_ctx, mod_ref[0, j:j + 1, :], mod_ref[1, j:j + 1, :])


def _mod_spec(l, D):
    return pl.BlockSpec((None, None, 2, 8, D), lambda b, i: (l, b, 0, 0, 0))


def _proj_kernel(x_ref, g_ref, mod_ref, wa_ref, wb_ref, wc_ref, wd_ref, wg_ref,
                 cosa_ref, sma_ref, spa_ref, cosd_ref, smd_ref, spd_ref,
                 qa_ref, ka_ref, va_ref, cb_ref, cc_ref, qd_ref, kd_ref, vd_ref, gates_ref, *,
                 tm, n_ctx, half_a, half_d):
    h = _rms(x_ref[0], 1e-6) * g_ref[...]
    h = h * (1.0 + _mod_row(mod_ref, 1, tm, n_ctx)) + _mod_row(mod_ref, 0, tm, n_ctx)
    hb = h.astype(BF16)
    proj = lambda w_ref: jnp.dot(hb, w_ref[...], preferred_element_type=F32)
    _store_diff_qkv(proj(wa_ref), cosa_ref[...], sma_ref[...], spa_ref[...], half_a, qa_ref, ka_ref, va_ref)
    cb_ref[0] = proj(wb_ref)
    cc_ref[0] = proj(wc_ref)
    _store_win_qkv(proj(wd_ref), cosd_ref[...], smd_ref[...], spd_ref[...], half_d, qd_ref, kd_ref, vd_ref)
    gates_ref[0] = proj(wg_ref).astype(gates_ref.dtype)


def _input_proj(xs, gain, mods, weights, tabs_a, tabs_d, l, tm, n_ctx):
    B, T, D = xs.shape
    W = BRANCH_W
    tok = lambda n: pl.BlockSpec((1, tm, n), lambda b, i: (b, i, 0))
    tab = pl.BlockSpec((tm, W), lambda b, i: (i, 0))
    in_specs = [tok(D), pl.BlockSpec((None, 1, D), lambda b, i: (l, 0, 0)), _mod_spec(l, D)]
    in_specs += [_const_spec((None, D, w.shape[-1]), lambda b, i: (l, 0, 0)) for w in weights]
    in_specs += [tab] * 6
    heads = lambda *lead: pl.BlockSpec((1,) + lead + (tm, HEAD_W), lambda b, i: (b,) + (0,) * len(lead) + (i, 0))
    heads_t = lambda n: pl.BlockSpec((1, n, HEAD_W, tm), lambda b, i: (b, 0, 0, i))
    qa_spec = pl.BlockSpec((1, N_HEADS, 2, HEAD_W, tm), lambda b, i: (b, 0, 0, 0, i))
    out_specs = [qa_spec, heads(N_HEADS), heads_t(N_HEADS), tok(weights[1].shape[-1]),
                 tok(weights[2].shape[-1]), pl.BlockSpec((1, W, tm), lambda b, i: (b, 0, i)),
                 heads(2), heads_t(2), tok(weights[4].shape[-1])]
    sds = jax.ShapeDtypeStruct
    out_shape = [sds((B, N_HEADS, 2, HEAD_W, T), BF16), sds((B, N_HEADS, T, HEAD_W), BF16),
                 sds((B, N_HEADS, HEAD_W, T), BF16), sds((B, T, weights[1].shape[-1]), F32),
                 sds((B, T, weights[2].shape[-1]), F32), sds((B, W, T), BF16),
                 sds((B, 2, T, HEAD_W), BF16), sds((B, 2, HEAD_W, T), BF16),
                 sds((B, T, weights[4].shape[-1]), BF16)]
    return pl.pallas_call(
        functools.partial(_proj_kernel, tm=tm, n_ctx=n_ctx, half_a=tabs_a[3], half_d=tabs_d[3]),
        grid=(B, T // tm),
        in_specs=in_specs, out_specs=out_specs, out_shape=out_shape,
        compiler_params=_cparams(("parallel", "parallel")),
        name="input_proj",
    )(xs, gain, mods, *weights, *tabs_a[:3], *tabs_d[:3])


def _rope_tables(pos_sets, lanes, group, n_ctx):
    n_sets = len(pos_sets)
    sub = group // n_sets
    half = sub // 2
    angs = []
    for pos in pos_sets:
        inv_freq = ROPE_BASE ** (-jnp.arange(half, dtype=F32) / half)
        ang = pos[:, None] * inv_freq[None, :]
        angs.append(jnp.concatenate([ang, ang], axis=-1))
    ang = jnp.concatenate(angs, axis=-1)
    ang = jnp.tile(ang, (1, lanes // group))
    ang = jnp.concatenate([jnp.zeros((n_ctx, lanes), F32), ang], axis=0)
    first = (np.arange(lanes) % sub) < half
    cos, sin = jnp.cos(ang), jnp.sin(ang)
    sin_minus = jnp.where(first[None, :], -sin, 0.0)
    sin_plus = jnp.where(first[None, :], 0.0, sin)
    return cos, sin_minus, sin_plus, half


def _rope(x, cos, sin_minus, sin_plus, half):
    n = x.shape[-1]
    return x * cos + pltpu.roll(x, n - half, 1) * sin_minus + pltpu.roll(x, half, 1) * sin_plus


def _store_diff_qkv(c, cos, sm, sp, half, q_ref, k_ref, v_ref):
    scale = (HEAD_W // 2) ** -0.5 * math.log2(math.e)
    q = _rope(c[:, 0:BRANCH_W], cos, sm, sp, half) * scale
    k = _rope(c[:, BRANCH_W:2 * BRANCH_W], cos, sm, sp, half)
    v = c[:, 2 * BRANCH_W:3 * BRANCH_W]
    lane = lax.broadcasted_iota(jnp.int32, q.shape, 1)
    comp = (lane >> 5) & 1
    q0 = jnp.where(comp == 0, q, 0.0).T.astype(BF16)
    q1 = jnp.where(comp == 1, q, 0.0).T.astype(BF16)
    kb, vt = k.astype(BF16), v.T.astype(BF16)
    for h in range(N_HEADS):
        sl = slice(h * HEAD_W, (h + 1) * HEAD_W)
        q_ref[0, h, 0] = q0[sl, :]
        q_ref[0, h, 1] = q1[sl, :]
        k_ref[0, h] = kb[:, sl]
        v_ref[0, h] = vt[sl, :]


def _flash_kernel(lq_ref, lk_ref, g_ref, q_ref, k_ref, vt_ref, o_ref, s_sc, acc_sc, *,
                  tq, tk, unroll, n_ctx, T, lam_init):
    qi = pl.program_id(2)
    qt = jnp.concatenate([q_ref[0, 0, 0], q_ref[0, 0, 1]], axis=1)

    def scores(slot, off, size):
        s_sc[slot, 0:size] = jnp.dot(k_ref[0, 0, pl.ds(off, size), :], qt, preferred_element_type=F32)

    def absorb(slot, off, size, m):
        s = s_sc[slot, 0:size]
        m_new = jnp.maximum(m, jnp.max(s, axis=0, keepdims=True))
        p = jnp.exp2(s - m_new).astype(BF16)
        vt = jnp.concatenate([vt_ref[0, 0, :, pl.ds(off, size)], jnp.ones((8, size), BF16)], axis=0)
        acc_sc[...] = acc_sc[...] * jnp.exp2(m - m_new) + jnp.dot(vt, p, preferred_element_type=F32)
        return m_new

    n_slots, look = s_sc.shape[0], ATT_LOOKAHEAD
    lat_off = lambda c: n_ctx + (c - 1) * tk
    acc_sc[...] = jnp.zeros(acc_sc.shape, F32)
    m = jnp.full((1, 2 * tq), NEG_INF, F32)
    scores(0, 0, n_ctx)
    for c in range(1, look + 1):
        scores(c % n_slots, min(lat_off(c), T - tk), tk)
    m = absorb(0, 0, n_ctx, m)

    def body(jj, m):
        c0 = 1 + jj * unroll
        for u in range(unroll):
            off = pl.multiple_of(lat_off(c0 + u), math.gcd(n_ctx, tk))
            nxt = pl.multiple_of(jnp.minimum(lat_off(c0 + u + look), T - tk), math.gcd(n_ctx, tk))
            scores((1 + u + look) % n_slots, nxt, tk)
            m = absorb((1 + u) % n_slots, off, tk, m)
        return m

    n_trips = (T - n_ctx) // (unroll * tk)
    lax.fori_loop(0, jnp.where(qi * tq < n_ctx, 0, n_trips), body, m)
    acc = acc_sc[...]
    o = acc[0:HEAD_W] / acc[HEAD_W:HEAD_W + 1]
    e0 = jnp.exp(jnp.sum(lq_ref[0:1, :] * lk_ref[0:1, :], axis=1, keepdims=True))
    e1 = jnp.exp(jnp.sum(lq_ref[1:2, :] * lk_ref[1:2, :], axis=1, keepdims=True))
    lam = e0 - e1 + lam_init
    y = o[:, :tq] - lam * o[:, tq:]
    o_ref[0] = y * lax.rsqrt(jnp.mean(y * y, axis=0, keepdims=True) + 1e-5) * g_ref[...] * (1.0 - lam_init)


def _diff_attention(q, k, v, lam_q, lam_k, subln, l, tq, tk, unroll, n_ctx):
    B, H, _, _, T = q.shape
    dqk = lam_q.shape[-1]
    lam_init = 0.8 - 0.6 * math.exp(-0.3 * l)
    unroll = math.gcd(unroll, (T - n_ctx) // tk)
    assert (unroll % ATT_SLOTS == 0 or unroll * tk == T - n_ctx) and ATT_LOOKAHEAD < ATT_SLOTS
    assert (T - n_ctx) % (unroll * tk) == 0 and n_ctx % tq == 0 and n_ctx % 128 == 0 and tk % 128 == 0
    kern = functools.partial(_flash_kernel, tq=tq, tk=tk, unroll=unroll, n_ctx=n_ctx, T=T,
                             lam_init=lam_init)
    return pl.pallas_call(
        kern,
        grid=(B, H, T // tq),
        in_specs=[pl.BlockSpec((None, 2, dqk), lambda b, h, i: (l, 0, 0)),
                  pl.BlockSpec((None, 2, dqk), lambda b, h, i: (l, 0, 0)),
                  pl.BlockSpec((None, None, HEAD_W, 1), lambda b, h, i: (l, h, 0, 0)),
                  pl.BlockSpec((1, 1, 2, HEAD_W, tq), lambda b, h, i: (b, h, 0, 0, i)),
                  pl.BlockSpec((1, 1, T, HEAD_W), lambda b, h, i: (b, h, 0, 0)),
                  pl.BlockSpec((1, 1, HEAD_W, T), lambda b, h, i: (b, h, 0, 0))],
        out_specs=pl.BlockSpec((1, HEAD_W, tq), lambda b, h, i: (b, h, i)),
        out_shape=jax.ShapeDtypeStruct((B, H * HEAD_W, T), F32),
        scratch_shapes=[pltpu.VMEM((ATT_SLOTS, max(tk, n_ctx), 2 * tq), F32),
                        pltpu.VMEM((HEAD_W + 8, 2 * tq), F32)],
        compiler_params=_cparams(("parallel", "parallel", "arbitrary")),
        name="diff_attention",
    )(lam_q, lam_k, subln, q, k, v)


def _rwprep_kernel(c_ref, p_ref, n_ref, mu_ref, kk_ref, ka_ref, rk_ref, w0_ref, a0_ref,
                   w2_ref, a2_ref, g2_ref, ones_ref, rvk_ref, dir_ref, aux_ref, *, tm, n_ctx, T):
    i = pl.program_id(1)
    c = c_ref[0]
    t0 = i * tm
    seg_start = jnp.logical_or(t0 == 0, t0 == n_ctx)
    seg_end = jnp.logical_or(t0 + tm == n_ctx, t0 + tm == T)
    pv = jnp.where(seg_start, 0.0, p_ref[0][7:8, :])
    nx = jnp.where(seg_end, 0.0, n_ref[0][0:1, :])
    row = lax.broadcasted_iota(jnp.int32, c.shape, 0)
    prev = jnp.where(row == 0, pv, pltpu.roll(c, 1, 0))
    nxt = jnp.where(row == tm - 1, nx, pltpu.roll(c, tm - 1, 0))
    xs = c + (0.5 * (prev + nxt) - c) * mu_ref[...]
    W = BRANCH_W
    r, k, v = xs[:, 0:W], xs[:, W:2 * W], xs[:, 2 * W:3 * W]
    lo, gl = xs[:, 3 * W:4 * W], xs[:, 4 * W:]
    ones2 = ones_ref[...]
    kk = k * kk_ref[...]
    kk = kk * lax.rsqrt(jnp.maximum(_group_sum(kk * kk, ones2), 1e-12))
    wpre = _dot_hl(jnp.tanh(lo), w2_ref[...])
    apre = _dot_hl(lo, a2_ref[...])
    gate = _dot_hl(_sigmoid(gl), g2_ref[...])
    rb = jnp.zeros_like(r)
    for d in range(2):
        z = w0_ref[d:d + 1, :] + wpre[:, d * W:(d + 1) * W]
        w = -_softplus(-z) - 0.5
        a = _sigmoid(a0_ref[d:d + 1, :] + apre[:, d * W:(d + 1) * W])
        kd = k * (1.0 + (a - 1.0) * ka_ref[...])
        dir_ref[0, d, :, 0:W] = -jnp.exp(w)
        dir_ref[0, d, :, W:2 * W] = kd
        dir_ref[0, d, :, 2 * W:3 * W] = kk * a
        rb = rb + r * kd * rk_ref[...]
    rvk_ref[0, :, 0:W] = r
    rvk_ref[0, :, W:2 * W] = v
    rvk_ref[0, :, 2 * W:3 * W] = kk
    aux_ref[0, :, 0:W] = _group_sum(rb, ones2) * v
    aux_ref[0, :, W:2 * W] = gate


def _rwkv_prep(cols, prm, l, tm, n_ctx):
    B, T, NB = cols.shape
    W = BRANCH_W
    nblk8 = T // 8
    r8 = tm // 8
    vec = lambda n: pl.BlockSpec((None, 1, n), lambda b, i: (l, 0, 0))
    pair = lambda n: pl.BlockSpec((None, 2, n), lambda b, i: (l, 0, 0))
    mat = lambda m, n: _const_spec((None, m, n), lambda b, i: (l, 0, 0))
    return pl.pallas_call(
        functools.partial(_rwprep_kernel, tm=tm, n_ctx=n_ctx, T=T),
        grid=(B, T // tm),
        in_specs=[pl.BlockSpec((1, tm, NB), lambda b, i: (b, i, 0)),
                  pl.BlockSpec((1, 8, NB), lambda b, i: (b, jnp.maximum(i * r8 - 1, 0), 0)),
                  pl.BlockSpec((1, 8, NB), lambda b, i: (b, jnp.minimum((i + 1) * r8, nblk8 - 1), 0)),
                  vec(NB), vec(W), vec(W), vec(W), pair(W), pair(W),
                  mat(3 * W, 2 * W), mat(3 * W, 2 * W), mat(3 * (NB - 4 * W), W),
                  _const_spec((2 * W, W), lambda b, i: (0, 0))],
        out_specs=[pl.BlockSpec((1, tm, 3 * W), lambda b, i: (b, i, 0)),
                   pl.BlockSpec((1, 2, tm, 3 * W), lambda b, i: (b, 0, i, 0)),
                   pl.BlockSpec((1, tm, 2 * W), lambda b, i: (b, i, 0))],
        out_shape=[jax.ShapeDtypeStruct((B, T, 3 * W), F32),
                   jax.ShapeDtypeStruct((B, 2, T, 3 * W), F32),
                   jax.ShapeDtypeStruct((B, T, 2 * W), F32)],
        compiler_params=_cparams(("parallel", "parallel")),
        name="rwkv_prep",
    )(cols, cols, cols, prm["mu"], prm["kk"], prm["ka"], prm["rk"], prm["w0"], prm["a0"],
      prm["w2"], prm["a2"], prm["g2"], prm["ones_bd"])


def _rw_chunks(chunks, rounds, state_of, states):
    C, W = RW_CHUNK, BRANCH_W
    each = lambda f, *cols: [f(*a) for a in zip(*cols)]
    r, v, kk, lw, kd, b, sgn = (list(col) for col in zip(*chunks))

    ti = lax.broadcasted_iota(jnp.int32, (C, C), 0)
    tj = lax.broadcasted_iota(jnp.int32, (C, C), 1)
    ri = lax.broadcasted_iota(jnp.int32, (W, W), 0)
    ci = lax.broadcasted_iota(jnp.int32, (W, W), 1)
    same_head = (ri >> 6) == (ci >> 6)
    bd = lambda x: jnp.where(same_head, jnp.concatenate([x] * N_HEADS, axis=0), 0.0)
    row = lax.broadcasted_iota(jnp.int32, (C, W), 0)
    col = lax.broadcasted_iota(jnp.int32, (C, W), 1) & (C - 1)
    masks = {sg: (jnp.where((tj - ti) * sg <= 0, 1.0, 0.0), (col - row) * sg < 0, (col - row) * sg <= 0)
             for sg in set(sgn)}
    tri = [masks[sg][0] for sg in sgn]
    strict = [masks[sg][1] for sg in sgn]
    incl = [masks[sg][2] for sg in sgn]
    blk16 = (row >> 4) == (col >> 4)
    blk32 = (row >> 5) == (col >> 5)
    eye = jnp.where(row == col, 1.0, 0.0)

    log_g = each(_dotf, tri, lw)
    log_gc = [jnp.sum(x, axis=0, keepdims=True) for x in lw]
    kk_t = each(lambda kk_, lg, lw_: kk_ * jnp.exp(lg - lw_), kk, log_g, lw)
    r_t = each(lambda r_, lg: r_ * jnp.exp(lg), r, log_g)
    g_inv = [jnp.exp(-lg) for lg in log_g]
    g_tail = each(lambda lgc, lg: jnp.exp(lgc - lg), log_gc, log_g)

    bdb = lambda x: bd(x.astype(BF16))
    left = each(lambda a, c: jnp.concatenate([a, c], axis=0).astype(BF16), kk_t, r_t)
    right = each(lambda b_, kd_, gi: jnp.concatenate([bdb(b_ * gi), bdb(kd_ * gi)], axis=0), b, kd, g_inv)
    g = each(_dot_nt, left, right)
    m_sl = each(lambda st, g_: jnp.where(st, g_[:C, :W], 0.0), strict, g)
    n_sl = each(lambda st, g_: jnp.where(st, g_[:C, W:], 0.0), strict, g)
    pq_l = each(lambda ic, g_: jnp.concatenate([jnp.where(ic, g_[C:, :W], 0.0),
                                                jnp.where(ic, g_[C:, W:], 0.0)], axis=1).astype(BF16), incl, g)

    v_bd = [bdb(x) for x in v]
    nv = each(_sdot, n_sl, v_bd)
    b_tail = each(lambda b_, kd_, gt: jnp.concatenate([b_ * gt, kd_ * gt], axis=0).astype(BF16), b, kd, g_tail)
    gc_col = [jnp.sum(jnp.where(ri == ci, jnp.exp(lgc), 0.0), axis=1, keepdims=True) for lgc in log_gc]

    d0 = [jnp.where(blk16, m_, 0.0) for m_ in m_sl]
    sq = lambda x: _sdot(x, bdb(x))
    a2 = each(sq, d0)
    a4 = each(sq, a2)
    a8 = each(sq, a4)
    t = [eye - d_ for d_ in d0]
    for a_ in (a2, a4, a8):
        t = each(lambda t_, x: t_ + _sdot(t_, bdb(x)), t, a_)
    for cm in (jnp.logical_and(blk32, jnp.logical_not(blk16)), jnp.logical_not(blk32)):
        u = each(lambda m_, t_: _sdot(jnp.where(cm, m_, 0.0), bdb(t_)), m_sl, t)
        t = each(lambda t_, u_: t_ - _sdot(t_, bdb(u_)), t, u)

    states = list(states)
    ys = [None] * len(chunks)
    for ids in rounds:
        pick = lambda xs: [xs[i] for i in ids]
        s0 = [states[state_of[i]] for i in ids]
        g2 = each(_sdot, pick(left), s0)
        rhs = each(lambda g2_, nv_: -(g2_[:C] + nv_), g2, pick(nv))
        sa = each(lambda t_, rh: _sdot(t_, bdb(rh)), pick(t), rhs)
        y = each(lambda g2_, pq, sa_, vb: g2_[C:] + _sdot(pq, jnp.concatenate([bdb(sa_), vb], axis=0)),
                 g2, pick(pq_l), sa, pick(v_bd))
        ds = each(lambda bt, sa_, v_: _sdot_tn(bt, jnp.concatenate([sa_, v_], axis=0)),
                  pick(b_tail), sa, pick(v))
        for i, y_, s_, ds_ in zip(ids, y, s0, ds):
            ys[i] = y_
            states[state_of[i]] = jnp.where(same_head, s_ * gc_col[i] + ds_, 0.0)
    return ys, states


def _rwscan_kernel(rvf_ref, rvb_ref, df_ref, db_ref, yf_ref, yb_ref, s_ref, *, n_sub):
    C, W = RW_CHUNK, BRANCH_W

    @pl.when(pl.program_id(0) == 0)
    def _():
        s_ref[...] = jnp.zeros(s_ref.shape, F32)

    refs = ((rvf_ref, df_ref, yf_ref), (rvb_ref, db_ref, yb_ref))
    chunks, where, state_of = [], [], []
    rounds = [[] for _ in range(n_sub)]
    for bi in range(rvf_ref.shape[0]):
        for d, (rv, dr, _) in enumerate(refs):
            for k in range(n_sub):
                rows = pl.ds((k if d == 0 else n_sub - 1 - k) * C, C)
                rounds[k].append(len(chunks))
                state_of.append(2 * bi + d)
                where.append((d, bi, rows))
                chunks.append((rv[bi, rows, 0:W], rv[bi, rows, W:2 * W], rv[bi, rows, 2 * W:3 * W],
                               dr[bi, 0, rows, 0:W], dr[bi, 0, rows, W:2 * W], dr[bi, 0, rows, 2 * W:3 * W],
                               1 - 2 * d))
    n_states = s_ref.shape[0]
    ys, states = _rw_chunks(chunks, rounds, state_of, [s_ref[i] for i in range(n_states)])
    for (d, bi, rows), y in zip(where, ys):
        refs[d][2][bi, rows, :] = y
    for i in range(n_states):
        s_ref[i] = states[i]


def _chunk_order(s, d, n_ctx_chunks, n_chunks):
    bwd = jnp.where(s < n_ctx_chunks, n_ctx_chunks - 1 - s, n_chunks - 1 - (s - n_ctx_chunks))
    return jnp.where(d == 0, s, bwd)


def _rwkv_scan(rvk, dirs, n_ctx):
    B, T, _ = rvk.shape
    W = BRANCH_W
    C = RW_CHUNK * RW_SUB
    assert T % C == 0 and n_ctx % C == 0
    nc, ncx = T // C, n_ctx // C
    ch = lambda s, d: _chunk_order(s, d, ncx, nc)
    rv_spec = lambda d: pl.BlockSpec((B, C, 3 * W), lambda s: (0, ch(s, d), 0))
    dir_spec = lambda d: pl.BlockSpec((B, 1, C, 3 * W), lambda s: (0, d, ch(s, d), 0))
    y_spec = lambda d: pl.BlockSpec((B, C, W), lambda s: (0, ch(s, d), 0))
    return pl.pallas_call(
        functools.partial(_rwscan_kernel, n_sub=RW_SUB),
        grid=(nc,),
        in_specs=[rv_spec(0), rv_spec(1), dir_spec(0), dir_spec(1)],
        out_specs=[y_spec(0), y_spec(1)],
        out_shape=[jax.ShapeDtypeStruct((B, T, W), F32)] * 2,
        scratch_shapes=[pltpu.VMEM((2 * B, W, W), F32)],
        compiler_params=_cparams(("arbitrary",)),
        name="rwkv_scan",
    )(rvk, rvk, dirs, dirs)


def _ret_chunk(c, cos, sm, sp, dec, d, half, scale):
    C, W = RET_CHUNK, BRANCH_W
    QW = W // 2
    q = _rope(c[:, 0:QW], cos, sm, sp, half)
    k = _rope(c[:, QW:2 * QW], cos, sm, sp, half) * scale
    v = c[:, 2 * QW:2 * QW + W]
    log_g = -_softplus(-dec)
    lane_q = lax.broadcasted_iota(jnp.int32, (1, QW), 1) >> 5
    lane_v = lax.broadcasted_iota(jnp.int32, (1, W), 1) >> 6
    lg_q = jnp.zeros((1, QW), F32)
    lg_v = jnp.zeros((1, W), F32)
    for h in range(N_HEADS):
        lg_q = jnp.where(lane_q == h, log_g[:, h:h + 1], lg_q)
        lg_v = jnp.where(lane_v == h, log_g[:, h:h + 1], lg_v)
    idx = lax.broadcasted_iota(jnp.int32, (C, 1), 0).astype(F32)
    q_dec = jnp.exp((idx + 1.0 if d == 0 else C - idx) * lg_q)
    k_dec = jnp.exp((C - 1.0 - idx if d == 0 else idx) * lg_q)
    ti = lax.broadcasted_iota(jnp.int32, (C, C), 0)
    tj = lax.broadcasted_iota(jnp.int32, (C, C), 1)
    rel = ((ti - tj) * (1 - 2 * d)).astype(F32)

    kb = k.astype(BF16)
    vb = v.astype(BF16)
    o = jnp.zeros((C, W), F32)
    for h in range(N_HEADS):
        dm = jnp.where(rel >= 0, jnp.exp(jnp.maximum(rel, 0.0) * log_g[:, h:h + 1]), 0.0)
        qh = jnp.where(lane_q == h, q, 0.0).astype(BF16)
        sc = _dot_nt(qh, kb) * dm
        o = o + jnp.where(lane_v == h, jnp.dot(sc.astype(BF16), vb, preferred_element_type=F32), 0.0)

    ri = lax.broadcasted_iota(jnp.int32, (QW, W), 0) >> 5
    ci = lax.broadcasted_iota(jnp.int32, (QW, W), 1) >> 6
    upd = jnp.where(ri == ci, _dot_tn((k * k_dec).astype(BF16), vb), 0.0)
    return o, (q * q_dec).astype(BF16), upd, jnp.exp(C * lg_v)


def _ret_kernel(cf_ref, cb_ref, cosf_ref, smf_ref, spf_ref, cosb_ref, smb_ref, spb_ref, dec_ref,
                of_ref, ob_ref, s_ref, *, half, scale):
    C = RET_CHUNK

    @pl.when(pl.program_id(0) == 0)
    def _():
        s_ref[...] = jnp.zeros(s_ref.shape, F32)

    dirs = ((cf_ref, (cosf_ref, smf_ref, spf_ref), of_ref), (cb_ref, (cosb_ref, smb_ref, spb_ref), ob_ref))
    for bi in range(cf_ref.shape[0]):
        for d, (c_ref, tabs, o_ref) in enumerate(dirs):
            order = [pl.ds((k if d == 0 else RET_SUB - 1 - k) * C, C) for k in range(RET_SUB)]
            parts = [_ret_chunk(c_ref[bi, rows, :], tabs[0][rows, :], tabs[1][rows, :], tabs[2][rows, :],
                                dec_ref[d], d, half, scale) for rows in order]
            s = s_ref[2 * bi + d]
            for rows, (o, qd, upd, cdec) in zip(order, parts):
                o_ref[bi, rows, :] = o + jnp.dot(qd, s.astype(BF16), preferred_element_type=F32)
                s = s * cdec + upd
            s_ref[2 * bi + d] = s


def _retention(cols, tabs, decay, l, n_ctx):
    B, T, NCOL = cols.shape
    C, W = RET_CHUNK * RET_SUB, BRANCH_W
    cos, sm, sp, half = tabs
    assert T % C == 0 and n_ctx % C == 0
    nc, ncx = T // C, n_ctx // C
    ch = lambda s, d: _chunk_order(s, d, ncx, nc)
    c_spec = lambda d: pl.BlockSpec((B, C, NCOL), lambda s: (0, ch(s, d), 0))
    tab_spec = lambda d: pl.BlockSpec((C, W // 2), lambda s: (ch(s, d), 0))
    o_spec = lambda d: pl.BlockSpec((B, C, W), lambda s: (0, ch(s, d), 0))
    return pl.pallas_call(
        functools.partial(_ret_kernel, half=half, scale=(HEAD_W // 2) ** -0.5),
        grid=(nc,),
        in_specs=[c_spec(0), c_spec(1), tab_spec(0), tab_spec(0), tab_spec(0),
                  tab_spec(1), tab_spec(1), tab_spec(1),
                  pl.BlockSpec((None, 2, 1, N_HEADS), lambda s: (l, 0, 0, 0))],
        out_specs=[o_spec(0), o_spec(1)],
        out_shape=[jax.ShapeDtypeStruct((B, T, W), F32)] * 2,
        scratch_shapes=[pltpu.VMEM((2 * B, W // 2, W), F32)],
        compiler_params=_cparams(("arbitrary",)),
        name="retention",
    )(cols, cols, cos, sm, sp, cos, sm, sp, decay)


def _store_win_qkv(c, cos, sm, sp, half, q_ref, k_ref, v_ref):
    W = BRANCH_W
    q_ref[0] = (_rope(c[:, 0:W], cos, sm, sp, half) * HEAD_W ** -0.5).T.astype(BF16)
    kw = W // 2
    k = _rope(c[:, W:W + kw], cos[:, :kw], sm[:, :kw], sp[:, :kw], half).astype(BF16)
    vt = c[:, W + kw:W + 2 * kw].T.astype(BF16)
    for kvh in range(2):
        k_ref[0, kvh] = k[:, kvh * HEAD_W:(kvh + 1) * HEAD_W]
        v_ref[0, kvh] = vt[kvh * HEAD_W:(kvh + 1) * HEAD_W, :]


def _winattn_kernel(sink_ref, q_ref, k_ref, v_ref, o_ref, *, tq, n_ctx, T):
    i = pl.program_id(1)
    band = tq + 2 * WINDOW
    ws = pl.multiple_of(jnp.clip(i * tq - WINDOW, n_ctx, T - band), WINDOW)
    kpos = ws + lax.broadcasted_iota(jnp.int32, (band, tq), 0)
    qpos = i * tq + lax.broadcasted_iota(jnp.int32, (band, tq), 1)
    valid = jnp.logical_and(jnp.abs(qpos - kpos) <= WINDOW, qpos >= n_ctx)
    valid = jnp.concatenate([valid, valid], axis=1)
    for kvh in range(k_ref.shape[1]):
        r0 = 2 * kvh * HEAD_W
        qt = jnp.concatenate([q_ref[0, r0:r0 + HEAD_W, :], q_ref[0, r0 + HEAD_W:r0 + 2 * HEAD_W, :]], axis=1)
        s_ctx = jnp.dot(k_ref[0, kvh, 0:n_ctx, :], qt, preferred_element_type=F32)
        s_band = jnp.dot(k_ref[0, kvh, pl.ds(ws, band), :], qt, preferred_element_type=F32)
        s_band = jnp.where(valid, s_band, NEG_INF)
        sink = sink_ref[kvh]
        m = jnp.maximum(jnp.maximum(jnp.max(s_ctx, axis=0, keepdims=True),
                                    jnp.max(s_band, axis=0, keepdims=True)), sink)
        p_ctx = jnp.exp(s_ctx - m).astype(BF16)
        p_band = jnp.exp(s_band - m).astype(BF16)
        vx = jnp.concatenate([v_ref[0, kvh, :, 0:n_ctx], jnp.ones((8, n_ctx), BF16)], axis=0)
        vb = jnp.concatenate([v_ref[0, kvh, :, pl.ds(ws, band)], jnp.ones((8, band), BF16)], axis=0)
        acc = jnp.dot(jnp.concatenate([vx, vb], axis=1), jnp.concatenate([p_ctx, p_band], axis=0),
                      preferred_element_type=F32)
        o = acc[0:HEAD_W] / (acc[HEAD_W:HEAD_W + 1] + jnp.exp(sink - m))
        for g in range(2):
            o_ref[0, r0 + g * HEAD_W:r0 + (g + 1) * HEAD_W, :] = o[:, g * tq:(g + 1) * tq]


def _window_attention(q, k, v, sink_col, l, tq, n_ctx):
    B, KVH, T, _ = k.shape
    G = q.shape[1] // (KVH * HEAD_W)
    return pl.pallas_call(
        functools.partial(_winattn_kernel, tq=tq, n_ctx=n_ctx, T=T),
        grid=(B, T // tq),
        in_specs=[pl.BlockSpec((None, KVH, 1, G * tq), lambda b, i: (l, 0, 0, 0)),
                  pl.BlockSpec((1, KVH * G * HEAD_W, tq), lambda b, i: (b, 0, i)),
                  pl.BlockSpec((1, KVH, T, HEAD_W), lambda b, i: (b, 0, 0, 0)),
                  pl.BlockSpec((1, KVH, HEAD_W, T), lambda b, i: (b, 0, 0, 0))],
        out_specs=pl.BlockSpec((1, KVH * G * HEAD_W, tq), lambda b, i: (b, 0, i)),
        out_shape=jax.ShapeDtypeStruct((B, KVH * G * HEAD_W, T), F32),
        compiler_params=_cparams(("parallel", "arbitrary")),
        name="window_attention",
    )(sink_col, q, k, v)


def _group_norm(x, ones2, eps, center):
    inv = 1.0 / HEAD_W
    if center:
        x = x - _group_sum(x, ones2) * inv
    return x * lax.rsqrt(_group_sum(x * x, ones2) * inv + eps)


def _merge_kernel(x_ref, gates_ref, ya_ref, ybf_ref, ybb_ref, aux_ref, ocf_ref, ocb_ref, gc_ref, yd_ref,
                  lnx_g_ref, lnx_b_ref, gn_ref, wb_ref, wo_ref, pn_ref, mod_ref, ones_ref, o_ref, *,
                  tm, n_ctx):
    W = BRANCH_W
    D = x_ref.shape[-1]
    ones_bd = ones_ref[...]
    yb = _group_norm(ybf_ref[0] + ybb_ref[0], ones_bd, RW_GN_EPS, True)
    yb = yb * lnx_g_ref[...] + lnx_b_ref[...]
    yb = (yb + aux_ref[0, :, 0:W]) * aux_ref[0, :, W:2 * W]
    yc = _group_norm(ocf_ref[0] + ocb_ref[0], ones_bd, 1e-5, True) * gn_ref[...]
    gc = gc_ref[0]
    yc = yc * (gc * _sigmoid(gc))

    lifted = [_dot_tn(ya_ref[0].astype(BF16), wb_ref[0]),
              jnp.dot(yb.astype(BF16), wb_ref[1], preferred_element_type=F32),
              jnp.dot(yc.astype(BF16), wb_ref[2], preferred_element_type=F32),
              _dot_tn(yd_ref[0].astype(BF16), wb_ref[3])]
    m = None
    for n in range(4):
        t = _sigmoid(gates_ref[0, :, n * D:(n + 1) * D].astype(F32)) * lifted[n]
        m = t if m is None else m + t
    out = jnp.dot(m.astype(BF16), wo_ref[...], preferred_element_type=F32)
    o_ref[0] = x_ref[0] + _mod_row(mod_ref, 2, tm, n_ctx) * (_rms(out, 1e-6) * pn_ref[...])


def _merge(xs, gates, ya, yb, aux, oc, cols_c, yd, prm, mods, l, tm, n_ctx):
    B, T, D = xs.shape
    W = BRANCH_W
    vec = lambda n: pl.BlockSpec((None, 1, n), lambda b, i: (l, 0, 0))
    tok = lambda n: pl.BlockSpec((1, tm, n), lambda b, i: (b, i, 0))
    feat = pl.BlockSpec((1, W, tm), lambda b, i: (b, 0, i))
    return pl.pallas_call(
        functools.partial(_merge_kernel, tm=tm, n_ctx=n_ctx),
        grid=(B, T // tm),
        in_specs=[tok(D), tok(4 * D), feat, tok(W), tok(W), tok(2 * W), tok(W), tok(W),
                  pl.BlockSpec((1, tm, W), lambda b, i: (b, i, 2)),
                  feat,
                  vec(W), vec(W), vec(W),
                  _const_spec((None, 4, W, D), lambda b, i: (l, 0, 0, 0)),
                  _const_spec((None, D, D), lambda b, i: (l, 0, 0)),
                  vec(D), _mod_spec(l, D),
                  _const_spec((2 * W, W), lambda b, i: (0, 0))],
        out_specs=tok(D),
        out_shape=jax.ShapeDtypeStruct((B, T, D), F32),
        compiler_params=_cparams(("parallel", "parallel")),
        name="merge",
    )(xs, gates, ya, yb[0], yb[1], aux, oc[0], oc[1], cols_c, yd, prm["lnx_g"], prm["lnx_b"], prm["ret_gn"],
      prm["w_branch"], prm["w_out"], prm["norm_post_mix"], mods, prm["ones_bd"])


def _mlp_kernel(x_ref, g_ref, pn_ref, mod_ref, wu_ref, wd_ref, o_ref, *, tm, n_ctx):
    x = x_ref[0]
    h = _rms(x, 1e-6) * g_ref[...]
    hb = (h * (1.0 + _mod_row(mod_ref, 4, tm, n_ctx)) + _mod_row(mod_ref, 3, tm, n_ctx)).astype(BF16)
    F = wu_ref.shape[-1]
    y = None
    for f0 in range(0, F, F // MLP_SPLIT):
        sl = slice(f0, f0 + F // MLP_SPLIT)
        u = jnp.dot(hb, wu_ref[:, sl], preferred_element_type=F32)
        u = jnp.square(jnp.maximum(u, 0.0)).astype(BF16)
        t = jnp.dot(u, wd_ref[sl, :], preferred_element_type=F32)
        y = t if y is None else y + t
    o_ref[0] = x + _mod_row(mod_ref, 5, tm, n_ctx) * (_rms(y, 1e-6) * pn_ref[...])


def _mlp(xs, prm, mods, l, tm, n_ctx):
    B, T, D = xs.shape
    F = prm["w_up"].shape[-1]
    vec = lambda n: pl.BlockSpec((None, 1, n), lambda b, i: (l, 0, 0))
    return pl.pallas_call(
        functools.partial(_mlp_kernel, tm=tm, n_ctx=n_ctx),
        grid=(B, T // tm),
        in_specs=[pl.BlockSpec((1, tm, D), lambda b, i: (b, i, 0)),
                  vec(D), vec(D), _mod_spec(l, D),
                  _const_spec((None, D, F), lambda b, i: (l, 0, 0)),
                  _const_spec((None, F, D), lambda b, i: (l, 0, 0))],
        out_specs=pl.BlockSpec((1, tm, D), lambda b, i: (b, i, 0)),
        out_shape=jax.ShapeDtypeStruct((B, T, D), F32),
        compiler_params=_cparams(("parallel", "parallel")),
        name="mlp",
    )(xs, prm["norm_pre_mlp"], prm["norm_post_mlp"], mods, prm["w_up"], prm["w_down"])


def kernel(x, c, ctx, c_ctx, ada_w, ada_b, norm_pre_mix, norm_post_mix, norm_pre_mlp, norm_post_mlp,
           w_in, diff_lam_q, diff_lam_k, diff_subln, rwkv_mu, rwkv_w0, rwkv_w2, rwkv_a0, rwkv_a2,
           rwkv_g2, rwkv_kk, rwkv_ka, rwkv_rk, rwkv_lnx_g, rwkv_lnx_b, ret_decay, ret_gn, win_sink,
           w_branch, w_out, w_up, w_down):
    B, S, D = x.shape
    n_ctx = ctx.shape[1]
    T = n_ctx + S
    L = ada_w.shape[0]
    W = BRANCH_W
    TM = 256
    TD = next(t for t in DENSE_TILES if T % t == 0)
    assert n_ctx % TM == 0 and S % TM == 0 and S % GRID_W == 0 and B + 1 <= 8

    cc = jnp.zeros((8, D), F32).at[:B].set(c).at[B].set(c_ctx)
    mod = _ada_mod(cc, ada_w, ada_b).reshape(L, 8, 6, D)
    mod_ctx = jnp.broadcast_to(mod[:, B][:, None], (L, B, 6, D))
    mods = jnp.stack([mod_ctx, mod[:, :B]], axis=2)
    mods = jnp.pad(mods, ((0, 0), (0, 0), (0, 0), (0, 2), (0, 0)))

    lat = jnp.arange(S, dtype=jnp.int32)
    row = (lat // GRID_W).astype(F32)
    col = (lat % GRID_W).astype(F32)
    tabs_a = _rope_tables([row, col], W, HEAD_W // 2, n_ctx)
    tabs_d = _rope_tables([row, col], W, HEAD_W, n_ctx)
    tabs_c = _rope_tables([lat.astype(F32)], W // 2, HEAD_W // 2, n_ctx)

    sizes = (3 * W, rwkv_mu.shape[-1], 3 * W, 2 * W, 4 * D)
    offs = np.concatenate([[0], np.cumsum(sizes)])
    w_groups = [w_in[:, :, int(offs[n]):int(offs[n + 1])].astype(BF16) for n in range(5)]

    lw_w, la_w = rwkv_w2.shape[2], rwkv_a2.shape[2]
    zw = jnp.zeros((L, W, 2 * W), F32)
    w2cat = (zw.at[:, 0:lw_w, 0:W].set(rwkv_w2[:, 0])
             .at[:, lw_w:2 * lw_w, W:2 * W].set(rwkv_w2[:, 1]))
    a2cat = (zw.at[:, 2 * lw_w:2 * lw_w + la_w, 0:W].set(rwkv_a2[:, 0])
             .at[:, 2 * lw_w + la_w:2 * lw_w + 2 * la_w, W:2 * W].set(rwkv_a2[:, 1]))
    blk = np.arange(W) // HEAD_W
    ones_bd = (blk[:, None] == blk[None, :]).astype(np.float32)
    ones_bd = jnp.asarray(np.concatenate([ones_bd, ones_bd], axis=0), BF16)
    v3 = lambda a: a.reshape(L, 1, -1)

    def hl3(w):
        hi = w.astype(BF16)
        return jnp.concatenate([hi, (w - hi.astype(F32)).astype(BF16), hi], axis=-2)

    prm = dict(mu=v3(rwkv_mu), kk=v3(rwkv_kk), ka=v3(rwkv_ka), rk=v3(rwkv_rk), w0=rwkv_w0, a0=rwkv_a0,
               w2=hl3(w2cat), a2=hl3(a2cat), g2=hl3(rwkv_g2), ones_bd=ones_bd,
               lnx_g=v3(rwkv_lnx_g), lnx_b=v3(rwkv_lnx_b), ret_gn=v3(ret_gn),
               w_branch=w_branch.astype(BF16), w_out=w_out.astype(BF16),
               norm_post_mix=v3(norm_post_mix), norm_pre_mlp=v3(norm_pre_mlp),
               norm_post_mlp=v3(norm_post_mlp), w_up=w_up.astype(BF16), w_down=w_down.astype(BF16))
    subln = diff_subln.reshape(L, N_HEADS, HEAD_W, 1)
    decay = ret_decay.reshape(L, 2, 1, N_HEADS)
    TQ_D = 256
    sink_col = jnp.broadcast_to(win_sink.reshape(L, 2, 1, 2, 1), (L, 2, 1, 2, TQ_D)).reshape(L, 2, 1, 2 * TQ_D)
    gain_pre = v3(norm_pre_mix)

    xs = jnp.concatenate([ctx, x], axis=1)
    for l in range(L):
        qa, ka, va, cols_b, cols_c, qd, kd, vd, gates = _input_proj(
            xs, gain_pre, mods, w_groups, tabs_a, tabs_d, l, TD, n_ctx)
        ya = _diff_attention(qa, ka, va, diff_lam_q, diff_lam_k, subln, l, ATT_TQ, ATT_TK, ATT_UNROLL, n_ctx)
        rvk, dirs, aux = _rwkv_prep(cols_b, prm, l, TM, n_ctx)
        yb = _rwkv_scan(rvk, dirs, n_ctx)
        oc = _retention(cols_c, tabs_c, decay, l, n_ctx)
        yd = _window_attention(qd, kd, vd, sink_col, l, TQ_D, n_ctx)
        xs = _merge(xs, gates, ya, yb, aux, oc, cols_c, yd, prm, mods, l, TD, n_ctx)
        xs = _mlp(xs, prm, mods, l, TD, n_ctx)
    return xs[:, n_ctx:]
```

```python
import functools
import math

import numpy as np
import jax
import jax.numpy as jnp
from jax import lax
from jax.experimental import pallas as pl
from jax.experimental.pallas import tpu as pltpu

F32 = jnp.float32
BF16 = jnp.bfloat16
HI = lax.Precision.HIGHEST

GRID_W = 64
ROPE_BASE = 10000.0
NEG_INF = -1e30
WINDOW = 128
N_HEADS = 4
HEAD_W = 64
BRANCH_W = N_HEADS * HEAD_W
RW_CHUNK = 64
RW_SUB = 4
RET_CHUNK = 128
RET_SUB = 2
RW_GN_EPS = 64e-5
ATT_TQ, ATT_TK, ATT_UNROLL = 256, 256, 64
ATT_LOOKAHEAD, ATT_SLOTS = 3, 8
DENSE_TILES = (640, 512, 384, 256, 128)
MLP_SPLIT = 2
VMEM_LIMIT = 56 * 1024 * 1024


def _cparams(sem, vmem=None):
    return pltpu.CompilerParams(dimension_semantics=sem, vmem_limit_bytes=vmem or VMEM_LIMIT)


def _dotf(a, b):
    return jnp.dot(a, b, precision=HI, preferred_element_type=F32)


def _dot_nt(a, b, precision=None):
    return lax.dot_general(a, b, (((1,), (1,)), ((), ())), precision=precision,
                           preferred_element_type=F32)


def _dot_tn(a, b, precision=None):
    return lax.dot_general(a, b, (((0,), (0,)), ((), ())), precision=precision,
                           preferred_element_type=F32)


def _split(x):
    hi = x.astype(BF16)
    return hi, (x - hi.astype(F32)).astype(BF16)


def _dot_hl(x, w3):
    hi, lo = _split(x)
    return jnp.dot(jnp.concatenate([hi, hi, lo], axis=1), w3, preferred_element_type=F32)


def _group_sum(x, ones2):
    hi, lo = _split(x)
    return jnp.dot(jnp.concatenate([hi, lo], axis=1), ones2, preferred_element_type=F32)


def _sdot(a, b):
    return jnp.dot(a.astype(BF16), b.astype(BF16), preferred_element_type=F32)


def _sdot_tn(a, b):
    return _dot_tn(a.astype(BF16), b.astype(BF16))


def _sigmoid(x):
    return 0.5 * jnp.tanh(0.5 * x) + 0.5


def _softplus(x):
    return jnp.maximum(x, 0.0) + jnp.log1p(jnp.exp(-jnp.abs(x)))


def _rms(x, eps):
    return x * lax.rsqrt(jnp.mean(x * x, axis=-1, keepdims=True) + eps)


def _const_spec(shape, index):
    return pl.BlockSpec(shape, index, pipeline_mode=pl.Buffered(1))


def _ada_kernel(c_ref, w_ref, b_ref, o_ref):
    c = c_ref[...]
    o_ref[0] = _dotf(c * _sigmoid(c), w_ref[0]) + b_ref[0]


def _ada_mod(cc, ada_w, ada_b):
    L, D, N = ada_w.shape
    tn = N // 4
    return pl.pallas_call(
        _ada_kernel,
        grid=(L, N // tn),
        in_specs=[pl.BlockSpec((8, D), lambda l, n: (0, 0)),
                  pl.BlockSpec((1, D, tn), lambda l, n: (l, 0, n)),
                  pl.BlockSpec((1, 1, tn), lambda l, n: (l, 0, n))],
        out_specs=pl.BlockSpec((1, 8, tn), lambda l, n: (l, 0, n)),
        out_shape=jax.ShapeDtypeStruct((L, 8, N), F32),
        compiler_params=_cparams(("arbitrary", "arbitrary")),
        name="ada_mod",
    )(cc, ada_w, ada_b.reshape(L, 1, N))


def _mod_row(mod_ref, j, tm, n_ctx):
    tok = pl.program_id(1) * tm + lax.broadcasted_iota(jnp.int32, (tm, 1), 0)
    return jnp.where(tok < n_ctx, mod_ref[0, j:j + 1, :], mod_ref[1, j:j + 1, :])


def _mod_spec(l, D):
    return pl.BlockSpec((None, None, 2, 8, D), lambda b, i: (l, b, 0, 0, 0))


def _proj_kernel(x_ref, g_ref, mod_ref, wa_ref, wb_ref, wc_ref, wd_ref, wg_ref,
                 cosa_ref, sma_ref, spa_ref, cosd_ref, smd_ref, spd_ref,
                 qa_ref, ka_ref, va_ref, cb_ref, cc_ref, qd_ref, kd_ref, vd_ref, gates_ref, *,
                 tm, n_ctx, half_a, half_d):
    h = _rms(x_ref[0], 1e-6) * g_ref[...]
    h = h * (1.0 + _mod_row(mod_ref, 1, tm, n_ctx)) + _mod_row(mod_ref, 0, tm, n_ctx)
    hb = h.astype(BF16)
    proj = lambda w_ref: jnp.dot(hb, w_ref[...], preferred_element_type=F32)
    _store_diff_qkv(proj(wa_ref), cosa_ref[...], sma_ref[...], spa_ref[...], half_a, qa_ref, ka_ref, va_ref)
    cb_ref[0] = proj(wb_ref)
    cc_ref[0] = proj(wc_ref)
    _store_win_qkv(proj(wd_ref), cosd_ref[...], smd_ref[...], spd_ref[...], half_d, qd_ref, kd_ref, vd_ref)
    gates_ref[0] = proj(wg_ref).astype(gates_ref.dtype)


def _input_proj(xs, gain, mods, weights, tabs_a, tabs_d, l, tm, n_ctx):
    B, T, D = xs.shape
    W = BRANCH_W
    tok = lambda n: pl.BlockSpec((1, tm, n), lambda b, i: (b, i, 0))
    tab = pl.BlockSpec((tm, W), lambda b, i: (i, 0))
    in_specs = [tok(D), pl.BlockSpec((None, 1, D), lambda b, i: (l, 0, 0)), _mod_spec(l, D)]
    in_specs += [_const_spec((None, D, w.shape[-1]), lambda b, i: (l, 0, 0)) for w in weights]
    in_specs += [tab] * 6
    heads = lambda *lead: pl.BlockSpec((1,) + lead + (tm, HEAD_W), lambda b, i: (b,) + (0,) * len(lead) + (i, 0))
    heads_t = lambda n: pl.BlockSpec((1, n, HEAD_W, tm), lambda b, i: (b, 0, 0, i))
    qa_spec = pl.BlockSpec((1, N_HEADS, 2, HEAD_W, tm), lambda b, i: (b, 0, 0, 0, i))
    out_specs = [qa_spec, heads(N_HEADS), heads_t(N_HEADS), tok(weights[1].shape[-1]),
                 tok(weights[2].shape[-1]), pl.BlockSpec((1, W, tm), lambda b, i: (b, 0, i)),
                 heads(2), heads_t(2), tok(weights[4].shape[-1])]
    sds = jax.ShapeDtypeStruct
    out_shape = [sds((B, N_HEADS, 2, HEAD_W, T), BF16), sds((B, N_HEADS, T, HEAD_W), BF16),
                 sds((B, N_HEADS, HEAD_W, T), BF16), sds((B, T, weights[1].shape[-1]), F32),
                 sds((B, T, weights[2].shape[-1]), F32), sds((B, W, T), BF16),
                 sds((B, 2, T, HEAD_W), BF16), sds((B, 2, HEAD_W, T), BF16),
                 sds((B, T, weights[4].shape[-1]), BF16)]
    return pl.pallas_call(
        functools.partial(_proj_kernel, tm=tm, n_ctx=n_ctx, half_a=tabs_a[3], half_d=tabs_d[3]),
        grid=(B, T // tm),
        in_specs=in_specs, out_specs=out_specs, out_shape=out_shape,
        compiler_params=_cparams(("parallel", "parallel")),
        name="input_proj",
    )(xs, gain, mods, *weights, *tabs_a[:3], *tabs_d[:3])


def _rope_tables(pos_sets, lanes, group, n_ctx):
    n_sets = len(pos_sets)
    sub = group // n_sets
    half = sub // 2
    angs = []
    for pos in pos_sets:
        inv_freq = ROPE_BASE ** (-jnp.arange(half, dtype=F32) / half)
        ang = pos[:, None] * inv_freq[None, :]
        angs.append(jnp.concatenate([ang, ang], axis=-1))
    ang = jnp.concatenate(angs, axis=-1)
    ang = jnp.tile(ang, (1, lanes // group))
    ang = jnp.concatenate([jnp.zeros((n_ctx, lanes), F32), ang], axis=0)
    first = (np.arange(lanes) % sub) < half
    cos, sin = jnp.cos(ang), jnp.sin(ang)
    sin_minus = jnp.where(first[None, :], -sin, 0.0)
    sin_plus = jnp.where(first[None, :], 0.0, sin)
    return cos, sin_minus, sin_plus, half


def _rope(x, cos, sin_minus, sin_plus, half):
    n = x.shape[-1]
    return x * cos + pltpu.roll(x, n - half, 1) * sin_minus + pltpu.roll(x, half, 1) * sin_plus


def _store_diff_qkv(c, cos, sm, sp, half, q_ref, k_ref, v_ref):
    scale = (HEAD_W // 2) ** -0.5 * math.log2(math.e)
    q = _rope(c[:, 0:BRANCH_W], cos, sm, sp, half) * scale
    k = _rope(c[:, BRANCH_W:2 * BRANCH_W], cos, sm, sp, half)
    v = c[:, 2 * BRANCH_W:3 * BRANCH_W]
    lane = lax.broadcasted_iota(jnp.int32, q.shape, 1)
    comp = (lane >> 5) & 1
    q0 = jnp.where(comp == 0, q, 0.0).T.astype(BF16)
    q1 = jnp.where(comp == 1, q, 0.0).T.astype(BF16)
    kb, vt = k.astype(BF16), v.T.astype(BF16)
    for h in range(N_HEADS):
        sl = slice(h * HEAD_W, (h + 1) * HEAD_W)
        q_ref[0, h, 0] = q0[sl, :]
        q_ref[0, h, 1] = q1[sl, :]
        k_ref[0, h] = kb[:, sl]
        v_ref[0, h] = vt[sl, :]


def _flash_kernel(lq_ref, lk_ref, g_ref, q_ref, k_ref, vt_ref, o_ref, s_sc, acc_sc, *,
                  tq, tk, unroll, n_ctx, T, lam_init):
    qi = pl.program_id(2)
    qt = jnp.concatenate([q_ref[0, 0, 0], q_ref[0, 0, 1]], axis=1)

    def scores(slot, off, size):
        s_sc[slot, 0:size] = jnp.dot(k_ref[0, 0, pl.ds(off, size), :], qt, preferred_element_type=F32)

    def absorb(slot, off, size, m):
        s = s_sc[slot, 0:size]
        m_new = jnp.maximum(m, jnp.max(s, axis=0, keepdims=True))
        p = jnp.exp2(s - m_new).astype(BF16)
        vt = jnp.concatenate([vt_ref[0, 0, :, pl.ds(off, size)], jnp.ones((8, size), BF16)], axis=0)
        acc_sc[...] = acc_sc[...] * jnp.exp2(m - m_new) + jnp.dot(vt, p, preferred_element_type=F32)
        return m_new

    n_slots, look = s_sc.shape[0], ATT_LOOKAHEAD
    lat_off = lambda c: n_ctx + (c - 1) * tk
    acc_sc[...] = jnp.zeros(acc_sc.shape, F32)
    m = jnp.full((1, 2 * tq), NEG_INF, F32)
    scores(0, 0, n_ctx)
    for c in range(1, look + 1):
        scores(c % n_slots, min(lat_off(c), T - tk), tk)
    m = absorb(0, 0, n_ctx, m)

    def body(jj, m):
        c0 = 1 + jj * unroll
        for u in range(unroll):
            off = pl.multiple_of(lat_off(c0 + u), math.gcd(n_ctx, tk))
            nxt = pl.multiple_of(jnp.minimum(lat_off(c0 + u + look), T - tk), math.gcd(n_ctx, tk))
            scores((1 + u + look) % n_slots, nxt, tk)
            m = absorb((1 + u) % n_slots, off, tk, m)
        return m

    n_trips = (T - n_ctx) // (unroll * tk)
    lax.fori_loop(0, jnp.where(qi * tq < n_ctx, 0, n_trips), body, m)
    acc = acc_sc[...]
    o = acc[0:HEAD_W] / acc[HEAD_W:HEAD_W + 1]
    e0 = jnp.exp(jnp.sum(lq_ref[0:1, :] * lk_ref[0:1, :], axis=1, keepdims=True))
    e1 = jnp.exp(jnp.sum(lq_ref[1:2, :] * lk_ref[1:2, :], axis=1, keepdims=True))
    lam = e0 - e1 + lam_init
    y = o[:, :tq] - lam * o[:, tq:]
    o_ref[0] = y * lax.rsqrt(jnp.mean(y * y, axis=0, keepdims=True) + 1e-5) * g_ref[...] * (1.0 - lam_init)


def _diff_attention(q, k, v, lam_q, lam_k, subln, l, tq, tk, unroll, n_ctx):
    B, H, _, _, T = q.shape
    dqk = lam_q.shape[-1]
    lam_init = 0.8 - 0.6 * math.exp(-0.3 * l)
    unroll = math.gcd(unroll, (T - n_ctx) // tk)
    assert (unroll % ATT_SLOTS == 0 or unroll * tk == T - n_ctx) and ATT_LOOKAHEAD < ATT_SLOTS
    assert (T - n_ctx) % (unroll * tk) == 0 and n_ctx % tq == 0 and n_ctx % 128 == 0 and tk % 128 == 0
    kern = functools.partial(_flash_kernel, tq=tq, tk=tk, unroll=unroll, n_ctx=n_ctx, T=T,
                             lam_init=lam_init)
    return pl.pallas_call(
        kern,
        grid=(B, H, T // tq),
        in_specs=[pl.BlockSpec((None, 2, dqk), lambda b, h, i: (l, 0, 0)),
                  pl.BlockSpec((None, 2, dqk), lambda b, h, i: (l, 0, 0)),
                  pl.BlockSpec((None, None, HEAD_W, 1), lambda b, h, i: (l, h, 0, 0)),
                  pl.BlockSpec((1, 1, 2, HEAD_W, tq), lambda b, h, i: (b, h, 0, 0, i)),
                  pl.BlockSpec((1, 1, T, HEAD_W), lambda b, h, i: (b, h, 0, 0)),
                  pl.BlockSpec((1, 1, HEAD_W, T), lambda b, h, i: (b, h, 0, 0))],
        out_specs=pl.BlockSpec((1, HEAD_W, tq), lambda b, h, i: (b, h, i)),
        out_shape=jax.ShapeDtypeStruct((B, H * HEAD_W, T), F32),
        scratch_shapes=[pltpu.VMEM((ATT_SLOTS, max(tk, n_ctx), 2 * tq), F32),
                        pltpu.VMEM((HEAD_W + 8, 2 * tq), F32)],
        compiler_params=_cparams(("parallel", "parallel", "arbitrary")),
        name="diff_attention",
    )(lam_q, lam_k, subln, q, k, v)


def _rwprep_kernel(c_ref, p_ref, n_ref, mu_ref, kk_ref, ka_ref, rk_ref, w0_ref, a0_ref,
                   w2_ref, a2_ref, g2_ref, ones_ref, rvk_ref, dir_ref, aux_ref, *, tm, n_ctx, T):
    i = pl.program_id(1)
    c = c_ref[0]
    t0 = i * tm
    seg_start = jnp.logical_or(t0 == 0, t0 == n_ctx)
    seg_end = jnp.logical_or(t0 + tm == n_ctx, t0 + tm == T)
    pv = jnp.where(seg_start, 0.0, p_ref[0][7:8, :])
    nx = jnp.where(seg_end, 0.0, n_ref[0][0:1, :])
    row = lax.broadcasted_iota(jnp.int32, c.shape, 0)
    prev = jnp.where(row == 0, pv, pltpu.roll(c, 1, 0))
    nxt = jnp.where(row == tm - 1, nx, pltpu.roll(c, tm - 1, 0))
    xs = c + (0.5 * (prev + nxt) - c) * mu_ref[...]
    W = BRANCH_W
    r, k, v = xs[:, 0:W], xs[:, W:2 * W], xs[:, 2 * W:3 * W]
    lo, gl = xs[:, 3 * W:4 * W], xs[:, 4 * W:]
    ones2 = ones_ref[...]
    kk = k * kk_ref[...]
    kk = kk * lax.rsqrt(jnp.maximum(_group_sum(kk * kk, ones2), 1e-12))
    wpre = _dot_hl(jnp.tanh(lo), w2_ref[...])
    apre = _dot_hl(lo, a2_ref[...])
    gate = _dot_hl(_sigmoid(gl), g2_ref[...])
    rb = jnp.zeros_like(r)
    for d in range(2):
        z = w0_ref[d:d + 1, :] + wpre[:, d * W:(d + 1) * W]
        w = -_softplus(-z) - 0.5
        a = _sigmoid(a0_ref[d:d + 1, :] + apre[:, d * W:(d + 1) * W])
        kd = k * (1.0 + (a - 1.0) * ka_ref[...])
        dir_ref[0, d, :, 0:W] = -jnp.exp(w)
        dir_ref[0, d, :, W:2 * W] = kd
        dir_ref[0, d, :, 2 * W:3 * W] = kk * a
        rb = rb + r * kd * rk_ref[...]
    rvk_ref[0, :, 0:W] = r
    rvk_ref[0, :, W:2 * W] = v
    rvk_ref[0, :, 2 * W:3 * W] = kk
    aux_ref[0, :, 0:W] = _group_sum(rb, ones2) * v
    aux_ref[0, :, W:2 * W] = gate


def _rwkv_prep(cols, prm, l, tm, n_ctx):
    B, T, NB = cols.shape
    W = BRANCH_W
    nblk8 = T // 8
    r8 = tm // 8
    vec = lambda n: pl.BlockSpec((None, 1, n), lambda b, i: (l, 0, 0))
    pair = lambda n: pl.BlockSpec((None, 2, n), lambda b, i: (l, 0, 0))
    mat = lambda m, n: _const_spec((None, m, n), lambda b, i: (l, 0, 0))
    return pl.pallas_call(
        functools.partial(_rwprep_kernel, tm=tm, n_ctx=n_ctx, T=T),
        grid=(B, T // tm),
        in_specs=[pl.BlockSpec((1, tm, NB), lambda b, i: (b, i, 0)),
                  pl.BlockSpec((1, 8, NB), lambda b, i: (b, jnp.maximum(i * r8 - 1, 0), 0)),
                  pl.BlockSpec((1, 8, NB), lambda b, i: (b, jnp.minimum((i + 1) * r8, nblk8 - 1), 0)),
                  vec(NB), vec(W), vec(W), vec(W), pair(W), pair(W),
                  mat(3 * W, 2 * W), mat(3 * W, 2 * W), mat(3 * (NB - 4 * W), W),
                  _const_spec((2 * W, W), lambda b, i: (0, 0))],
        out_specs=[pl.BlockSpec((1, tm, 3 * W), lambda b, i: (b, i, 0)),
                   pl.BlockSpec((1, 2, tm, 3 * W), lambda b, i: (b, 0, i, 0)),
                   pl.BlockSpec((1, tm, 2 * W), lambda b, i: (b, i, 0))],
        out_shape=[jax.ShapeDtypeStruct((B, T, 3 * W), F32),
                   jax.ShapeDtypeStruct((B, 2, T, 3 * W), F32),
                   jax.ShapeDtypeStruct((B, T, 2 * W), F32)],
        compiler_params=_cparams(("parallel", "parallel")),
        name="rwkv_prep",
    )(cols, cols, cols, prm["mu"], prm["kk"], prm["ka"], prm["rk"], prm["w0"], prm["a0"],
      prm["w2"], prm["a2"], prm["g2"], prm["ones_bd"])


def _rw_chunks(chunks, rounds, state_of, states):
    C, W = RW_CHUNK, BRANCH_W
    each = lambda f, *cols: [f(*a) for a in zip(*cols)]
    r, v, kk, lw, kd, b, sgn = (list(col) for col in zip(*chunks))

    ti = lax.broadcasted_iota(jnp.int32, (C, C), 0)
    tj = lax.broadcasted_iota(jnp.int32, (C, C), 1)
    ri = lax.broadcasted_iota(jnp.int32, (W, W), 0)
    ci = lax.broadcasted_iota(jnp.int32, (W, W), 1)
    same_head = (ri >> 6) == (ci >> 6)
    bd = lambda x: jnp.where(same_head, jnp.concatenate([x] * N_HEADS, axis=0), 0.0)
    row = lax.broadcasted_iota(jnp.int32, (C, W), 0)
    col = lax.broadcasted_iota(jnp.int32, (C, W), 1) & (C - 1)
    masks = {sg: (jnp.where((tj - ti) * sg <= 0, 1.0, 0.0), (col - row) * sg < 0, (col - row) * sg <= 0)
             for sg in set(sgn)}
    tri = [masks[sg][0] for sg in sgn]
    strict = [masks[sg][1] for sg in sgn]
    incl = [masks[sg][2] for sg in sgn]
    blk16 = (row >> 4) == (col >> 4)
    blk32 = (row >> 5) == (col >> 5)
    eye = jnp.where(row == col, 1.0, 0.0)

    log_g = each(_dotf, tri, lw)
    log_gc = [jnp.sum(x, axis=0, keepdims=True) for x in lw]
    kk_t = each(lambda kk_, lg, lw_: kk_ * jnp.exp(lg - lw_), kk, log_g, lw)
    r_t = each(lambda r_, lg: r_ * jnp.exp(lg), r, log_g)
    g_inv = [jnp.exp(-lg) for lg in log_g]
    g_tail = each(lambda lgc, lg: jnp.exp(lgc - lg), log_gc, log_g)

    bdb = lambda x: bd(x.astype(BF16))
    left = each(lambda a, c: jnp.concatenate([a, c], axis=0).astype(BF16), kk_t, r_t)
    right = each(lambda b_, kd_, gi: jnp.concatenate([bdb(b_ * gi), bdb(kd_ * gi)], axis=0), b, kd, g_inv)
    g = each(_dot_nt, left, right)
    m_sl = each(lambda st, g_: jnp.where(st, g_[:C, :W], 0.0), strict, g)
    n_sl = each(lambda st, g_: jnp.where(st, g_[:C, W:], 0.0), strict, g)
    pq_l = each(lambda ic, g_: jnp.concatenate([jnp.where(ic, g_[C:, :W], 0.0),
                                                jnp.where(ic, g_[C:, W:], 0.0)], axis=1).astype(BF16), incl, g)

    v_bd = [bdb(x) for x in v]
    nv = each(_sdot, n_sl, v_bd)
    b_tail = each(lambda b_, kd_, gt: jnp.concatenate([b_ * gt, kd_ * gt], axis=0).astype(BF16), b, kd, g_tail)
    gc_col = [jnp.sum(jnp.where(ri == ci, jnp.exp(lgc), 0.0), axis=1, keepdims=True) for lgc in log_gc]

    d0 = [jnp.where(blk16, m_, 0.0) for m_ in m_sl]
    sq = lambda x: _sdot(x, bdb(x))
    a2 = each(sq, d0)
    a4 = each(sq, a2)
    a8 = each(sq, a4)
    t = [eye - d_ for d_ in d0]
    for a_ in (a2, a4, a8):
        t = each(lambda t_, x: t_ + _sdot(t_, bdb(x)), t, a_)
    for cm in (jnp.logical_and(blk32, jnp.logical_not(blk16)), jnp.logical_not(blk32)):
        u = each(lambda m_, t_: _sdot(jnp.where(cm, m_, 0.0), bdb(t_)), m_sl, t)
        t = each(lambda t_, u_: t_ - _sdot(t_, bdb(u_)), t, u)

    states = list(states)
    ys = [None] * len(chunks)
    for ids in rounds:
        pick = lambda xs: [xs[i] for i in ids]
        s0 = [states[state_of[i]] for i in ids]
        g2 = each(_sdot, pick(left), s0)
        rhs = each(lambda g2_, nv_: -(g2_[:C] + nv_), g2, pick(nv))
        sa = each(lambda t_, rh: _sdot(t_, bdb(rh)), pick(t), rhs)
        y = each(lambda g2_, pq, sa_, vb: g2_[C:] + _sdot(pq, jnp.concatenate([bdb(sa_), vb], axis=0)),
                 g2, pick(pq_l), sa, pick(v_bd))
        ds = each(lambda bt, sa_, v_: _sdot_tn(bt, jnp.concatenate([sa_, v_], axis=0)),
                  pick(b_tail), sa, pick(v))
        for i, y_, s_, ds_ in zip(ids, y, s0, ds):
            ys[i] = y_
            states[state_of[i]] = jnp.where(same_head, s_ * gc_col[i] + ds_, 0.0)
    return ys, states


def _rwscan_kernel(rvf_ref, rvb_ref, df_ref, db_ref, yf_ref, yb_ref, s_ref, *, n_sub):
    C, W = RW_CHUNK, BRANCH_W

    @pl.when(pl.program_id(0) == 0)
    def _():
        s_ref[...] = jnp.zeros(s_ref.shape, F32)

    refs = ((rvf_ref, df_ref, yf_ref), (rvb_ref, db_ref, yb_ref))
    chunks, where, state_of = [], [], []
    rounds = [[] for _ in range(n_sub)]
    for bi in range(rvf_ref.shape[0]):
        for d, (rv, dr, _) in enumerate(refs):
            for k in range(n_sub):
                rows = pl.ds((k if d == 0 else n_sub - 1 - k) * C, C)
                rounds[k].append(len(chunks))
                state_of.append(2 * bi + d)
                where.append((d, bi, rows))
                chunks.append((rv[bi, rows, 0:W], rv[bi, rows, W:2 * W], rv[bi, rows, 2 * W:3 * W],
                               dr[bi, 0, rows, 0:W], dr[bi, 0, rows, W:2 * W], dr[bi, 0, rows, 2 * W:3 * W],
                               1 - 2 * d))
    n_states = s_ref.shape[0]
    ys, states = _rw_chunks(chunks, rounds, state_of, [s_ref[i] for i in range(n_states)])
    for (d, bi, rows), y in zip(where, ys):
        refs[d][2][bi, rows, :] = y
    for i in range(n_states):
        s_ref[i] = states[i]


def _chunk_order(s, d, n_ctx_chunks, n_chunks):
    bwd = jnp.where(s < n_ctx_chunks, n_ctx_chunks - 1 - s, n_chunks - 1 - (s - n_ctx_chunks))
    return jnp.where(d == 0, s, bwd)


def _rwkv_scan(rvk, dirs, n_ctx):
    B, T, _ = rvk.shape
    W = BRANCH_W
    C = RW_CHUNK * RW_SUB
    assert T % C == 0 and n_ctx % C == 0
    nc, ncx = T // C, n_ctx // C
    ch = lambda s, d: _chunk_order(s, d, ncx, nc)
    rv_spec = lambda d: pl.BlockSpec((B, C, 3 * W), lambda s: (0, ch(s, d), 0))
    dir_spec = lambda d: pl.BlockSpec((B, 1, C, 3 * W), lambda s: (0, d, ch(s, d), 0))
    y_spec = lambda d: pl.BlockSpec((B, C, W), lambda s: (0, ch(s, d), 0))
    return pl.pallas_call(
        functools.partial(_rwscan_kernel, n_sub=RW_SUB),
        grid=(nc,),
        in_specs=[rv_spec(0), rv_spec(1), dir_spec(0), dir_spec(1)],
        out_specs=[y_spec(0), y_spec(1)],
        out_shape=[jax.ShapeDtypeStruct((B, T, W), F32)] * 2,
        scratch_shapes=[pltpu.VMEM((2 * B, W, W), F32)],
        compiler_params=_cparams(("arbitrary",)),
        name="rwkv_scan",
    )(rvk, rvk, dirs, dirs)


def _ret_chunk(c, cos, sm, sp, dec, d, half, scale):
    C, W = RET_CHUNK, BRANCH_W
    QW = W // 2
    q = _rope(c[:, 0:QW], cos, sm, sp, half)
    k = _rope(c[:, QW:2 * QW], cos, sm, sp, half) * scale
    v = c[:, 2 * QW:2 * QW + W]
    log_g = -_softplus(-dec)
    lane_q = lax.broadcasted_iota(jnp.int32, (1, QW), 1) >> 5
    lane_v = lax.broadcasted_iota(jnp.int32, (1, W), 1) >> 6
    lg_q = jnp.zeros((1, QW), F32)
    lg_v = jnp.zeros((1, W), F32)
    for h in range(N_HEADS):
        lg_q = jnp.where(lane_q == h, log_g[:, h:h + 1], lg_q)
        lg_v = jnp.where(lane_v == h, log_g[:, h:h + 1], lg_v)
    idx = lax.broadcasted_iota(jnp.int32, (C, 1), 0).astype(F32)
    q_dec = jnp.exp((idx + 1.0 if d == 0 else C - idx) * lg_q)
    k_dec = jnp.exp((C - 1.0 - idx if d == 0 else idx) * lg_q)
    ti = lax.broadcasted_iota(jnp.int32, (C, C), 0)
    tj = lax.broadcasted_iota(jnp.int32, (C, C), 1)
    rel = ((ti - tj) * (1 - 2 * d)).astype(F32)

    kb = k.astype(BF16)
    vb = v.astype(BF16)
    o = jnp.zeros((C, W), F32)
    for h in range(N_HEADS):
        dm = jnp.where(rel >= 0, jnp.exp(jnp.maximum(rel, 0.0) * log_g[:, h:h + 1]), 0.0)
        qh = jnp.where(lane_q == h, q, 0.0).astype(BF16)
        sc = _dot_nt(qh, kb) * dm
        o = o + jnp.where(lane_v == h, jnp.dot(sc.astype(BF16), vb, preferred_element_type=F32), 0.0)

    ri = lax.broadcasted_iota(jnp.int32, (QW, W), 0) >> 5
    ci = lax.broadcasted_iota(jnp.int32, (QW, W), 1) >> 6
    upd = jnp.where(ri == ci, _dot_tn((k * k_dec).astype(BF16), vb), 0.0)
    return o, (q * q_dec).astype(BF16), upd, jnp.exp(C * lg_v)


def _ret_kernel(cf_ref, cb_ref, cosf_ref, smf_ref, spf_ref, cosb_ref, smb_ref, spb_ref, dec_ref,
                of_ref, ob_ref, s_ref, *, half, scale):
    C = RET_CHUNK

    @pl.when(pl.program_id(0) == 0)
    def _():
        s_ref[...] = jnp.zeros(s_ref.shape, F32)

    dirs = ((cf_ref, (cosf_ref, smf_ref, spf_ref), of_ref), (cb_ref, (cosb_ref, smb_ref, spb_ref), ob_ref))
    for bi in range(cf_ref.shape[0]):
        for d, (c_ref, tabs, o_ref) in enumerate(dirs):
            order = [pl.ds((k if d == 0 else RET_SUB - 1 - k) * C, C) for k in range(RET_SUB)]
            parts = [_ret_chunk(c_ref[bi, rows, :], tabs[0][rows, :], tabs[1][rows, :], tabs[2][rows, :],
                                dec_ref[d], d, half, scale) for rows in order]
            s = s_ref[2 * bi + d]
            for rows, (o, qd, upd, cdec) in zip(order, parts):
                o_ref[bi, rows, :] = o + jnp.dot(qd, s.astype(BF16), preferred_element_type=F32)
                s = s * cdec + upd
            s_ref[2 * bi + d] = s


def _retention(cols, tabs, decay, l, n_ctx):
    B, T, NCOL = cols.shape
    C, W = RET_CHUNK * RET_SUB, BRANCH_W
    cos, sm, sp, half = tabs
    assert T % C == 0 and n_ctx % C == 0
    nc, ncx = T // C, n_ctx // C
    ch = lambda s, d: _chunk_order(s, d, ncx, nc)
    c_spec = lambda d: pl.BlockSpec((B, C, NCOL), lambda s: (0, ch(s, d), 0))
    tab_spec = lambda d: pl.BlockSpec((C, W // 2), lambda s: (ch(s, d), 0))
    o_spec = lambda d: pl.BlockSpec((B, C, W), lambda s: (0, ch(s, d), 0))
    return pl.pallas_call(
        functools.partial(_ret_kernel, half=half, scale=(HEAD_W // 2) ** -0.5),
        grid=(nc,),
        in_specs=[c_spec(0), c_spec(1), tab_spec(0), tab_spec(0), tab_spec(0),
                  tab_spec(1), tab_spec(1), tab_spec(1),
                  pl.BlockSpec((None, 2, 1, N_HEADS), lambda s: (l, 0, 0, 0))],
        out_specs=[o_spec(0), o_spec(1)],
        out_shape=[jax.ShapeDtypeStruct((B, T, W), F32)] * 2,
        scratch_shapes=[pltpu.VMEM((2 * B, W // 2, W), F32)],
        compiler_params=_cparams(("arbitrary",)),
        name="retention",
    )(cols, cols, cos, sm, sp, cos, sm, sp, decay)


def _store_win_qkv(c, cos, sm, sp, half, q_ref, k_ref, v_ref):
    W = BRANCH_W
    q_ref[0] = (_rope(c[:, 0:W], cos, sm, sp, half) * HEAD_W ** -0.5).T.astype(BF16)
    kw = W // 2
    k = _rope(c[:, W:W + kw], cos[:, :kw], sm[:, :kw], sp[:, :kw], half).astype(BF16)
    vt = c[:, W + kw:W + 2 * kw].T.astype(BF16)
    for kvh in range(2):
        k_ref[0, kvh] = k[:, kvh * HEAD_W:(kvh + 1) * HEAD_W]
        v_ref[0, kvh] = vt[kvh * HEAD_W:(kvh + 1) * HEAD_W, :]


def _winattn_kernel(sink_ref, q_ref, k_ref, v_ref, o_ref, *, tq, n_ctx, T):
    i = pl.program_id(1)
    band = tq + 2 * WINDOW
    ws = pl.multiple_of(jnp.clip(i * tq - WINDOW, n_ctx, T - band), WINDOW)
    kpos = ws + lax.broadcasted_iota(jnp.int32, (band, tq), 0)
    qpos = i * tq + lax.broadcasted_iota(jnp.int32, (band, tq), 1)
    valid = jnp.logical_and(jnp.abs(qpos - kpos) <= WINDOW, qpos >= n_ctx)
    valid = jnp.concatenate([valid, valid], axis=1)
    for kvh in range(k_ref.shape[1]):
        r0 = 2 * kvh * HEAD_W
        qt = jnp.concatenate([q_ref[0, r0:r0 + HEAD_W, :], q_ref[0, r0 + HEAD_W:r0 + 2 * HEAD_W, :]], axis=1)
        s_ctx = jnp.dot(k_ref[0, kvh, 0:n_ctx, :], qt, preferred_element_type=F32)
        s_band = jnp.dot(k_ref[0, kvh, pl.ds(ws, band), :], qt, preferred_element_type=F32)
        s_band = jnp.where(valid, s_band, NEG_INF)
        sink = sink_ref[kvh]
        m = jnp.maximum(jnp.maximum(jnp.max(s_ctx, axis=0, keepdims=True),
                                    jnp.max(s_band, axis=0, keepdims=True)), sink)
        p_ctx = jnp.exp(s_ctx - m).astype(BF16)
        p_band = jnp.exp(s_band - m).astype(BF16)
        vx = jnp.concatenate([v_ref[0, kvh, :, 0:n_ctx], jnp.ones((8, n_ctx), BF16)], axis=0)
        vb = jnp.concatenate([v_ref[0, kvh, :, pl.ds(ws, band)], jnp.ones((8, band), BF16)], axis=0)
        acc = jnp.dot(jnp.concatenate([vx, vb], axis=1), jnp.concatenate([p_ctx, p_band], axis=0),
                      preferred_element_type=F32)
        o = acc[0:HEAD_W] / (acc[HEAD_W:HEAD_W + 1] + jnp.exp(sink - m))
        for g in range(2):
            o_ref[0, r0 + g * HEAD_W:r0 + (g + 1) * HEAD_W, :] = o[:, g * tq:(g + 1) * tq]


def _window_attention(q, k, v, sink_col, l, tq, n_ctx):
    B, KVH, T, _ = k.shape
    G = q.shape[1] // (KVH * HEAD_W)
    return pl.pallas_call(
        functools.partial(_winattn_kernel, tq=tq, n_ctx=n_ctx, T=T),
        grid=(B, T // tq),
        in_specs=[pl.BlockSpec((None, KVH, 1, G * tq), lambda b, i: (l, 0, 0, 0)),
                  pl.BlockSpec((1, KVH * G * HEAD_W, tq), lambda b, i: (b, 0, i)),
                  pl.BlockSpec((1, KVH, T, HEAD_W), lambda b, i: (b, 0, 0, 0)),
                  pl.BlockSpec((1, KVH, HEAD_W, T), lambda b, i: (b, 0, 0, 0))],
        out_specs=pl.BlockSpec((1, KVH * G * HEAD_W, tq), lambda b, i: (b, 0, i)),
        out_shape=jax.ShapeDtypeStruct((B, KVH * G * HEAD_W, T), F32),
        compiler_params=_cparams(("parallel", "arbitrary")),
        name="window_attention",
    )(sink_col, q, k, v)


def _group_norm(x, ones2, eps, center):
    inv = 1.0 / HEAD_W
    if center:
        x = x - _group_sum(x, ones2) * inv
    return x * lax.rsqrt(_group_sum(x * x, ones2) * inv + eps)


def _merge_kernel(x_ref, gates_ref, ya_ref, ybf_ref, ybb_ref, aux_ref, ocf_ref, ocb_ref, gc_ref, yd_ref,
                  lnx_g_ref, lnx_b_ref, gn_ref, wb_ref, wo_ref, pn_ref, mod_ref, ones_ref, o_ref, *,
                  tm, n_ctx):
    W = BRANCH_W
    D = x_ref.shape[-1]
    ones_bd = ones_ref[...]
    yb = _group_norm(ybf_ref[0] + ybb_ref[0], ones_bd, RW_GN_EPS, True)
    yb = yb * lnx_g_ref[...] + lnx_b_ref[...]
    yb = (yb + aux_ref[0, :, 0:W]) * aux_ref[0, :, W:2 * W]
    yc = _group_norm(ocf_ref[0] + ocb_ref[0], ones_bd, 1e-5, True) * gn_ref[...]
    gc = gc_ref[0]
    yc = yc * (gc * _sigmoid(gc))

    lifted = [_dot_tn(ya_ref[0].astype(BF16), wb_ref[0]),
              jnp.dot(yb.astype(BF16), wb_ref[1], preferred_element_type=F32),
              jnp.dot(yc.astype(BF16), wb_ref[2], preferred_element_type=F32),
              _dot_tn(yd_ref[0].astype(BF16), wb_ref[3])]
    m = None
    for n in range(4):
        t = _sigmoid(gates_ref[0, :, n * D:(n + 1) * D].astype(F32)) * lifted[n]
        m = t if m is None else m + t
    out = jnp.dot(m.astype(BF16), wo_ref[...], preferred_element_type=F32)
    o_ref[0] = x_ref[0] + _mod_row(mod_ref, 2, tm, n_ctx) * (_rms(out, 1e-6) * pn_ref[...])


def _merge(xs, gates, ya, yb, aux, oc, cols_c, yd, prm, mods, l, tm, n_ctx):
    B, T, D = xs.shape
    W = BRANCH_W
    vec = lambda n: pl.BlockSpec((None, 1, n), lambda b, i: (l, 0, 0))
    tok = lambda n: pl.BlockSpec((1, tm, n), lambda b, i: (b, i, 0))
    feat = pl.BlockSpec((1, W, tm), lambda b, i: (b, 0, i))
    return pl.pallas_call(
        functools.partial(_merge_kernel, tm=tm, n_ctx=n_ctx),
        grid=(B, T // tm),
        in_specs=[tok(D), tok(4 * D), feat, tok(W), tok(W), tok(2 * W), tok(W), tok(W),
                  pl.BlockSpec((1, tm, W), lambda b, i: (b, i, 2)),
                  feat,
                  vec(W), vec(W), vec(W),
                  _const_spec((None, 4, W, D), lambda b, i: (l, 0, 0, 0)),
                  _const_spec((None, D, D), lambda b, i: (l, 0, 0)),
                  vec(D), _mod_spec(l, D),
                  _const_spec((2 * W, W), lambda b, i: (0, 0))],
        out_specs=tok(D),
        out_shape=jax.ShapeDtypeStruct((B, T, D), F32),
        compiler_params=_cparams(("parallel", "parallel")),
        name="merge",
    )(xs, gates, ya, yb[0], yb[1], aux, oc[0], oc[1], cols_c, yd, prm["lnx_g"], prm["lnx_b"], prm["ret_gn"],
      prm["w_branch"], prm["w_out"], prm["norm_post_mix"], mods, prm["ones_bd"])


def _mlp_kernel(x_ref, g_ref, pn_ref, mod_ref, wu_ref, wd_ref, o_ref, *, tm, n_ctx):
    x = x_ref[0]
    h = _rms(x, 1e-6) * g_ref[...]
    hb = (h * (1.0 + _mod_row(mod_ref, 4, tm, n_ctx)) + _mod_row(mod_ref, 3, tm, n_ctx)).astype(BF16)
    F = wu_ref.shape[-1]
    y = None
    for f0 in range(0, F, F // MLP_SPLIT):
        sl = slice(f0, f0 + F // MLP_SPLIT)
        u = jnp.dot(hb, wu_ref[:, sl], preferred_element_type=F32)
        u = jnp.square(jnp.maximum(u, 0.0)).astype(BF16)
        t = jnp.dot(u, wd_ref[sl, :], preferred_element_type=F32)
        y = t if y is None else y + t
    o_ref[0] = x + _mod_row(mod_ref, 5, tm, n_ctx) * (_rms(y, 1e-6) * pn_ref[...])


def _mlp(xs, prm, mods, l, tm, n_ctx):
    B, T, D = xs.shape
    F = prm["w_up"].shape[-1]
    vec = lambda n: pl.BlockSpec((None, 1, n), lambda b, i: (l, 0, 0))
    return pl.pallas_call(
        functools.partial(_mlp_kernel, tm=tm, n_ctx=n_ctx),
        grid=(B, T // tm),
        in_specs=[pl.BlockSpec((1, tm, D), lambda b, i: (b, i, 0)),
                  vec(D), vec(D), _mod_spec(l, D),
                  _const_spec((None, D, F), lambda b, i: (l, 0, 0)),
                  _const_spec((None, F, D), lambda b, i: (l, 0, 0))],
        out_specs=pl.BlockSpec((1, tm, D), lambda b, i: (b, i, 0)),
        out_shape=jax.ShapeDtypeStruct((B, T, D), F32),
        compiler_params=_cparams(("parallel", "parallel")),
        name="mlp",
    )(xs, prm["norm_pre_mlp"], prm["norm_post_mlp"], mods, prm["w_up"], prm["w_down"])


def kernel(x, c, ctx, c_ctx, ada_w, ada_b, norm_pre_mix, norm_post_mix, norm_pre_mlp, norm_post_mlp,
           w_in, diff_lam_q, diff_lam_k, diff_subln, rwkv_mu, rwkv_w0, rwkv_w2, rwkv_a0, rwkv_a2,
           rwkv_g2, rwkv_kk, rwkv_ka, rwkv_rk, rwkv_lnx_g, rwkv_lnx_b, ret_decay, ret_gn, win_sink,
           w_branch, w_out, w_up, w_down):
    B, S, D = x.shape
    n_ctx = ctx.shape[1]
    T = n_ctx + S
    L = ada_w.shape[0]
    W = BRANCH_W
    TM = 256
    TD = next(t for t in DENSE_TILES if T % t == 0)
    assert n_ctx % TM == 0 and S % TM == 0 and S % GRID_W == 0 and B + 1 <= 8

    cc = jnp.zeros((8, D), F32).at[:B].set(c).at[B].set(c_ctx)
    mod = _ada_mod(cc, ada_w, ada_b).reshape(L, 8, 6, D)
    mod_ctx = jnp.broadcast_to(mod[:, B][:, None], (L, B, 6, D))
    mods = jnp.stack([mod_ctx, mod[:, :B]], axis=2)
    mods = jnp.pad(mods, ((0, 0), (0, 0), (0, 0), (0, 2), (0, 0)))

    lat = jnp.arange(S, dtype=jnp.int32)
    row = (lat // GRID_W).astype(F32)
    col = (lat % GRID_W).astype(F32)
    tabs_a = _rope_tables([row, col], W, HEAD_W // 2, n_ctx)
    tabs_d = _rope_tables([row, col], W, HEAD_W, n_ctx)
    tabs_c = _rope_tables([lat.astype(F32)], W // 2, HEAD_W // 2, n_ctx)

    sizes = (3 * W, rwkv_mu.shape[-1], 3 * W, 2 * W, 4 * D)
    offs = np.concatenate([[0], np.cumsum(sizes)])
    w_groups = [w_in[:, :, int(offs[n]):int(offs[n + 1])].astype(BF16) for n in range(5)]

    lw_w, la_w = rwkv_w2.shape[2], rwkv_a2.shape[2]
    zw = jnp.zeros((L, W, 2 * W), F32)
    w2cat = (zw.at[:, 0:lw_w, 0:W].set(rwkv_w2[:, 0])
             .at[:, lw_w:2 * lw_w, W:2 * W].set(rwkv_w2[:, 1]))
    a2cat = (zw.at[:, 2 * lw_w:2 * lw_w + la_w, 0:W].set(rwkv_a2[:, 0])
             .at[:, 2 * lw_w + la_w:2 * lw_w + 2 * la_w, W:2 * W].set(rwkv_a2[:, 1]))
    blk = np.arange(W) // HEAD_W
    ones_bd = (blk[:, None] == blk[None, :]).astype(np.float32)
    ones_bd = jnp.asarray(np.concatenate([ones_bd, ones_bd], axis=0), BF16)
    v3 = lambda a: a.reshape(L, 1, -1)

    def hl3(w):
        hi = w.astype(BF16)
        return jnp.concatenate([hi, (w - hi.astype(F32)).astype(BF16), hi], axis=-2)

    prm = dict(mu=v3(rwkv_mu), kk=v3(rwkv_kk), ka=v3(rwkv_ka), rk=v3(rwkv_rk), w0=rwkv_w0, a0=rwkv_a0,
               w2=hl3(w2cat), a2=hl3(a2cat), g2=hl3(rwkv_g2), ones_bd=ones_bd,
               lnx_g=v3(rwkv_lnx_g), lnx_b=v3(rwkv_lnx_b), ret_gn=v3(ret_gn),
               w_branch=w_branch.astype(BF16), w_out=w_out.astype(BF16),
               norm_post_mix=v3(norm_post_mix), norm_pre_mlp=v3(norm_pre_mlp),
               norm_post_mlp=v3(norm_post_mlp), w_up=w_up.astype(BF16), w_down=w_down.astype(BF16))
    subln = diff_subln.reshape(L, N_HEADS, HEAD_W, 1)
    decay = ret_decay.reshape(L, 2, 1, N_HEADS)
    TQ_D = 256
    sink_col = jnp.broadcast_to(win_sink.reshape(L, 2, 1, 2, 1), (L, 2, 1, 2, TQ_D)).reshape(L, 2, 1, 2 * TQ_D)
    gain_pre = v3(norm_pre_mix)

    xs = jnp.concatenate([ctx, x], axis=1)
    for l in range(L):
        qa, ka, va, cols_b, cols_c, qd, kd, vd, gates = _input_proj(
            xs, gain_pre, mods, w_groups, tabs_a, tabs_d, l, TD, n_ctx)
        ya = _diff_attention(qa, ka, va, diff_lam_q, diff_lam_k, subln, l, ATT_TQ, ATT_TK, ATT_UNROLL, n_ctx)
        rvk, dirs, aux = _rwkv_prep(cols_b, prm, l, TM, n_ctx)
        yb = _rwkv_scan(rvk, dirs, n_ctx)
        oc = _retention(cols_c, tabs_c, decay, l, n_ctx)
        yd = _window_attention(qd, kd, vd, sink_col, l, TQ_D, n_ctx)
        xs = _merge(xs, gates, ya, yb, aux, oc, cols_c, yd, prm, mods, l, TD, n_ctx)
        xs = _mlp(xs, prm, mods, l, TD, n_ctx)
    return xs[:, n_ctx:]
```

```python
import functools
import math

import numpy as np
import jax
import jax.numpy as jnp
from jax import lax
from jax.experimental import pallas as pl
from jax.experimental.pallas import tpu as pltpu

F32 = jnp.float32
BF16 = jnp.bfloat16
HI = lax.Precision.HIGHEST

GRID_W = 64
ROPE_BASE = 10000.0
NEG_INF = -1e30
WINDOW = 128
N_HEADS = 4
HEAD_W = 64
BRANCH_W = N_HEADS * HEAD_W
RW_CHUNK = 64
RW_SUB = 4
RET_CHUNK = 128
RET_SUB = 2
RW_GN_EPS = 64e-5
ATT_TQ, ATT_TK, ATT_UNROLL = 256, 256, 64
ATT_HEADS = 2
ATT_LOOKAHEAD, ATT_SLOTS = 3, 8
DENSE_TILES = (640, 512, 384, 256, 128)
MLP_SPLIT = 2
VMEM_LIMIT = 56 * 1024 * 1024


def _cparams(sem, vmem=None):
    return pltpu.CompilerParams(dimension_semantics=sem, vmem_limit_bytes=vmem or VMEM_LIMIT)


def _dotf(a, b):
    return jnp.dot(a, b, precision=HI, preferred_element_type=F32)


def _dot_nt(a, b, precision=None):
    return lax.dot_general(a, b, (((1,), (1,)), ((), ())), precision=precision,
                           preferred_element_type=F32)


def _dot_tn(a, b, precision=None):
    return lax.dot_general(a, b, (((0,), (0,)), ((), ())), precision=precision,
                           preferred_element_type=F32)


def _split(x):
    hi = x.astype(BF16)
    return hi, (x - hi.astype(F32)).astype(BF16)


def _dot_hl(x, w3):
    hi, lo = _split(x)
    return jnp.dot(jnp.concatenate([hi, hi, lo], axis=1), w3, preferred_element_type=F32)


def _group_sum(x, ones2):
    hi, lo = _split(x)
    return jnp.dot(jnp.concatenate([hi, lo], axis=1), ones2, preferred_element_type=F32)


def _sdot(a, b):
    return jnp.dot(a.astype(BF16), b.astype(BF16), preferred_element_type=F32)


def _sdot_tn(a, b):
    return _dot_tn(a.astype(BF16), b.astype(BF16))


def _sigmoid(x):
    return 0.5 * jnp.tanh(0.5 * x) + 0.5


def _softplus(x):
    return jnp.maximum(x, 0.0) + jnp.log1p(jnp.exp(-jnp.abs(x)))


def _rms(x, eps):
    return x * lax.rsqrt(jnp.mean(x * x, axis=-1, keepdims=True) + eps)


def _const_spec(shape, index):
    return pl.BlockSpec(shape, index, pipeline_mode=pl.Buffered(1))


def _ada_kernel(c_ref, w_ref, b_ref, o_ref):
    c = c_ref[...]
    o_ref[0] = _dotf(c * _sigmoid(c), w_ref[0]) + b_ref[0]


def _ada_mod(cc, ada_w, ada_b):
    L, D, N = ada_w.shape
    tn = N // 4
    return pl.pallas_call(
        _ada_kernel,
        grid=(L, N // tn),
        in_specs=[pl.BlockSpec((8, D), lambda l, n: (0, 0)),
                  pl.BlockSpec((1, D, tn), lambda l, n: (l, 0, n)),
                  pl.BlockSpec((1, 1, tn), lambda l, n: (l, 0, n))],
        out_specs=pl.BlockSpec((1, 8, tn), lambda l, n: (l, 0, n)),
        out_shape=jax.ShapeDtypeStruct((L, 8, N), F32),
        compiler_params=_cparams(("arbitrary", "arbitrary")),
        name="ada_mod",
    )(cc, ada_w, ada_b.reshape(L, 1, N))


def _mod_row(mod_ref, j, tm, n_ctx):
    tok = pl.program_id(1) * tm + lax.broadcasted_iota(jnp.int32, (tm, 1), 0)
    return jnp.where(tok < n_ctx, mod_ref[0, j:j + 1, :], mod_ref[1, j:j + 1, :])


def _mod_spec(l, D):
    return pl.BlockSpec((None, None, 2, 8, D), lambda b, i: (l, b, 0, 0, 0))


def _proj_kernel(x_ref, g_ref, mod_ref, wa_ref, wb_ref, wc_ref, wd_ref, wg_ref,
                 cosa_ref, sma_ref, spa_ref, cosd_ref, smd_ref, spd_ref,
                 qa_ref, ka_ref, va_ref, cb_ref, cc_ref, qd_ref, kd_ref, vd_ref, gates_ref, *,
                 tm, n_ctx, half_a, half_d):
    h = _rms(x_ref[0], 1e-6) * g_ref[...]
    h = h * (1.0 + _mod_row(mod_ref, 1, tm, n_ctx)) + _mod_row(mod_ref, 0, tm, n_ctx)
    hb = h.astype(BF16)
    proj = lambda w_ref: jnp.dot(hb, w_ref[...], preferred_element_type=F32)
    _store_diff_qkv(proj(wa_ref), cosa_ref[...], sma_ref[...], spa_ref[...], half_a, qa_ref, ka_ref, va_ref)
    cb_ref[0] = proj(wb_ref)
    cc_ref[0] = proj(wc_ref)
    _store_win_qkv(proj(wd_ref), cosd_ref[...], smd_ref[...], spd_ref[...], half_d, qd_ref, kd_ref, vd_ref)
    gates_ref[0] = proj(wg_ref).astype(gates_ref.dtype)


def _input_proj(xs, gain, mods, weights, tabs_a, tabs_d, l, tm, n_ctx):
    B, T, D = xs.shape
    W = BRANCH_W
    tok = lambda n: pl.BlockSpec((1, tm, n), lambda b, i: (b, i, 0))
    tab = pl.BlockSpec((tm, W), lambda b, i: (i, 0))
    in_specs = [tok(D), pl.BlockSpec((None, 1, D), lambda b, i: (l, 0, 0)), _mod_spec(l, D)]
    in_specs += [_const_spec((None, D, w.shape[-1]), lambda b, i: (l, 0, 0)) for w in weights]
    in_specs += [tab] * 6
    heads = lambda *lead: pl.BlockSpec((1,) + lead + (tm, HEAD_W), lambda b, i: (b,) + (0,) * len(lead) + (i, 0))
    heads_t = lambda n: pl.BlockSpec((1, n, HEAD_W, tm), lambda b, i: (b, 0, 0, i))
    qa_spec = pl.BlockSpec((1, N_HEADS, 2, HEAD_W, tm), lambda b, i: (b, 0, 0, 0, i))
    out_specs = [qa_spec, heads(N_HEADS), heads_t(N_HEADS), tok(weights[1].shape[-1]),
                 tok(weights[2].shape[-1]), pl.BlockSpec((1, W, tm), lambda b, i: (b, 0, i)),
                 heads(2), heads_t(2), tok(weights[4].shape[-1])]
    sds = jax.ShapeDtypeStruct
    out_shape = [sds((B, N_HEADS, 2, HEAD_W, T), BF16), sds((B, N_HEADS, T, HEAD_W), BF16),
                 sds((B, N_HEADS, HEAD_W, T), BF16), sds((B, T, weights[1].shape[-1]), F32),
                 sds((B, T, weights[2].shape[-1]), F32), sds((B, W, T), BF16),
                 sds((B, 2, T, HEAD_W), BF16), sds((B, 2, HEAD_W, T), BF16),
                 sds((B, T, weights[4].shape[-1]), BF16)]
    return pl.pallas_call(
        functools.partial(_proj_kernel, tm=tm, n_ctx=n_ctx, half_a=tabs_a[3], half_d=tabs_d[3]),
        grid=(B, T // tm),
        in_specs=in_specs, out_specs=out_specs, out_shape=out_shape,
        compiler_params=_cparams(("parallel", "parallel")),
        name="input_proj",
    )(xs, gain, mods, *weights, *tabs_a[:3], *tabs_d[:3])


def _rope_tables(pos_sets, lanes, group, n_ctx):
    n_sets = len(pos_sets)
    sub = group // n_sets
    half = sub // 2
    angs = []
    for pos in pos_sets:
        inv_freq = ROPE_BASE ** (-jnp.arange(half, dtype=F32) / half)
        ang = pos[:, None] * inv_freq[None, :]
        angs.append(jnp.concatenate([ang, ang], axis=-1))
    ang = jnp.concatenate(angs, axis=-1)
    ang = jnp.tile(ang, (1, lanes // group))
    ang = jnp.concatenate([jnp.zeros((n_ctx, lanes), F32), ang], axis=0)
    first = (np.arange(lanes) % sub) < half
    cos, sin = jnp.cos(ang), jnp.sin(ang)
    sin_minus = jnp.where(first[None, :], -sin, 0.0)
    sin_plus = jnp.where(first[None, :], 0.0, sin)
    return cos, sin_minus, sin_plus, half


def _rope(x, cos, sin_minus, sin_plus, half):
    n = x.shape[-1]
    return x * cos + pltpu.roll(x, n - half, 1) * sin_minus + pltpu.roll(x, half, 1) * sin_plus


def _store_diff_qkv(c, cos, sm, sp, half, q_ref, k_ref, v_ref):
    scale = (HEAD_W // 2) ** -0.5 * math.log2(math.e)
    q = _rope(c[:, 0:BRANCH_W], cos, sm, sp, half) * scale
    k = _rope(c[:, BRANCH_W:2 * BRANCH_W], cos, sm, sp, half)
    v = c[:, 2 * BRANCH_W:3 * BRANCH_W]
    lane = lax.broadcasted_iota(jnp.int32, q.shape, 1)
    comp = (lane >> 5) & 1
    q0 = jnp.where(comp == 0, q, 0.0).T.astype(BF16)
    q1 = jnp.where(comp == 1, q, 0.0).T.astype(BF16)
    kb, vt = k.astype(BF16), v.T.astype(BF16)
    for h in range(N_HEADS):
        sl = slice(h * HEAD_W, (h + 1) * HEAD_W)
        q_ref[0, h, 0] = q0[sl, :]
        q_ref[0, h, 1] = q1[sl, :]
        k_ref[0, h] = kb[:, sl]
        v_ref[0, h] = vt[sl, :]


def _flash_kernel(lq_ref, lk_ref, g_ref, q_ref, k_ref, vt_ref, o_ref, s_sc, acc_sc, *,
                  tq, tk, unroll, n_ctx, T, lam_init):
    qi = pl.program_id(2)
    NH = q_ref.shape[1]
    qt = [jnp.concatenate([q_ref[0, hh, 0], q_ref[0, hh, 1]], axis=1) for hh in range(NH)]

    def scores(hh, slot, off, size):
        s_sc[hh, slot, 0:size] = jnp.dot(k_ref[0, hh, pl.ds(off, size), :], qt[hh], preferred_element_type=F32)

    def absorb(hh, slot, off, size, m):
        s = s_sc[hh, slot, 0:size]
        m_new = jnp.maximum(m, jnp.max(s, axis=0, keepdims=True))
        p = jnp.exp2(s - m_new).astype(BF16)
        vt = jnp.concatenate([vt_ref[0, hh, :, pl.ds(off, size)], jnp.ones((8, size), BF16)], axis=0)
        acc_sc[hh] = acc_sc[hh] * jnp.exp2(m - m_new) + jnp.dot(vt, p, preferred_element_type=F32)
        return m_new

    n_slots, look = s_sc.shape[1], ATT_LOOKAHEAD
    lat_off = lambda c: n_ctx + (c - 1) * tk
    acc_sc[...] = jnp.zeros(acc_sc.shape, F32)
    m = tuple(jnp.full((1, 2 * tq), NEG_INF, F32) for _ in range(NH))
    for hh in range(NH):
        scores(hh, 0, 0, n_ctx)
    for c in range(1, look + 1):
        for hh in range(NH):
            scores(hh, c % n_slots, min(lat_off(c), T - tk), tk)
    m = tuple(absorb(hh, 0, 0, n_ctx, m[hh]) for hh in range(NH))

    def body(jj, m):
        m = list(m)
        c0 = 1 + jj * unroll
        for u in range(unroll):
            off = pl.multiple_of(lat_off(c0 + u), math.gcd(n_ctx, tk))
            nxt = pl.multiple_of(jnp.minimum(lat_off(c0 + u + look), T - tk), math.gcd(n_ctx, tk))
            for hh in range(NH):
                scores(hh, (1 + u + look) % n_slots, nxt, tk)
                m[hh] = absorb(hh, (1 + u) % n_slots, off, tk, m[hh])
        return tuple(m)

    n_trips = (T - n_ctx) // (unroll * tk)
    lax.fori_loop(0, jnp.where(qi * tq < n_ctx, 0, n_trips), body, m)
    e0 = jnp.exp(jnp.sum(lq_ref[0:1, :] * lk_ref[0:1, :], axis=1, keepdims=True))
    e1 = jnp.exp(jnp.sum(lq_ref[1:2, :] * lk_ref[1:2, :], axis=1, keepdims=True))
    lam = e0 - e1 + lam_init
    for hh in range(NH):
        acc = acc_sc[hh]
        o = acc[0:HEAD_W] / acc[HEAD_W:HEAD_W + 1]
        y = o[:, :tq] - lam * o[:, tq:]
        o_ref[0, hh * HEAD_W:(hh + 1) * HEAD_W, :] = (
            y * lax.rsqrt(jnp.mean(y * y, axis=0, keepdims=True) + 1e-5) * g_ref[hh] * (1.0 - lam_init))


def _diff_attention(q, k, v, lam_q, lam_k, subln, l, tq, tk, unroll, n_ctx):
    B, H, _, _, T = q.shape
    NH = ATT_HEADS
    dqk = lam_q.shape[-1]
    lam_init = 0.8 - 0.6 * math.exp(-0.3 * l)
    unroll = math.gcd(unroll, (T - n_ctx) // tk)
    assert (unroll % ATT_SLOTS == 0 or unroll * tk == T - n_ctx) and ATT_LOOKAHEAD < ATT_SLOTS
    assert (T - n_ctx) % (unroll * tk) == 0 and n_ctx % tq == 0 and n_ctx % 128 == 0 and tk % 128 == 0
    assert H % NH == 0
    kern = functools.partial(_flash_kernel, tq=tq, tk=tk, unroll=unroll, n_ctx=n_ctx, T=T,
                             lam_init=lam_init)
    return pl.pallas_call(
        kern,
        grid=(B, H // NH, T // tq),
        in_specs=[pl.BlockSpec((None, 2, dqk), lambda b, h, i: (l, 0, 0)),
                  pl.BlockSpec((None, 2, dqk), lambda b, h, i: (l, 0, 0)),
                  pl.BlockSpec((None, NH, HEAD_W, 1), lambda b, h, i: (l, h, 0, 0)),
                  pl.BlockSpec((1, NH, 2, HEAD_W, tq), lambda b, h, i: (b, h, 0, 0, i)),
                  pl.BlockSpec((1, NH, T, HEAD_W), lambda b, h, i: (b, h, 0, 0)),
                  pl.BlockSpec((1, NH, HEAD_W, T), lambda b, h, i: (b, h, 0, 0))],
        out_specs=pl.BlockSpec((1, NH * HEAD_W, tq), lambda b, h, i: (b, h, i)),
        out_shape=jax.ShapeDtypeStruct((B, H * HEAD_W, T), F32),
        scratch_shapes=[pltpu.VMEM((NH, ATT_SLOTS, max(tk, n_ctx), 2 * tq), F32),
                        pltpu.VMEM((NH, HEAD_W + 8, 2 * tq), F32)],
        compiler_params=_cparams(("parallel", "parallel", "arbitrary")),
        name="diff_attention",
    )(lam_q, lam_k, subln, q, k, v)


def _rwprep_kernel(c_ref, p_ref, n_ref, mu_ref, kk_ref, ka_ref, rk_ref, w0_ref, a0_ref,
                   w2_ref, a2_ref, g2_ref, ones_ref, rvk_ref, dir_ref, aux_ref, *, tm, n_ctx, T):
    i = pl.program_id(1)
    c = c_ref[0]
    t0 = i * tm
    seg_start = jnp.logical_or(t0 == 0, t0 == n_ctx)
    seg_end = jnp.logical_or(t0 + tm == n_ctx, t0 + tm == T)
    pv = jnp.where(seg_start, 0.0, p_ref[0][7:8, :])
    nx = jnp.where(seg_end, 0.0, n_ref[0][0:1, :])
    row = lax.broadcasted_iota(jnp.int32, c.shape, 0)
    prev = jnp.where(row == 0, pv, pltpu.roll(c, 1, 0))
    nxt = jnp.where(row == tm - 1, nx, pltpu.roll(c, tm - 1, 0))
    xs = c + (0.5 * (prev + nxt) - c) * mu_ref[...]
    W = BRANCH_W
    r, k, v = xs[:, 0:W], xs[:, W:2 * W], xs[:, 2 * W:3 * W]
    lo, gl = xs[:, 3 * W:4 * W], xs[:, 4 * W:]
    ones2 = ones_ref[...]
    kk = k * kk_ref[...]
    kk = kk * lax.rsqrt(jnp.maximum(_group_sum(kk * kk, ones2), 1e-12))
    wpre = _dot_hl(jnp.tanh(lo), w2_ref[...])
    apre = _dot_hl(lo, a2_ref[...])
    gate = _dot_hl(_sigmoid(gl), g2_ref[...])
    rb = jnp.zeros_like(r)
    for d in range(2):
        z = w0_ref[d:d + 1, :] + wpre[:, d * W:(d + 1) * W]
        w = -_softplus(-z) - 0.5
        a = _sigmoid(a0_ref[d:d + 1, :] + apre[:, d * W:(d + 1) * W])
        kd = k * (1.0 + (a - 1.0) * ka_ref[...])
        dir_ref[0, d, :, 0:W] = -jnp.exp(w)
        dir_ref[0, d, :, W:2 * W] = kd
        dir_ref[0, d, :, 2 * W:3 * W] = kk * a
        rb = rb + r * kd * rk_ref[...]
    rvk_ref[0, :, 0:W] = r
    rvk_ref[0, :, W:2 * W] = v
    rvk_ref[0, :, 2 * W:3 * W] = kk
    aux_ref[0, :, 0:W] = _group_sum(rb, ones2) * v
    aux_ref[0, :, W:2 * W] = gate


def _rwkv_prep(cols, prm, l, tm, n_ctx):
    B, T, NB = cols.shape
    W = BRANCH_W
    nblk8 = T // 8
    r8 = tm // 8
    vec = lambda n: pl.BlockSpec((None, 1, n), lambda b, i: (l, 0, 0))
    pair = lambda n: pl.BlockSpec((None, 2, n), lambda b, i: (l, 0, 0))
    mat = lambda m, n: _const_spec((None, m, n), lambda b, i: (l, 0, 0))
    return pl.pallas_call(
        functools.partial(_rwprep_kernel, tm=tm, n_ctx=n_ctx, T=T),
        grid=(B, T // tm),
        in_specs=[pl.BlockSpec((1, tm, NB), lambda b, i: (b, i, 0)),
                  pl.BlockSpec((1, 8, NB), lambda b, i: (b, jnp.maximum(i * r8 - 1, 0), 0)),
                  pl.BlockSpec((1, 8, NB), lambda b, i: (b, jnp.minimum((i + 1) * r8, nblk8 - 1), 0)),
                  vec(NB), vec(W), vec(W), vec(W), pair(W), pair(W),
                  mat(3 * W, 2 * W), mat(3 * W, 2 * W), mat(3 * (NB - 4 * W), W),
                  _const_spec((2 * W, W), lambda b, i: (0, 0))],
        out_specs=[pl.BlockSpec((1, tm, 3 * W), lambda b, i: (b, i, 0)),
                   pl.BlockSpec((1, 2, tm, 3 * W), lambda b, i: (b, 0, i, 0)),
                   pl.BlockSpec((1, tm, 2 * W), lambda b, i: (b, i, 0))],
        out_shape=[jax.ShapeDtypeStruct((B, T, 3 * W), F32),
                   jax.ShapeDtypeStruct((B, 2, T, 3 * W), F32),
                   jax.ShapeDtypeStruct((B, T, 2 * W), F32)],
        compiler_params=_cparams(("parallel", "parallel")),
        name="rwkv_prep",
    )(cols, cols, cols, prm["mu"], prm["kk"], prm["ka"], prm["rk"], prm["w0"], prm["a0"],
      prm["w2"], prm["a2"], prm["g2"], prm["ones_bd"])


def _rw_chunks(chunks, rounds, state_of, states):
    C, W = RW_CHUNK, BRANCH_W
    each = lambda f, *cols: [f(*a) for a in zip(*cols)]
    r, v, kk, lw, kd, b, sgn = (list(col) for col in zip(*chunks))

    ti = lax.broadcasted_iota(jnp.int32, (C, C), 0)
    tj = lax.broadcasted_iota(jnp.int32, (C, C), 1)
    ri = lax.broadcasted_iota(jnp.int32, (W, W), 0)
    ci = lax.broadcasted_iota(jnp.int32, (W, W), 1)
    same_head = (ri >> 6) == (ci >> 6)
    bd = lambda x: jnp.where(same_head, jnp.concatenate([x] * N_HEADS, axis=0), 0.0)
    row = lax.broadcasted_iota(jnp.int32, (C, W), 0)
    col = lax.broadcasted_iota(jnp.int32, (C, W), 1) & (C - 1)
    masks = {sg: (jnp.where((tj - ti) * sg <= 0, 1.0, 0.0), (col - row) * sg < 0, (col - row) * sg <= 0)
             for sg in set(sgn)}
    tri = [masks[sg][0] for sg in sgn]
    strict = [masks[sg][1] for sg in sgn]
    incl = [masks[sg][2] for sg in sgn]
    blk16 = (row >> 4) == (col >> 4)
    blk32 = (row >> 5) == (col >> 5)
    eye = jnp.where(row == col, 1.0, 0.0)

    log_g = each(_dotf, tri, lw)
    log_gc = [jnp.sum(x, axis=0, keepdims=True) for x in lw]
    kk_t = each(lambda kk_, lg, lw_: kk_ * jnp.exp(lg - lw_), kk, log_g, lw)
    r_t = each(lambda r_, lg: r_ * jnp.exp(lg), r, log_g)
    g_inv = [jnp.exp(-lg) for lg in log_g]
    g_tail = each(lambda lgc, lg: jnp.exp(lgc - lg), log_gc, log_g)

    bdb = lambda x: bd(x.astype(BF16))
    left = each(lambda a, c: jnp.concatenate([a, c], axis=0).astype(BF16), kk_t, r_t)
    right = each(lambda b_, kd_, gi: jnp.concatenate([bdb(b_ * gi), bdb(kd_ * gi)], axis=0), b, kd, g_inv)
    g = each(_dot_nt, left, right)
    m_sl = each(lambda st, g_: jnp.where(st, g_[:C, :W], 0.0), strict, g)
    n_sl = each(lambda st, g_: jnp.where(st, g_[:C, W:], 0.0), strict, g)
    pq_l = each(lambda ic, g_: jnp.concatenate([jnp.where(ic, g_[C:, :W], 0.0),
                                                jnp.where(ic, g_[C:, W:], 0.0)], axis=1).astype(BF16), incl, g)

    v_bd = [bdb(x) for x in v]
    nv = each(_sdot, n_sl, v_bd)
    b_tail = each(lambda b_, kd_, gt: jnp.concatenate([b_ * gt, kd_ * gt], axis=0).astype(BF16), b, kd, g_tail)
    gc_col = [jnp.sum(jnp.where(ri == ci, jnp.exp(lgc), 0.0), axis=1, keepdims=True) for lgc in log_gc]

    d0 = [jnp.where(blk16, m_, 0.0) for m_ in m_sl]
    sq = lambda x: _sdot(x, bdb(x))
    a2 = each(sq, d0)
    a4 = each(sq, a2)
    a8 = each(sq, a4)
    t = [eye - d_ for d_ in d0]
    for a_ in (a2, a4, a8):
        t = each(lambda t_, x: t_ + _sdot(t_, bdb(x)), t, a_)
    for cm in (jnp.logical_and(blk32, jnp.logical_not(blk16)), jnp.logical_not(blk32)):
        u = each(lambda m_, t_: _sdot(jnp.where(cm, m_, 0.0), bdb(t_)), m_sl, t)
        t = each(lambda t_, u_: t_ - _sdot(t_, bdb(u_)), t, u)

    states = list(states)
    ys = [None] * len(chunks)
    for ids in rounds:
        pick = lambda xs: [xs[i] for i in ids]
        s0 = [states[state_of[i]] for i in ids]
        g2 = each(_sdot, pick(left), s0)
        rhs = each(lambda g2_, nv_: -(g2_[:C] + nv_), g2, pick(nv))
        sa = each(lambda t_, rh: _sdot(t_, bdb(rh)), pick(t), rhs)
        y = each(lambda g2_, pq, sa_, vb: g2_[C:] + _sdot(pq, jnp.concatenate([bdb(sa_), vb], axis=0)),
                 g2, pick(pq_l), sa, pick(v_bd))
        ds = each(lambda bt, sa_, v_: _sdot_tn(bt, jnp.concatenate([sa_, v_], axis=0)),
                  pick(b_tail), sa, pick(v))
        for i, y_, s_, ds_ in zip(ids, y, s0, ds):
            ys[i] = y_
            states[state_of[i]] = jnp.where(same_head, s_ * gc_col[i] + ds_, 0.0)
    return ys, states


def _rwscan_kernel(rvf_ref, rvb_ref, df_ref, db_ref, yf_ref, yb_ref, s_ref, *, n_sub):
    C, W = RW_CHUNK, BRANCH_W

    @pl.when(pl.program_id(0) == 0)
    def _():
        s_ref[...] = jnp.zeros(s_ref.shape, F32)

    refs = ((rvf_ref, df_ref, yf_ref), (rvb_ref, db_ref, yb_ref))
    chunks, where, state_of = [], [], []
    rounds = [[] for _ in range(n_sub)]
    for bi in range(rvf_ref.shape[0]):
        for d, (rv, dr, _) in enumerate(refs):
            for k in range(n_sub):
                rows = pl.ds((k if d == 0 else n_sub - 1 - k) * C, C)
                rounds[k].append(len(chunks))
                state_of.append(2 * bi + d)
                where.append((d, bi, rows))
                chunks.append((rv[bi, rows, 0:W], rv[bi, rows, W:2 * W], rv[bi, rows, 2 * W:3 * W],
                               dr[bi, 0, rows, 0:W], dr[bi, 0, rows, W:2 * W], dr[bi, 0, rows, 2 * W:3 * W],
                               1 - 2 * d))
    n_states = s_ref.shape[0]
    ys, states = _rw_chunks(chunks, rounds, state_of, [s_ref[i] for i in range(n_states)])
    for (d, bi, rows), y in zip(where, ys):
        refs[d][2][bi, rows, :] = y
    for i in range(n_states):
        s_ref[i] = states[i]


def _chunk_order(s, d, n_ctx_chunks, n_chunks):
    bwd = jnp.where(s < n_ctx_chunks, n_ctx_chunks - 1 - s, n_chunks - 1 - (s - n_ctx_chunks))
    return jnp.where(d == 0, s, bwd)


def _rwkv_scan(rvk, dirs, n_ctx):
    B, T, _ = rvk.shape
    W = BRANCH_W
    C = RW_CHUNK * RW_SUB
    assert T % C == 0 and n_ctx % C == 0
    nc, ncx = T // C, n_ctx // C
    ch = lambda s, d: _chunk_order(s, d, ncx, nc)
    rv_spec = lambda d: pl.BlockSpec((B, C, 3 * W), lambda s: (0, ch(s, d), 0))
    dir_spec = lambda d: pl.BlockSpec((B, 1, C, 3 * W), lambda s: (0, d, ch(s, d), 0))
    y_spec = lambda d: pl.BlockSpec((B, C, W), lambda s: (0, ch(s, d), 0))
    return pl.pallas_call(
        functools.partial(_rwscan_kernel, n_sub=RW_SUB),
        grid=(nc,),
        in_specs=[rv_spec(0), rv_spec(1), dir_spec(0), dir_spec(1)],
        out_specs=[y_spec(0), y_spec(1)],
        out_shape=[jax.ShapeDtypeStruct((B, T, W), F32)] * 2,
        scratch_shapes=[pltpu.VMEM((2 * B, W, W), F32)],
        compiler_params=_cparams(("arbitrary",)),
        name="rwkv_scan",
    )(rvk, rvk, dirs, dirs)


def _ret_chunk(c, cos, sm, sp, dec, d, half, scale):
    C, W = RET_CHUNK, BRANCH_W
    QW = W // 2
    q = _rope(c[:, 0:QW], cos, sm, sp, half)
    k = _rope(c[:, QW:2 * QW], cos, sm, sp, half) * scale
    v = c[:, 2 * QW:2 * QW + W]
    log_g = -_softplus(-dec)
    lane_q = lax.broadcasted_iota(jnp.int32, (1, QW), 1) >> 5
    lane_v = lax.broadcasted_iota(jnp.int32, (1, W), 1) >> 6
    lg_q = jnp.zeros((1, QW), F32)
    lg_v = jnp.zeros((1, W), F32)
    for h in range(N_HEADS):
        lg_q = jnp.where(lane_q == h, log_g[:, h:h + 1], lg_q)
        lg_v = jnp.where(lane_v == h, log_g[:, h:h + 1], lg_v)
    idx = lax.broadcasted_iota(jnp.int32, (C, 1), 0).astype(F32)
    q_dec = jnp.exp((idx + 1.0 if d == 0 else C - idx) * lg_q)
    k_dec = jnp.exp((C - 1.0 - idx if d == 0 else idx) * lg_q)
    ti = lax.broadcasted_iota(jnp.int32, (C, C), 0)
    tj = lax.broadcasted_iota(jnp.int32, (C, C), 1)
    rel = ((ti - tj) * (1 - 2 * d)).astype(F32)

    kb = k.astype(BF16)
    vb = v.astype(BF16)
    o = jnp.zeros((C, W), F32)
    for h in range(N_HEADS):
        dm = jnp.where(rel >= 0, jnp.exp(jnp.maximum(rel, 0.0) * log_g[:, h:h + 1]), 0.0)
        qh = jnp.where(lane_q == h, q, 0.0).astype(BF16)
        sc = _dot_nt(qh, kb) * dm
        o = o + jnp.where(lane_v == h, jnp.dot(sc.astype(BF16), vb, preferred_element_type=F32), 0.0)

    ri = lax.broadcasted_iota(jnp.int32, (QW, W), 0) >> 5
    ci = lax.broadcasted_iota(jnp.int32, (QW, W), 1) >> 6
    upd = jnp.where(ri == ci, _dot_tn((k * k_dec).astype(BF16), vb), 0.0)
    return o, (q * q_dec).astype(BF16), upd, jnp.exp(C * lg_v)


def _ret_kernel(cf_ref, cb_ref, cosf_ref, smf_ref, spf_ref, cosb_ref, smb_ref, spb_ref, dec_ref,
                of_ref, ob_ref, s_ref, *, half, scale):
    C = RET_CHUNK

    @pl.when(pl.program_id(0) == 0)
    def _():
        s_ref[...] = jnp.zeros(s_ref.shape, F32)

    dirs = ((cf_ref, (cosf_ref, smf_ref, spf_ref), of_ref), (cb_ref, (cosb_ref, smb_ref, spb_ref), ob_ref))
    for bi in range(cf_ref.shape[0]):
        for d, (c_ref, tabs, o_ref) in enumerate(dirs):
            order = [pl.ds((k if d == 0 else RET_SUB - 1 - k) * C, C) for k in range(RET_SUB)]
            parts = [_ret_chunk(c_ref[bi, rows, :], tabs[0][rows, :], tabs[1][rows, :], tabs[2][rows, :],
                                dec_ref[d], d, half, scale) for rows in order]
            s = s_ref[2 * bi + d]
            for rows, (o, qd, upd, cdec) in zip(order, parts):
                o_ref[bi, rows, :] = o + jnp.dot(qd, s.astype(BF16), preferred_element_type=F32)
                s = s * cdec + upd
            s_ref[2 * bi + d] = s


def _retention(cols, tabs, decay, l, n_ctx):
    B, T, NCOL = cols.shape
    C, W = RET_CHUNK * RET_SUB, BRANCH_W
    cos, sm, sp, half = tabs
    assert T % C == 0 and n_ctx % C == 0
    nc, ncx = T // C, n_ctx // C
    ch = lambda s, d: _chunk_order(s, d, ncx, nc)
    c_spec = lambda d: pl.BlockSpec((B, C, NCOL), lambda s: (0, ch(s, d), 0))
    tab_spec = lambda d: pl.BlockSpec((C, W // 2), lambda s: (ch(s, d), 0))
    o_spec = lambda d: pl.BlockSpec((B, C, W), lambda s: (0, ch(s, d), 0))
    return pl.pallas_call(
        functools.partial(_ret_kernel, half=half, scale=(HEAD_W // 2) ** -0.5),
        grid=(nc,),
        in_specs=[c_spec(0), c_spec(1), tab_spec(0), tab_spec(0), tab_spec(0),
                  tab_spec(1), tab_spec(1), tab_spec(1),
                  pl.BlockSpec((None, 2, 1, N_HEADS), lambda s: (l, 0, 0, 0))],
        out_specs=[o_spec(0), o_spec(1)],
        out_shape=[jax.ShapeDtypeStruct((B, T, W), F32)] * 2,
        scratch_shapes=[pltpu.VMEM((2 * B, W // 2, W), F32)],
        compiler_params=_cparams(("arbitrary",)),
        name="retention",
    )(cols, cols, cos, sm, sp, cos, sm, sp, decay)


def _store_win_qkv(c, cos, sm, sp, half, q_ref, k_ref, v_ref):
    W = BRANCH_W
    q_ref[0] = (_rope(c[:, 0:W], cos, sm, sp, half) * HEAD_W ** -0.5).T.astype(BF16)
    kw = W // 2
    k = _rope(c[:, W:W + kw], cos[:, :kw], sm[:, :kw], sp[:, :kw], half).astype(BF16)
    vt = c[:, W + kw:W + 2 * kw].T.astype(BF16)
    for kvh in range(2):
        k_ref[0, kvh] = k[:, kvh * HEAD_W:(kvh + 1) * HEAD_W]
        v_ref[0, kvh] = vt[kvh * HEAD_W:(kvh + 1) * HEAD_W, :]


def _winattn_kernel(sink_ref, q_ref, k_ref, v_ref, o_ref, *, tq, n_ctx, T):
    i = pl.program_id(1)
    band = tq + 2 * WINDOW
    ws = pl.multiple_of(jnp.clip(i * tq - WINDOW, n_ctx, T - band), WINDOW)
    kpos = ws + lax.broadcasted_iota(jnp.int32, (band, tq), 0)
    qpos = i * tq + lax.broadcasted_iota(jnp.int32, (band, tq), 1)
    valid = jnp.logical_and(jnp.abs(qpos - kpos) <= WINDOW, qpos >= n_ctx)
    valid = jnp.concatenate([valid, valid], axis=1)
    for kvh in range(k_ref.shape[1]):
        r0 = 2 * kvh * HEAD_W
        qt = jnp.concatenate([q_ref[0, r0:r0 + HEAD_W, :], q_ref[0, r0 + HEAD_W:r0 + 2 * HEAD_W, :]], axis=1)
        s_ctx = jnp.dot(k_ref[0, kvh, 0:n_ctx, :], qt, preferred_element_type=F32)
        s_band = jnp.dot(k_ref[0, kvh, pl.ds(ws, band), :], qt, preferred_element_type=F32)
        s_band = jnp.where(valid, s_band, NEG_INF)
        sink = sink_ref[kvh]
        m = jnp.maximum(jnp.maximum(jnp.max(s_ctx, axis=0, keepdims=True),
                                    jnp.max(s_band, axis=0, keepdims=True)), sink)
        p_ctx = jnp.exp(s_ctx - m).astype(BF16)
        p_band = jnp.exp(s_band - m).astype(BF16)
        vx = jnp.concatenate([v_ref[0, kvh, :, 0:n_ctx], jnp.ones((8, n_ctx), BF16)], axis=0)
        vb = jnp.concatenate([v_ref[0, kvh, :, pl.ds(ws, band)], jnp.ones((8, band), BF16)], axis=0)
        acc = jnp.dot(jnp.concatenate([vx, vb], axis=1), jnp.concatenate([p_ctx, p_band], axis=0),
                      preferred_element_type=F32)
        o = acc[0:HEAD_W] / (acc[HEAD_W:HEAD_W + 1] + jnp.exp(sink - m))
        for g in range(2):
            o_ref[0, r0 + g * HEAD_W:r0 + (g + 1) * HEAD_W, :] = o[:, g * tq:(g + 1) * tq]


def _window_attention(q, k, v, sink_col, l, tq, n_ctx):
    B, KVH, T, _ = k.shape
    G = q.shape[1] // (KVH * HEAD_W)
    return pl.pallas_call(
        functools.partial(_winattn_kernel, tq=tq, n_ctx=n_ctx, T=T),
        grid=(B, T // tq),
        in_specs=[pl.BlockSpec((None, KVH, 1, G * tq), lambda b, i: (l, 0, 0, 0)),
                  pl.BlockSpec((1, KVH * G * HEAD_W, tq), lambda b, i: (b, 0, i)),
                  pl.BlockSpec((1, KVH, T, HEAD_W), lambda b, i: (b, 0, 0, 0)),
                  pl.BlockSpec((1, KVH, HEAD_W, T), lambda b, i: (b, 0, 0, 0))],
        out_specs=pl.BlockSpec((1, KVH * G * HEAD_W, tq), lambda b, i: (b, 0, i)),
        out_shape=jax.ShapeDtypeStruct((B, KVH * G * HEAD_W, T), F32),
        compiler_params=_cparams(("parallel", "arbitrary")),
        name="window_attention",
    )(sink_col, q, k, v)


def _group_norm(x, ones2, eps, center):
    inv = 1.0 / HEAD_W
    if center:
        x = x - _group_sum(x, ones2) * inv
    return x * lax.rsqrt(_group_sum(x * x, ones2) * inv + eps)


def _merge_kernel(x_ref, gates_ref, ya_ref, ybf_ref, ybb_ref, aux_ref, ocf_ref, ocb_ref, gc_ref, yd_ref,
                  lnx_g_ref, lnx_b_ref, gn_ref, wb_ref, wo_ref, pn_ref, mod_ref, ones_ref, o_ref, *,
                  tm, n_ctx):
    W = BRANCH_W
    D = x_ref.shape[-1]
    ones_bd = ones_ref[...]
    yb = _group_norm(ybf_ref[0] + ybb_ref[0], ones_bd, RW_GN_EPS, True)
    yb = yb * lnx_g_ref[...] + lnx_b_ref[...]
    yb = (yb + aux_ref[0, :, 0:W]) * aux_ref[0, :, W:2 * W]
    yc = _group_norm(ocf_ref[0] + ocb_ref[0], ones_bd, 1e-5, True) * gn_ref[...]
    gc = gc_ref[0]
    yc = yc * (gc * _sigmoid(gc))

    lifted = [_dot_tn(ya_ref[0].astype(BF16), wb_ref[0]),
              jnp.dot(yb.astype(BF16), wb_ref[1], preferred_element_type=F32),
              jnp.dot(yc.astype(BF16), wb_ref[2], preferred_element_type=F32),
              _dot_tn(yd_ref[0].astype(BF16), wb_ref[3])]
    m = None
    for n in range(4):
        t = _sigmoid(gates_ref[0, :, n * D:(n + 1) * D].astype(F32)) * lifted[n]
        m = t if m is None else m + t
    out = jnp.dot(m.astype(BF16), wo_ref[...], preferred_element_type=F32)
    o_ref[0] = x_ref[0] + _mod_row(mod_ref, 2, tm, n_ctx) * (_rms(out, 1e-6) * pn_ref[...])


def _merge(xs, gates, ya, yb, aux, oc, cols_c, yd, prm, mods, l, tm, n_ctx):
    B, T, D = xs.shape
    W = BRANCH_W
    vec = lambda n: pl.BlockSpec((None, 1, n), lambda b, i: (l, 0, 0))
    tok = lambda n: pl.BlockSpec((1, tm, n), lambda b, i: (b, i, 0))
    feat = pl.BlockSpec((1, W, tm), lambda b, i: (b, 0, i))
    return pl.pallas_call(
        functools.partial(_merge_kernel, tm=tm, n_ctx=n_ctx),
        grid=(B, T // tm),
        in_specs=[tok(D), tok(4 * D), feat, tok(W), tok(W), tok(2 * W), tok(W), tok(W),
                  pl.BlockSpec((1, tm, W), lambda b, i: (b, i, 2)),
                  feat,
                  vec(W), vec(W), vec(W),
                  _const_spec((None, 4, W, D), lambda b, i: (l, 0, 0, 0)),
                  _const_spec((None, D, D), lambda b, i: (l, 0, 0)),
                  vec(D), _mod_spec(l, D),
                  _const_spec((2 * W, W), lambda b, i: (0, 0))],
        out_specs=tok(D),
        out_shape=jax.ShapeDtypeStruct((B, T, D), F32),
        compiler_params=_cparams(("parallel", "parallel")),
        name="merge",
    )(xs, gates, ya, yb[0], yb[1], aux, oc[0], oc[1], cols_c, yd, prm["lnx_g"], prm["lnx_b"], prm["ret_gn"],
      prm["w_branch"], prm["w_out"], prm["norm_post_mix"], mods, prm["ones_bd"])


def _mlp_kernel(x_ref, g_ref, pn_ref, mod_ref, wu_ref, wd_ref, o_ref, *, tm, n_ctx):
    x = x_ref[0]
    h = _rms(x, 1e-6) * g_ref[...]
    hb = (h * (1.0 + _mod_row(mod_ref, 4, tm, n_ctx)) + _mod_row(mod_ref, 3, tm, n_ctx)).astype(BF16)
    F = wu_ref.shape[-1]
    y = None
    for f0 in range(0, F, F // MLP_SPLIT):
        sl = slice(f0, f0 + F // MLP_SPLIT)
        u = jnp.dot(hb, wu_ref[:, sl], preferred_element_type=F32)
        u = jnp.square(jnp.maximum(u, 0.0)).astype(BF16)
        t = jnp.dot(u, wd_ref[sl, :], preferred_element_type=F32)
        y = t if y is None else y + t
    o_ref[0] = x + _mod_row(mod_ref, 5, tm, n_ctx) * (_rms(y, 1e-6) * pn_ref[...])


def _mlp(xs, prm, mods, l, tm, n_ctx):
    B, T, D = xs.shape
    F = prm["w_up"].shape[-1]
    vec = lambda n: pl.BlockSpec((None, 1, n), lambda b, i: (l, 0, 0))
    return pl.pallas_call(
        functools.partial(_mlp_kernel, tm=tm, n_ctx=n_ctx),
        grid=(B, T // tm),
        in_specs=[pl.BlockSpec((1, tm, D), lambda b, i: (b, i, 0)),
                  vec(D), vec(D), _mod_spec(l, D),
                  _const_spec((None, D, F), lambda b, i: (l, 0, 0)),
                  _const_spec((None, F, D), lambda b, i: (l, 0, 0))],
        out_specs=pl.BlockSpec((1, tm, D), lambda b, i: (b, i, 0)),
        out_shape=jax.ShapeDtypeStruct((B, T, D), F32),
        compiler_params=_cparams(("parallel", "parallel")),
        name="mlp",
    )(xs, prm["norm_pre_mlp"], prm["norm_post_mlp"], mods, prm["w_up"], prm["w_down"])


def kernel(x, c, ctx, c_ctx, ada_w, ada_b, norm_pre_mix, norm_post_mix, norm_pre_mlp, norm_post_mlp,
           w_in, diff_lam_q, diff_lam_k, diff_subln, rwkv_mu, rwkv_w0, rwkv_w2, rwkv_a0, rwkv_a2,
           rwkv_g2, rwkv_kk, rwkv_ka, rwkv_rk, rwkv_lnx_g, rwkv_lnx_b, ret_decay, ret_gn, win_sink,
           w_branch, w_out, w_up, w_down):
    B, S, D = x.shape
    n_ctx = ctx.shape[1]
    T = n_ctx + S
    L = ada_w.shape[0]
    W = BRANCH_W
    TM = 256
    TD = next(t for t in DENSE_TILES if T % t == 0)
    assert n_ctx % TM == 0 and S % TM == 0 and S % GRID_W == 0 and B + 1 <= 8

    cc = jnp.zeros((8, D), F32).at[:B].set(c).at[B].set(c_ctx)
    mod = _ada_mod(cc, ada_w, ada_b).reshape(L, 8, 6, D)
    mod_ctx = jnp.broadcast_to(mod[:, B][:, None], (L, B, 6, D))
    mods = jnp.stack([mod_ctx, mod[:, :B]], axis=2)
    mods = jnp.pad(mods, ((0, 0), (0, 0), (0, 0), (0, 2), (0, 0)))

    lat = jnp.arange(S, dtype=jnp.int32)
    row = (lat // GRID_W).astype(F32)
    col = (lat % GRID_W).astype(F32)
    tabs_a = _rope_tables([row, col], W, HEAD_W // 2, n_ctx)
    tabs_d = _rope_tables([row, col], W, HEAD_W, n_ctx)
    tabs_c = _rope_tables([lat.astype(F32)], W // 2, HEAD_W // 2, n_ctx)

    sizes = (3 * W, rwkv_mu.shape[-1], 3 * W, 2 * W, 4 * D)
    offs = np.concatenate([[0], np.cumsum(sizes)])
    w_groups = [w_in[:, :, int(offs[n]):int(offs[n + 1])].astype(BF16) for n in range(5)]

    lw_w, la_w = rwkv_w2.shape[2], rwkv_a2.shape[2]
    zw = jnp.zeros((L, W, 2 * W), F32)
    w2cat = (zw.at[:, 0:lw_w, 0:W].set(rwkv_w2[:, 0])
             .at[:, lw_w:2 * lw_w, W:2 * W].set(rwkv_w2[:, 1]))
    a2cat = (zw.at[:, 2 * lw_w:2 * lw_w + la_w, 0:W].set(rwkv_a2[:, 0])
             .at[:, 2 * lw_w + la_w:2 * lw_w + 2 * la_w, W:2 * W].set(rwkv_a2[:, 1]))
    blk = np.arange(W) // HEAD_W
    ones_bd = (blk[:, None] == blk[None, :]).astype(np.float32)
    ones_bd = jnp.asarray(np.concatenate([ones_bd, ones_bd], axis=0), BF16)
    v3 = lambda a: a.reshape(L, 1, -1)

    def hl3(w):
        hi = w.astype(BF16)
        return jnp.concatenate([hi, (w - hi.astype(F32)).astype(BF16), hi], axis=-2)

    prm = dict(mu=v3(rwkv_mu), kk=v3(rwkv_kk), ka=v3(rwkv_ka), rk=v3(rwkv_rk), w0=rwkv_w0, a0=rwkv_a0,
               w2=hl3(w2cat), a2=hl3(a2cat), g2=hl3(rwkv_g2), ones_bd=ones_bd,
               lnx_g=v3(rwkv_lnx_g), lnx_b=v3(rwkv_lnx_b), ret_gn=v3(ret_gn),
               w_branch=w_branch.astype(BF16), w_out=w_out.astype(BF16),
               norm_post_mix=v3(norm_post_mix), norm_pre_mlp=v3(norm_pre_mlp),
               norm_post_mlp=v3(norm_post_mlp), w_up=w_up.astype(BF16), w_down=w_down.astype(BF16))
    subln = diff_subln.reshape(L, N_HEADS, HEAD_W, 1)
    decay = ret_decay.reshape(L, 2, 1, N_HEADS)
    TQ_D = 256
    sink_col = jnp.broadcast_to(win_sink.reshape(L, 2, 1, 2, 1), (L, 2, 1, 2, TQ_D)).reshape(L, 2, 1, 2 * TQ_D)
    gain_pre = v3(norm_pre_mix)

    xs = jnp.concatenate([ctx, x], axis=1)
    for l in range(L):
        qa, ka, va, cols_b, cols_c, qd, kd, vd, gates = _input_proj(
            xs, gain_pre, mods, w_groups, tabs_a, tabs_d, l, TD, n_ctx)
        ya = _diff_attention(qa, ka, va, diff_lam_q, diff_lam_k, subln, l, ATT_TQ, ATT_TK, ATT_UNROLL, n_ctx)
        rvk, dirs, aux = _rwkv_prep(cols_b, prm, l, TM, n_ctx)
        yb = _rwkv_scan(rvk, dirs, n_ctx)
        oc = _retention(cols_c, tabs_c, decay, l, n_ctx)
        yd = _window_attention(qd, kd, vd, sink_col, l, TQ_D, n_ctx)
        xs = _merge(xs, gates, ya, yb, aux, oc, cols_c, yd, prm, mods, l, TD, n_ctx)
        xs = _mlp(xs, prm, mods, l, TD, n_ctx)
    return xs[:, n_ctx:]
```
